```python
import jax, jax.numpy as jnp
from jax import lax
import numpy as np

D_MODEL = 2048
BATCH = 16
SEQ = 2048
DEPTH = 2

HEAD_DIM = 64
D_MIX = D_MODEL
GROUP_WIDTH = D_MIX // 4
NORM_EPS = 1e-6
D_FF = 5632
FFN_RES_WEIGHT = 0.5

RWKV_HEADS = GROUP_WIDTH // HEAD_DIM
RWKV_W_RANK = 64
RWKV_A_RANK = 64
RWKV_G_RANK = 128
RWKV_LN_EPS = 64e-5

ATTN_HEADS = GROUP_WIDTH // HEAD_DIM
ATTN_KV_HEADS = 2
ATTN_WINDOW = 128
ATTN_BLOCK = 128
ROPE_THETA = 500000.0
ROPE_DIM = HEAD_DIM // 4

S5_CHANNELS = GROUP_WIDTH
S5_GROUP = 16
S5_GROUPS = S5_CHANNELS // S5_GROUP
S5_STATE = 64
S5_DT_MIN = 0.001
S5_DT_MAX = 0.1

HGRN_HEADS = GROUP_WIDTH // HEAD_DIM
HGRN_DK = 64
HGRN_DV = GROUP_WIDTH // HGRN_HEADS
HGRN_CHUNK = 16

RWKV_IN = 3 * GROUP_WIDTH + RWKV_W_RANK + RWKV_A_RANK + RWKV_G_RANK
ATTN_IN = (ATTN_HEADS + 2 * ATTN_KV_HEADS) * HEAD_DIM
S5_IN = S5_CHANNELS
HGRN_IN = 2 * HGRN_HEADS * HGRN_DK + 2 * HGRN_HEADS * HGRN_DV
D_IN = RWKV_IN + ATTN_IN + S5_IN + HGRN_IN

kernel_name = 'hybrid_parallel_head_groups_trunk'


def rmsnorm(x, g, eps=NORM_EPS):
    xf = x.astype(jnp.float32)
    y = xf * lax.rsqrt(jnp.mean(xf * xf, axis=-1, keepdims=True) + eps)
    return (y * g.astype(jnp.float32)).astype(x.dtype)


def swiglu_ffn(x, norm_g, w_gate, w_up, w_down):
    h = rmsnorm(x, norm_g)
    return (jax.nn.silu(h @ w_gate) * (h @ w_up)) @ w_down


def token_shift(p):
    return jnp.pad(p, ((0, 0), (1, 0), (0, 0)))[:, :-1]


def rwkv7_time_mix(p, mu, w0, w_up, a0, a_up, g_up, k_k, k_a, r_k, ln_w, ln_b):
    B_, S_, _ = p.shape
    H, N = RWKV_HEADS, HEAD_DIM
    f32 = jnp.float32
    p = p + (token_shift(p) - p) * mu
    cut = [GROUP_WIDTH, 2 * GROUP_WIDTH, 3 * GROUP_WIDTH,
           3 * GROUP_WIDTH + RWKV_W_RANK, 3 * GROUP_WIDTH + RWKV_W_RANK + RWKV_A_RANK]
    r, k, v, w_lo, a_lo, g_lo = jnp.split(p, cut, axis=-1)
    w_log = -jax.nn.softplus(-(w0 + jnp.tanh(w_lo) @ w_up)) - 0.5
    a = jax.nn.sigmoid(a0 + a_lo @ a_up)
    g = jax.nn.sigmoid(g_lo) @ g_up
    heads = lambda t: t.astype(f32).reshape(B_, S_, H, N)
    decay = jnp.exp(-jnp.exp(heads(w_log)))
    a = heads(a)
    kk = heads(k * k_k)
    kk = kk * lax.rsqrt(jnp.maximum(jnp.sum(kk * kk, axis=-1, keepdims=True), 1e-24))
    r, v = heads(r), heads(v)
    k = heads(k) * (1.0 + (a - 1.0) * k_a.astype(f32).reshape(H, N))
    xs = tuple(jnp.moveaxis(t, 1, 0) for t in (r, decay, k, v, kk, kk * a))

    def step(state, inp):
        r_t, w_t, k_t, v_t, kk_t, b_t = inp
        sa = jnp.einsum('bhij,bhj->bhi', state, -kk_t)
        state = (state * w_t[:, :, None, :] + sa[..., None] * b_t[:, :, None, :]
                 + v_t[..., None] * k_t[:, :, None, :])
        return state, jnp.einsum('bhij,bhj->bhi', state, r_t)

    _, ys = lax.scan(step, jnp.zeros((B_, H, N, N), f32), xs)
    o = jnp.moveaxis(ys, 0, 1)
    mean = jnp.mean(o, axis=-1, keepdims=True)
    var = jnp.mean(jnp.square(o - mean), axis=-1, keepdims=True)
    o = ((o - mean) * lax.rsqrt(var + RWKV_LN_EPS)).reshape(B_, S_, H * N) * ln_w + ln_b
    bonus = jnp.sum(r * k * r_k.astype(f32), axis=-1, keepdims=True) * v
    return ((o + bonus.reshape(B_, S_, H * N)) * g).astype(p.dtype)


def apply_partial_rope(x, cos, sin):
    half = ROPE_DIM // 2
    x1, x2, rest = x[..., :half], x[..., half:ROPE_DIM], x[..., ROPE_DIM:]
    return jnp.concatenate([x1 * cos - x2 * sin, x2 * cos + x1 * sin, rest], axis=-1)


def swa_sink_attention(p, positions, q_norm, k_norm, sinks):
    B_, S_, _ = p.shape
    H, KV, D = ATTN_HEADS, ATTN_KV_HEADS, HEAD_DIM
    G = H // KV
    f32 = jnp.float32
    q, k, v = jnp.split(p, [H * D, (H + KV) * D], axis=-1)
    q = rmsnorm(q.reshape(B_, S_, H, D), q_norm).astype(f32)
    k = rmsnorm(k.reshape(B_, S_, KV, D), k_norm).astype(f32)
    v = v.reshape(B_, S_, KV, D).astype(f32)
    inv_freq = ROPE_THETA ** (-jnp.arange(0, ROPE_DIM, 2, dtype=f32) / ROPE_DIM)
    ang = positions.astype(f32)[..., None] * inv_freq
    cos, sin = jnp.cos(ang)[:, :, None, :], jnp.sin(ang)[:, :, None, :]
    q = apply_partial_rope(q, cos, sin)
    k = apply_partial_rope(k, cos, sin)
    nb = S_ // ATTN_BLOCK
    qb = q.reshape(B_, nb, ATTN_BLOCK, KV, G, D) * (D ** -0.5)
    kb = k.reshape(B_, nb, ATTN_BLOCK, KV, D)
    vb = v.reshape(B_, nb, ATTN_BLOCK, KV, D)
    shift = lambda t: jnp.pad(t, ((0, 0), (1, 0), (0, 0), (0, 0), (0, 0)))[:, :-1]
    kcat = jnp.concatenate([shift(kb), kb], axis=2)
    vcat = jnp.concatenate([shift(vb), vb], axis=2)
    s = jnp.einsum('bnqhgd,bnkhd->bnhgqk', qb, kcat)
    blk = jnp.arange(nb)[:, None] * ATTN_BLOCK
    qpos = blk + jnp.arange(ATTN_BLOCK)[None, :]
    kpos = blk - ATTN_BLOCK + jnp.arange(2 * ATTN_BLOCK)[None, :]
    diff = qpos[:, :, None] - kpos[:, None, :]
    mask = (diff >= 0) & (diff < ATTN_WINDOW) & (kpos[:, None, :] >= 0)
    s = jnp.where(mask[None, :, None, None], s, -jnp.inf)
    sink = jnp.broadcast_to(sinks.astype(f32).reshape(1, 1, KV, G, 1, 1), s.shape[:-1] + (1,))
    probs = jax.nn.softmax(jnp.concatenate([s, sink], axis=-1), axis=-1)[..., :-1]
    o = jnp.einsum('bnhgqk,bnkhd->bnqhgd', probs, vcat)
    return o.reshape(B_, S_, H * D).astype(p.dtype)


def _linear_combine(e1, e2):
    a1, b1 = e1
    a2, b2 = e2
    return a1 * a2, a2 * b1 + b2


def s5_mix(u, lam_re, lam_im, log_step, b_re, b_im, c_re, c_im, d_skip, glu_w, glu_b):
    B_, S_, _ = u.shape
    f32 = jnp.float32
    uf = u.astype(f32)
    lam = lax.complex(lam_re.astype(f32), lam_im.astype(f32))
    dt = jnp.exp(log_step.astype(f32))[:, None]
    lam_bar = jnp.exp(lam * dt)
    b_bar = ((lam_bar - 1.0) / lam)[..., None] * lax.complex(b_re.astype(f32), b_im.astype(f32))
    c = lax.complex(c_re.astype(f32), c_im.astype(f32))
    ug = uf.reshape(B_, S_, S5_GROUPS, S5_GROUP)
    bu = jnp.einsum('gph,bsgh->bsgp', b_bar, ug.astype(jnp.complex64))
    a = jnp.broadcast_to(lam_bar, (1, S_, S5_GROUPS, S5_STATE))
    _, states = lax.associative_scan(_linear_combine, (a, bu), axis=1)
    y = jnp.einsum('ghp,bsgp->bsgh', c, states).real.reshape(B_, S_, S5_CHANNELS)
    y = y + d_skip.astype(f32) * uf
    z = jax.nn.gelu(y, approximate=False)
    out = z * jax.nn.sigmoid(z @ glu_w.astype(f32) + glu_b.astype(f32))
    return out.astype(u.dtype)


def hgrn2_mix(p, lower_bound, g_norm):
    B_, S_, _ = p.shape
    H, DK, DV, C = HGRN_HEADS, HGRN_DK, HGRN_DV, HGRN_CHUNK
    f32 = jnp.float32
    nc = S_ // C
    q, f, i, g = jnp.split(p, [H * DK, 2 * H * DK, 2 * H * DK + H * DV], axis=-1)
    q = jax.nn.silu(q.astype(f32)).reshape(B_, S_, H, DK) * (DK ** -0.5)
    lb = lower_bound.astype(f32).reshape(H, DK)
    f_gate = lb + (1.0 - lb) * jax.nn.sigmoid(f.astype(f32).reshape(B_, S_, H, DK))
    log_f = jnp.log(f_gate)
    k = 1.0 - f_gate
    v = i.astype(f32).reshape(B_, S_, H, DV)
    to_chunks = lambda t: t.reshape(B_, nc, C, H, t.shape[-1]).transpose(0, 3, 1, 2, 4)
    qc, kc, vc, lfc = to_chunks(q), to_chunks(k), to_chunks(v), to_chunks(log_f)
    b = jnp.cumsum(lfc, axis=3)
    bm = b[:, :, :, C // 2 - 1:C // 2]
    att = jnp.einsum('bhnid,bhnjd->bhnij', qc * jnp.exp(b - bm), kc * jnp.exp(bm - b))
    tri = jnp.tril(jnp.ones((C, C), dtype=bool))
    o_intra = jnp.einsum('bhnij,bhnjv->bhniv', jnp.where(tri, att, 0.0), vc)
    b_last = b[:, :, :, -1]
    kv = jnp.einsum('bhnjd,bhnjv->bhndv', kc * jnp.exp(b_last[:, :, :, None] - b), vc)

    def step(state, inp):
        dec, kv_n = inp
        return state * dec[..., None] + kv_n, state

    _, s_prev = lax.scan(step, jnp.zeros((B_, H, DK, DV), f32),
                         (jnp.moveaxis(jnp.exp(b_last), 2, 0), jnp.moveaxis(kv, 2, 0)))
    o_inter = jnp.einsum('bhnid,nbhdv->bhniv', qc * jnp.exp(b), s_prev)
    o = (o_intra + o_inter).transpose(0, 2, 3, 1, 4).reshape(B_, S_, H, DV)
    o = rmsnorm(o, g_norm) * jax.nn.silu(g.astype(f32).reshape(B_, S_, H, DV))
    return o.reshape(B_, S_, H * DV).astype(p.dtype)


def setup_inputs(seed: int = 0) -> dict:
    key = jax.random.key(seed)
    ks = iter(jax.random.split(key, 40))
    f32 = jnp.float32
    L, D, F, GW = DEPTH, D_MODEL, D_FF, GROUP_WIDTH
    nrm = lambda shape, scale: scale * jax.random.normal(next(ks), shape, f32)
    gain = lambda shape: 1.0 + 0.05 * jax.random.normal(next(ks), shape, f32)
    uni = lambda shape, lo, hi: jax.random.uniform(next(ks), shape, f32, lo, hi)
    x = jax.random.normal(next(ks), (BATCH, SEQ, D), f32)
    positions = (jnp.arange(SEQ, dtype=jnp.int32)[None, :]
                 + jax.random.randint(next(ks), (BATCH, 1), 0, 4096, dtype=jnp.int32))
    return {
        'x': x,
        'positions': positions,
        'ffn1_norm': gain((L, D)),
        'ffn1_w_gate': nrm((L, D, F), D ** -0.5),
        'ffn1_w_up': nrm((L, D, F), D ** -0.5),
        'ffn1_w_down': nrm((L, F, D), F ** -0.5),
        'mix_norm': gain((L, D)),
        'w_in': nrm((L, D, D_IN), D ** -0.5),
        'rwkv_mu': uni((L, RWKV_IN), 0.0, 1.0),
        'rwkv_w0': uni((L, GW), -6.5, -1.5),
        'rwkv_w_up': nrm((L, RWKV_W_RANK, GW), 0.5 * RWKV_W_RANK ** -0.5),
        'rwkv_a0': nrm((L, GW), 0.1),
        'rwkv_a_up': nrm((L, RWKV_A_RANK, GW), RWKV_A_RANK ** -0.5),
        'rwkv_g_up': nrm((L, RWKV_G_RANK, GW), RWKV_G_RANK ** -0.5),
        'rwkv_k_k': 0.85 + nrm((L, GW), 0.05),
        'rwkv_k_a': 1.0 + nrm((L, GW), 0.05),
        'rwkv_r_k': nrm((L, RWKV_HEADS, HEAD_DIM), 0.1),
        'rwkv_ln_w': gain((L, GW)),
        'rwkv_ln_b': nrm((L, GW), 0.02),
        'attn_q_norm': gain((L, HEAD_DIM)),
        'attn_k_norm': gain((L, HEAD_DIM)),
        'attn_sinks': nrm((L, ATTN_HEADS), 0.5),
        's5_lambda_re': -0.5 + nrm((L, S5_GROUPS, S5_STATE), 0.01),
        's5_lambda_im': (jnp.pi * jnp.arange(S5_STATE, dtype=f32))[None, None, :] + nrm((L, S5_GROUPS, S5_STATE), 0.01),
        's5_log_step': uni((L, S5_GROUPS), float(np.log(S5_DT_MIN)), float(np.log(S5_DT_MAX))),
        's5_b_re': nrm((L, S5_GROUPS, S5_STATE, S5_GROUP), (2 * S5_GROUP) ** -0.5),
        's5_b_im': nrm((L, S5_GROUPS, S5_STATE, S5_GROUP), (2 * S5_GROUP) ** -0.5),
        's5_c_re': nrm((L, S5_GROUPS, S5_GROUP, S5_STATE), (2 * S5_STATE) ** -0.5),
        's5_c_im': nrm((L, S5_GROUPS, S5_GROUP, S5_STATE), (2 * S5_STATE) ** -0.5),
        's5_d': nrm((L, S5_CHANNELS), 0.5),
        's5_glu_w': nrm((L, S5_CHANNELS, S5_CHANNELS), S5_CHANNELS ** -0.5),
        's5_glu_b': nrm((L, S5_CHANNELS), 0.02),
        'hgrn_lower_bounds': nrm((L, HGRN_HEADS * HGRN_DK), 0.1),
        'hgrn_g_norm': gain((L, HGRN_DV)),
        'w_out': nrm((L, D_MIX, D), D_MIX ** -0.5),
        'ffn2_norm': gain((L, D)),
        'ffn2_w_gate': nrm((L, D, F), D ** -0.5),
        'ffn2_w_up': nrm((L, D, F), D ** -0.5),
        'ffn2_w_down': nrm((L, F, D), F ** -0.5),
    }


def reference(x, positions, ffn1_norm, ffn1_w_gate, ffn1_w_up, ffn1_w_down, mix_norm, w_in,
              rwkv_mu, rwkv_w0, rwkv_w_up, rwkv_a0, rwkv_a_up, rwkv_g_up, rwkv_k_k, rwkv_k_a,
              rwkv_r_k, rwkv_ln_w, rwkv_ln_b, attn_q_norm, attn_k_norm, attn_sinks,
              s5_lambda_re, s5_lambda_im, s5_log_step, s5_b_re, s5_b_im, s5_c_re, s5_c_im,
              s5_d, s5_glu_w, s5_glu_b, hgrn_lower_bounds, hgrn_g_norm, w_out,
              ffn2_norm, ffn2_w_gate, ffn2_w_up, ffn2_w_down):
    lb_all = jnp.cumsum(jax.nn.softmax(hgrn_lower_bounds.astype(jnp.float32), axis=0), axis=0)
    lb_all = lb_all - lb_all[0]
    cut = [RWKV_IN, RWKV_IN + ATTN_IN, RWKV_IN + ATTN_IN + S5_IN]
    for l in range(DEPTH):
        x = x + FFN_RES_WEIGHT * swiglu_ffn(x, ffn1_norm[l], ffn1_w_gate[l], ffn1_w_up[l], ffn1_w_down[l])
        h = rmsnorm(x, mix_norm[l])
        proj = h @ w_in[l]
        p_a, p_b, p_c, p_d = jnp.split(proj, cut, axis=-1)
        y_a = rwkv7_time_mix(p_a, rwkv_mu[l], rwkv_w0[l], rwkv_w_up[l], rwkv_a0[l], rwkv_a_up[l],
                             rwkv_g_up[l], rwkv_k_k[l], rwkv_k_a[l], rwkv_r_k[l], rwkv_ln_w[l], rwkv_ln_b[l])
        y_b = swa_sink_attention(p_b, positions, attn_q_norm[l], attn_k_norm[l], attn_sinks[l])
        y_c = s5_mix(p_c, s5_lambda_re[l], s5_lambda_im[l], s5_log_step[l], s5_b_re[l], s5_b_im[l],
                     s5_c_re[l], s5_c_im[l], s5_d[l], s5_glu_w[l], s5_glu_b[l])
        y_d = hgrn2_mix(p_d, lb_all[l], hgrn_g_norm[l])
        x = x + jnp.concatenate([y_a, y_b, y_c, y_d], axis=-1) @ w_out[l]
        x = x + FFN_RES_WEIGHT * swiglu_ffn(x, ffn2_norm[l], ffn2_w_gate[l], ffn2_w_up[l], ffn2_w_down[l])
    return x
```

```python
import functools

import jax
import jax.numpy as jnp
import numpy as np
from jax import lax
from jax.experimental import pallas as pl
from jax.experimental.pallas import tpu as pltpu

F32 = jnp.float32
BF16 = jnp.bfloat16

HEAD_DIM = 64
GROUP_WIDTH = 512
NORM_EPS = 1e-6
FFN_RES_WEIGHT = 0.5

RWKV_W_RANK = 64
RWKV_A_RANK = 64
RWKV_G_RANK = 128
RWKV_LN_EPS = 64e-5
RWKV_IN = 3 * GROUP_WIDTH + RWKV_W_RANK + RWKV_A_RANK + RWKV_G_RANK
RWKV_CHUNK = 64

ATTN_HEADS = 8
ATTN_KV_HEADS = 2
ATTN_BLOCK = 128
ROPE_THETA = 500000.0
ROPE_DIM = HEAD_DIM // 4
ATTN_COL_BLOCK = 256
ATTN_OFF_BLOCKS = RWKV_IN // ATTN_COL_BLOCK

S5_GROUP = 16
S5_GROUPS = GROUP_WIDTH // S5_GROUP
S5_STATE = 64
S5_WIDTH = S5_GROUPS * S5_STATE
S5_OFF_BLOCKS = (RWKV_IN + 768) // GROUP_WIDTH
S5_STRIP = 512

HGRN_CHUNK = 16
HGRN_TILE = 128
HGRN_OFF_BLOCKS = 3

D_IN = 5120
VMEM_LIMIT = 56 * 1024 * 1024


def _dot(a, b, dims):
    return lax.dot_general(a, b, (dims, ((), ())), preferred_element_type=F32)


_NN = ((1,), (0,))
_NT = ((1,), (1,))
_TN = ((0,), (0,))


def _mm(a, b, dims=_NN):
    return _dot(a.astype(BF16), b.astype(BF16), dims)


def _split2(x):
    hi = x.astype(BF16)
    lo = (x - hi.astype(F32)).astype(BF16)
    return hi, lo


def _split3(x):
    hi = x.astype(BF16)
    r = x - hi.astype(F32)
    mid = r.astype(BF16)
    lo = (r - mid.astype(F32)).astype(BF16)
    return hi, mid, lo


def _mm_hi(a, b, dims=_NN):
    ah, al = _split2(a)
    bh, bl = _split2(b)
    return _dot(ah, bh, dims) + (_dot(al, bh, dims) + _dot(ah, bl, dims))


def _const_lhs_mm(c, x):
    hi, mid, lo = _split3(x)
    return _dot(c, hi, _NN) + (_dot(c, mid, _NN) + _dot(c, lo, _NN))


def _const_rhs_mm(x, c):
    hi, mid, lo = _split3(x)
    return _dot(hi, c, _NN) + (_dot(mid, c, _NN) + _dot(lo, c, _NN))


def _head_sum(x, ones_bd):
    w = ones_bd.shape[0]
    parts = [_const_rhs_mm(x[:, i:i + w], ones_bd) for i in range(0, x.shape[1], w)]
    return parts[0] if len(parts) == 1 else jnp.concatenate(parts, axis=1)


def _silu(x):
    return x * jax.nn.sigmoid(x)


def _rms_rows(x, gain):
    ms = jnp.mean(x * x, axis=-1, keepdims=True)
    return x * lax.rsqrt(ms + NORM_EPS) * gain


def _ffn_kernel(x_ref, g_ref, wg_ref, wu_ref, wd_ref, o_ref, h_ref):
    @pl.when(pl.program_id(1) == 0)
    def _():
        x = x_ref[...]
        h_ref[...] = _rms_rows(x, g_ref[...]).astype(BF16)
        o_ref[...] = x

    h = h_ref[...]
    gate = jnp.dot(h, wg_ref[...], preferred_element_type=F32)
    up = jnp.dot(h, wu_ref[...], preferred_element_type=F32)
    act = (_silu(gate) * up).astype(BF16)
    o_ref[...] += FFN_RES_WEIGHT * jnp.dot(act, wd_ref[...], preferred_element_type=F32)


def _ffn(x2d, gain, wg, wu, wd):
    n, d = x2d.shape
    f = wg.shape[1]
    tm = min(512, n)
    tf = 512 if f % 512 == 0 else f
    return pl.pallas_call(
        _ffn_kernel,
        grid=(n // tm, f // tf),
        in_specs=[
            pl.BlockSpec((tm, d), lambda i, j: (i, 0)),
            pl.BlockSpec((1, d), lambda i, j: (0, 0)),
            pl.BlockSpec((d, tf), lambda i, j: (0, j)),
            pl.BlockSpec((d, tf), lambda i, j: (0, j)),
            pl.BlockSpec((tf, d), lambda i, j: (j, 0)),
        ],
        out_specs=pl.BlockSpec((tm, d), lambda i, j: (i, 0)),
        out_shape=jax.ShapeDtypeStruct((n, d), F32),
        scratch_shapes=[pltpu.VMEM((tm, d), BF16)],
        compiler_params=pltpu.CompilerParams(
            dimension_semantics=("parallel", "arbitrary"), vmem_limit_bytes=VMEM_LIMIT),
        name="ffn",
    )(x2d, gain.reshape(1, d), wg, wu, wd)


def _proj_kernel(x_ref, g_ref, w_ref, o_ref, h_ref):
    @pl.when(pl.program_id(1) == 0)
    def _():
        h_ref[...] = _rms_rows(x_ref[...], g_ref[...]).astype(BF16)

    o_ref[...] = jnp.dot(h_ref[...], w_ref[...], preferred_element_type=F32)


def _proj(x2d, gain, w):
    n, d = x2d.shape
    dout = w.shape[1]
    tm = min(512, n)
    tn = 1024
    return pl.pallas_call(
        _proj_kernel,
        grid=(n // tm, dout // tn),
        in_specs=[
            pl.BlockSpec((tm, d), lambda i, j: (i, 0)),
            pl.BlockSpec((1, d), lambda i, j: (0, 0)),
            pl.BlockSpec((d, tn), lambda i, j: (0, j)),
        ],
        out_specs=pl.BlockSpec((tm, tn), lambda i, j: (i, j)),
        out_shape=jax.ShapeDtypeStruct((n, dout), F32),
        scratch_shapes=[pltpu.VMEM((tm, d), BF16)],
        compiler_params=pltpu.CompilerParams(
            dimension_semantics=("parallel", "arbitrary"), vmem_limit_bytes=VMEM_LIMIT),
        name="in_proj",
    )(x2d, gain.reshape(1, d), w)


def _outproj_kernel(x_ref, ya_ref, yb_ref, yc_ref, yd_ref, w_ref, o_ref):
    gw = GROUP_WIDTH
    acc = x_ref[...]
    for m, y_ref in enumerate((ya_ref, yb_ref, yc_ref, yd_ref)):
        acc = acc + jnp.dot(y_ref[...].astype(BF16), w_ref[m * gw:(m + 1) * gw, :],
                            preferred_element_type=F32)
    o_ref[...] = acc


def _outproj(x2d, ys, w):
    n, d = x2d.shape
    tm = min(256, n)
    yspec = pl.BlockSpec((tm, GROUP_WIDTH), lambda i: (i, 0))
    return pl.pallas_call(
        _outproj_kernel,
        grid=(n // tm,),
        in_specs=[pl.BlockSpec((tm, d), lambda i: (i, 0)), yspec, yspec, yspec, yspec,
                  pl.BlockSpec(w.shape, lambda i: (0, 0))],
        out_specs=pl.BlockSpec((tm, d), lambda i: (i, 0)),
        out_shape=jax.ShapeDtypeStruct((n, d), F32),
        compiler_params=pltpu.CompilerParams(
            dimension_semantics=("parallel",), vmem_limit_bytes=VMEM_LIMIT),
        name="out_proj",
    )(x2d, *ys, w)


def _rwkv_kernel(p_ref, mu_ref, w0_ref, wup_ref, a0_ref, aup_ref, gup_ref, kk_ref, ka_ref, rk_ref,
                 lnw_ref, lnb_ref, tri_ref, bd_ref, o_ref, s_ref, prev_ref, osc_ref):
    c = RWKV_CHUNK
    gw = GROUP_WIDTH
    n = HEAD_DIM

    @pl.when(pl.program_id(1) == 0)
    def _():
        s_ref[...] = jnp.zeros_like(s_ref)
        prev_ref[...] = jnp.zeros_like(prev_ref)

    p = p_ref[...]
    row = lax.broadcasted_iota(jnp.int32, p.shape, 0)
    shifted = jnp.where(row == 0, prev_ref[...], pltpu.roll(p, 1, axis=0))
    prev_ref[...] = p[c - 1:c, :]
    p = p + (shifted - p) * mu_ref[...]

    r = p[:, 0:gw]
    k = p[:, gw:2 * gw]
    v = p[:, 2 * gw:3 * gw]
    o1 = 3 * gw
    w_lo = p[:, o1:o1 + RWKV_W_RANK]
    a_lo = p[:, o1 + RWKV_W_RANK:o1 + RWKV_W_RANK + RWKV_A_RANK]
    g_lo = p[:, o1 + RWKV_W_RANK + RWKV_A_RANK:]

    z = -(w0_ref[...] + _mm(jnp.tanh(w_lo), wup_ref[...]))
    softplus = jnp.maximum(z, 0.0) + jnp.log1p(jnp.exp(-jnp.abs(z)))
    lw = -jnp.exp(-softplus - 0.5)
    a = jax.nn.sigmoid(a0_ref[...] + _mm(a_lo, aup_ref[...]))
    g = _mm(jax.nn.sigmoid(g_lo), gup_ref[...])

    ones_bd = bd_ref[...]
    kk = k * kk_ref[...]
    kk = kk * lax.rsqrt(jnp.maximum(_head_sum(kk * kk, ones_bd), 1e-24))
    k = k * (1.0 + (a - 1.0) * ka_ref[...])
    b = kk * a

    cum = _const_lhs_mm(tri_ref[...], lw)
    cum_last = cum[c - 1:c, :]
    e_pos = jnp.exp(cum)
    e_neg = jnp.exp(-cum)
    e_end = jnp.exp(cum_last - cum)
    g_end = jnp.exp(cum_last)
    kq = kk * jnp.exp(cum - lw)
    rq = r * e_pos
    bd = b * e_neg
    kd = k * e_neg
    b_end = b * e_end
    k_end = k * e_end

    ri = lax.broadcasted_iota(jnp.int32, (c, c), 0)
    ci = lax.broadcasted_iota(jnp.int32, (c, c), 1)
    strict = ri > ci
    incl = ri >= ci
    eye = (ri == ci).astype(F32)

    for h in range(gw // n):
        sl = slice(h * n, (h + 1) * n)
        kq_h, rq_h, bd_h, kd_h, v_h = kq[:, sl], rq[:, sl], bd[:, sl], kd[:, sl], v[:, sl]
        s0 = s_ref[h]
        a_bb = jnp.where(strict, _mm_hi(kq_h, bd_h, _NT), 0.0)
        a_bk = jnp.where(strict, _mm_hi(kq_h, kd_h, _NT), 0.0)
        a_rb = jnp.where(incl, _mm_hi(rq_h, bd_h, _NT), 0.0)
        a_rk = jnp.where(incl, _mm_hi(rq_h, kd_h, _NT), 0.0)
        pw = -a_bb
        t_inv = eye + pw
        for _ in range(int(np.log2(c)) - 1):
            pw = _mm_hi(pw, pw)
            t_inv = t_inv + _mm_hi(t_inv, pw)
        x = _mm_hi(kq_h, s0, _NT) + _mm_hi(a_bk, v_h)
        u = -_mm_hi(t_inv, x)
        y = _mm_hi(rq_h, s0, _NT) + _mm_hi(a_rb, u) + _mm_hi(a_rk, v_h)
        s_ref[h] = s0 * g_end[:, sl] + _mm_hi(u, b_end[:, sl], _TN) + _mm_hi(v_h, k_end[:, sl], _TN)
        osc_ref[:, sl] = y

    o = osc_ref[...]
    inv_n = 1.0 / n
    mean = _head_sum(o, ones_bd) * inv_n
    dlt = o - mean
    var = _head_sum(dlt * dlt, ones_bd) * inv_n
    o = dlt * lax.rsqrt(var + RWKV_LN_EPS) * lnw_ref[...] + lnb_ref[...]
    bonus = _head_sum(r * k * rk_ref[...], ones_bd) * v
    o_ref[...] = (o + bonus) * g


def _rwkv(p3, mu, w0, w_up, a0, a_up, g_up, k_k, k_a, r_k, ln_w, ln_b):
    bsz, seq, _ = p3.shape
    c = RWKV_CHUNK
    gw = GROUP_WIDTH
    tri = jnp.asarray(np.tril(np.ones((c, c), np.float32)), BF16)
    ones_bd = jnp.asarray(np.kron(np.eye(4, dtype=np.float32), np.ones((HEAD_DIM, HEAD_DIM), np.float32)), BF16)
    row = lambda t: t.reshape(1, -1)
    full = lambda t: pl.BlockSpec(t.shape, lambda i, j: (0,) * t.ndim)
    params = [row(mu), row(w0), w_up, row(a0), a_up, g_up, row(k_k), row(k_a), row(r_k), row(ln_w),
              row(ln_b), tri, ones_bd]
    return pl.pallas_call(
        _rwkv_kernel,
        grid=(bsz, seq // c),
        in_specs=[pl.BlockSpec((None, c, RWKV_IN), lambda i, j: (i, j, 0))] + [full(t) for t in params],
        out_specs=pl.BlockSpec((None, c, gw), lambda i, j: (i, j, 0)),
        out_shape=jax.ShapeDtypeStruct((bsz, seq, gw), F32),
        scratch_shapes=[pltpu.VMEM((gw // HEAD_DIM, HEAD_DIM, HEAD_DIM), F32),
                        pltpu.VMEM((1, RWKV_IN), F32),
                        pltpu.VMEM((c, gw), F32)],
        compiler_params=pltpu.CompilerParams(
            dimension_semantics=("parallel", "arbitrary"), vmem_limit_bytes=VMEM_LIMIT),
        name="rwkv7",
    )(p3, *params)


def _attn_kernel(pos_ref, q0_ref, q1_ref, kv_ref, qn_ref, kn_ref, invf_ref, sink_ref, bd_ref,
                 o_ref, kprev_ref, vprev_ref):
    blk = ATTN_BLOCK
    d = HEAD_DIM
    half = ROPE_DIM // 2
    group = ATTN_HEADS // ATTN_KV_HEADS
    kvw = ATTN_KV_HEADS * d
    first = pl.program_id(1) == 0

    @pl.when(first)
    def _():
        kprev_ref[...] = jnp.zeros_like(kprev_ref)
        vprev_ref[...] = jnp.zeros_like(vprev_ref)

    ones_bd = bd_ref[...]
    ang = pos_ref[...].astype(F32) * invf_ref[...]
    cos = jnp.cos(ang)
    sin = jnp.sin(ang)

    def norm_rope(x, gain):
        w = x.shape[1]
        x = x * lax.rsqrt(_head_sum(x * x, ones_bd[:w, :w]) * (1.0 / d) + NORM_EPS) * gain
        ld = lax.broadcasted_iota(jnp.int32, (blk, w), 1) % d
        rot = jnp.where(ld < half, -pltpu.roll(x, w - half, axis=1),
                        jnp.where(ld < ROPE_DIM, pltpu.roll(x, half, axis=1), 0.0))
        return x * cos[:, :w] + rot * sin[:, :w]

    qs = [norm_rope(q_ref[...], qn_ref[...]) * (d ** -0.5) for q_ref in (q0_ref, q1_ref)]
    kv = kv_ref[...]
    k_cur = norm_rope(kv[:, :kvw], kn_ref[:, :kvw])
    v_cur = kv[:, kvw:]
    k_prev = kprev_ref[...]
    v_prev = vprev_ref[...]

    ri = lax.broadcasted_iota(jnp.int32, (blk, blk), 0)
    ci = lax.broadcasted_iota(jnp.int32, (blk, blk), 1)
    neg_inf = -jnp.inf
    mask_prev = ci > ri
    mask_cur = ci <= ri
    no_prev = jnp.where(first, neg_inf, 0.0)

    heads_per_ref = ATTN_COL_BLOCK // d
    for qh in range(ATTN_HEADS):
        q = qs[qh // heads_per_ref][:, (qh % heads_per_ref) * d:(qh % heads_per_ref + 1) * d]
        ksl = slice((qh // group) * d, (qh // group + 1) * d)
        s_prev = jnp.where(mask_prev, _mm(q, k_prev[:, ksl], _NT), neg_inf) + no_prev
        s_cur = jnp.where(mask_cur, _mm(q, k_cur[:, ksl], _NT), neg_inf)
        sink = sink_ref[qh]
        m = jnp.maximum(jnp.maximum(jnp.max(s_prev, axis=1, keepdims=True),
                                    jnp.max(s_cur, axis=1, keepdims=True)), sink)
        p_prev = jnp.exp(s_prev - m)
        p_cur = jnp.exp(s_cur - m)
        den = (jnp.sum(p_prev, axis=1, keepdims=True) + jnp.sum(p_cur, axis=1, keepdims=True)
               + jnp.exp(sink - m))
        o = _mm(p_prev, v_prev[:, ksl]) + _mm(p_cur, v_cur[:, ksl])
        o_ref[:, qh * d:(qh + 1) * d] = o / den

    kprev_ref[...] = k_cur
    vprev_ref[...] = v_cur


def _attn(p3, positions, q_norm, k_norm, sinks):
    bsz, seq, _ = p3.shape
    blk = ATTN_BLOCK
    cw = ATTN_COL_BLOCK
    d = HEAD_DIM
    inv_freq = ROPE_THETA ** (-jnp.arange(0, ROPE_DIM, 2, dtype=F32) / ROPE_DIM)
    lane_d = np.arange(cw) % d
    invf = jnp.where(lane_d < ROPE_DIM, inv_freq[lane_d % (ROPE_DIM // 2)], 0.0).reshape(1, cw)
    qn = jnp.tile(q_norm.astype(F32), cw // d).reshape(1, cw)
    kn = jnp.tile(k_norm.astype(F32), cw // d).reshape(1, cw)
    ones_bd = jnp.asarray(np.kron(np.eye(cw // d, dtype=np.float32), np.ones((d, d), np.float32)), BF16)
    pos3 = positions.reshape(bsz, seq, 1)
    full = lambda t: pl.BlockSpec(t.shape, lambda i, j: (0,) * t.ndim)
    col = lambda o: pl.BlockSpec((None, blk, cw), lambda i, j: (i, j, ATTN_OFF_BLOCKS + o))
    return pl.pallas_call(
        _attn_kernel,
        grid=(bsz, seq // blk),
        in_specs=[pl.BlockSpec((None, blk, 1), lambda i, j: (i, j, 0)), col(0), col(1), col(2),
                  full(qn), full(kn), full(invf),
                  pl.BlockSpec(memory_space=pltpu.SMEM), full(ones_bd)],
        out_specs=pl.BlockSpec((None, blk, GROUP_WIDTH), lambda i, j: (i, j, 0)),
        out_shape=jax.ShapeDtypeStruct((bsz, seq, GROUP_WIDTH), F32),
        scratch_shapes=[pltpu.VMEM((blk, ATTN_KV_HEADS * d), F32),
                        pltpu.VMEM((blk, ATTN_KV_HEADS * d), F32)],
        compiler_params=pltpu.CompilerParams(
            dimension_semantics=("parallel", "arbitrary"), vmem_limit_bytes=VMEM_LIMIT),
        name="swa_attn",
    )(pos3, p3, p3, p3, qn, kn, invf, sinks.astype(F32), ones_bd)


def _s5_prep_kernel(lre_ref, lim_ref, ls_ref, bre_ref, bim_ref, are_ref, aim_ref, obre_ref, obim_ref):
    lre = lre_ref[...]
    lim = lim_ref[...]
    dt = jnp.exp(ls_ref[...])
    mag = jnp.exp(lre * dt)
    are = mag * jnp.cos(lim * dt)
    aim = mag * jnp.sin(lim * dt)
    are_ref[...] = are
    aim_ref[...] = aim
    inv = 1.0 / (lre * lre + lim * lim)
    cre = ((are - 1.0) * lre + aim * lim) * inv
    cim = (aim * lre - (are - 1.0) * lim) * inv
    bre = bre_ref[...]
    bim = bim_ref[...]
    obre_ref[...] = cre[:, None, :] * bre - cim[:, None, :] * bim
    obim_ref[...] = cre[:, None, :] * bim + cim[:, None, :] * bre


def _s5_kernel(u_ref, are_ref, aim_ref, bre_ref, bim_ref, cre_ref, cim_ref, d_ref, gw_ref, gb_ref,
               o_ref, xr_ref, xi_ref, sr_ref, si_ref):
    tc, bsz, ch = u_ref.shape

    @pl.when(pl.program_id(0) == 0)
    def _():
        sr_ref[...] = jnp.zeros_like(sr_ref)
        si_ref[...] = jnp.zeros_like(si_ref)

    u = u_ref[...].reshape(tc * bsz, ch)
    xr_ref[...] = _mm_hi(u, bre_ref[...])
    xi_ref[...] = _mm_hi(u, bim_ref[...])

    for s0 in range(0, S5_WIDTH, S5_STRIP):
        lanes = slice(s0, s0 + S5_STRIP)
        ar = jnp.broadcast_to(are_ref[:, lanes], (bsz, S5_STRIP))
        ai = jnp.broadcast_to(aim_ref[:, lanes], (bsz, S5_STRIP))

        def step(t, carry):
            sr, si = carry
            rows = pl.ds(pl.multiple_of(t * bsz, bsz), bsz)
            nr = ar * sr - ai * si + xr_ref[rows, lanes]
            ni = ar * si + ai * sr + xi_ref[rows, lanes]
            xr_ref[rows, lanes] = nr
            xi_ref[rows, lanes] = ni
            return nr, ni

        sr, si = lax.fori_loop(0, tc, step, (sr_ref[:, lanes], si_ref[:, lanes]))
        sr_ref[:, lanes] = sr
        si_ref[:, lanes] = si

    y = _mm_hi(xr_ref[...], cre_ref[...]) - _mm_hi(xi_ref[...], cim_ref[...])
    y = y + d_ref[...] * u
    z = 0.5 * y * (1.0 + lax.erf(y * (2.0 ** -0.5)))
    out = z * jax.nn.sigmoid(_mm(z, gw_ref[...]) + gb_ref[...])
    o_ref[...] = out.reshape(tc, bsz, ch)


def _s5(p3, lam_re, lam_im, log_step, b_re, b_im, c_re, c_im, d_skip, glu_w, glu_b):
    bsz, seq, _ = p3.shape
    g, st, ch = S5_GROUPS, S5_STATE, S5_GROUP
    gwd = GROUP_WIDTH
    vm = pl.BlockSpec(memory_space=pltpu.VMEM)
    a_re, a_im, bb_re, bb_im = pl.pallas_call(
        _s5_prep_kernel,
        in_specs=[vm] * 5,
        out_specs=[vm] * 4,
        out_shape=[jax.ShapeDtypeStruct((g, st), F32)] * 2 + [jax.ShapeDtypeStruct((g, ch, st), F32)] * 2,
        name="s5_prep",
    )(lam_re, lam_im, log_step.reshape(g, 1), jnp.swapaxes(b_re, 1, 2), jnp.swapaxes(b_im, 1, 2))

    eye = jnp.eye(g, dtype=F32)
    blockdiag_in = lambda t: (t[:, :, None, :] * eye[:, None, :, None]).reshape(gwd, S5_WIDTH)
    blockdiag_out = lambda t: (jnp.swapaxes(t, 1, 2)[:, :, None, :] * eye[:, None, :, None]).reshape(S5_WIDTH, gwd)
    bmat_re, bmat_im = blockdiag_in(bb_re), blockdiag_in(bb_im)
    cmat_re, cmat_im = blockdiag_out(c_re.astype(F32)), blockdiag_out(c_im.astype(F32))

    tc = min(32, seq)
    u_t = jnp.swapaxes(p3[:, :, S5_OFF_BLOCKS * gwd:(S5_OFF_BLOCKS + 1) * gwd], 0, 1)
    full = lambda t: pl.BlockSpec(t.shape, lambda i: (0,) * t.ndim)
    params = [a_re.reshape(1, S5_WIDTH), a_im.reshape(1, S5_WIDTH), bmat_re, bmat_im, cmat_re, cmat_im,
              d_skip.reshape(1, gwd), glu_w, glu_b.reshape(1, gwd)]
    y_t = pl.pallas_call(
        _s5_kernel,
        grid=(seq // tc,),
        in_specs=[pl.BlockSpec((tc, bsz, gwd), lambda i: (i, 0, 0))] + [full(t) for t in params],
        out_specs=pl.BlockSpec((tc, bsz, gwd), lambda i: (i, 0, 0)),
        out_shape=jax.ShapeDtypeStruct((seq, bsz, gwd), F32),
        scratch_shapes=[pltpu.VMEM((tc * bsz, S5_WIDTH), F32), pltpu.VMEM((tc * bsz, S5_WIDTH), F32),
                        pltpu.VMEM((bsz, S5_WIDTH), F32), pltpu.VMEM((bsz, S5_WIDTH), F32)],
        compiler_params=pltpu.CompilerParams(
            dimension_semantics=("arbitrary",), vmem_limit_bytes=VMEM_LIMIT),
        name="s5_scan",
    )(u_t, *params)
    return jnp.swapaxes(y_t, 0, 1)


def _hgrn_kernel(qf_ref, ig_ref, lbraw_ref, gn_ref, tri_ref, trim_ref, tril_ref, bd_ref, o_ref,
                 s_ref, osc_ref, *, layer):
    tb = HGRN_TILE
    cz = HGRN_CHUNK
    gw = GROUP_WIDTH
    n = HEAD_DIM

    @pl.when(pl.program_id(1) == 0)
    def _():
        s_ref[...] = jnp.zeros_like(s_ref)

    qf = qf_ref[...]
    ig = ig_ref[...]
    q, f = qf[:, :gw], qf[:, gw:]
    v, g = ig[:, :gw], ig[:, gw:]

    lbr = lbraw_ref[...]
    e = jnp.exp(lbr - jnp.max(lbr, axis=0, keepdims=True))
    sm = e / jnp.sum(e, axis=0, keepdims=True)
    lb = jnp.zeros((1, gw), F32)
    for i in range(1, layer + 1):
        lb = lb + sm[i:i + 1, :]

    q = _silu(q) * (n ** -0.5)
    f_gate = lb + (1.0 - lb) * jax.nn.sigmoid(f)
    log_f = jnp.log(f_gate)
    k = 1.0 - f_gate

    bcum = _const_lhs_mm(tri_ref[...], log_f)
    bmid = _const_lhs_mm(trim_ref[...], log_f)
    blast = _const_lhs_mm(tril_ref[...], log_f)
    qe = q * jnp.exp(bcum - bmid)
    ke = k * jnp.exp(bmid - bcum)
    kl = k * jnp.exp(blast - bcum)
    qb = q * jnp.exp(bcum)
    dec = jnp.exp(blast)

    ri = lax.broadcasted_iota(jnp.int32, (tb, tb), 0)
    ci = lax.broadcasted_iota(jnp.int32, (tb, tb), 1)
    mask = jnp.logical_and(ri // cz == ci // cz, ri >= ci)

    for h in range(gw // n):
        sl = slice(h * n, (h + 1) * n)
        v_h = v[:, sl]
        att = jnp.where(mask, _mm(qe[:, sl], ke[:, sl], _NT), 0.0)
        osc_ref[:, sl] = _mm(att, v_h)
        st = s_ref[h]
        for j in range(tb // cz):
            rows = slice(j * cz, (j + 1) * cz)
            osc_ref[rows, sl] += _mm(qb[rows, sl], st, _NT)
            st = st * dec[j * cz:j * cz + 1, sl] + _mm(v_h[rows, :], kl[rows, sl], _TN)
        s_ref[h] = st

    o = osc_ref[...]
    ms = _head_sum(o * o, bd_ref[...]) * (1.0 / n)
    o_ref[...] = o * lax.rsqrt(ms + NORM_EPS) * gn_ref[...] * _silu(g)


def _hgrn(p3, lower_bounds, g_norm, layer):
    bsz, seq, _ = p3.shape
    tb = min(HGRN_TILE, seq)
    cz = HGRN_CHUNK
    gw = GROUP_WIDTH
    idx = np.arange(tb)
    same = (idx[:, None] // cz) == (idx[None, :] // cz)
    tri = same & (idx[:, None] >= idx[None, :])
    trim = same & ((idx[:, None] // cz) * cz + cz // 2 - 1 >= idx[None, :])
    tril = same
    consts = [jnp.asarray(t.astype(np.float32), BF16) for t in (tri, trim, tril)]
    ones_bd = jnp.asarray(np.kron(np.eye(4, dtype=np.float32), np.ones((HEAD_DIM, HEAD_DIM), np.float32)), BF16)
    gn = jnp.tile(g_norm.astype(F32), gw // HEAD_DIM).reshape(1, gw)
    params = [lower_bounds.astype(F32), gn] + consts + [ones_bd]
    full = lambda t: pl.BlockSpec(t.shape, lambda i, j: (0,) * t.ndim)
    col = lambda o: pl.BlockSpec((None, tb, 2 * gw), lambda i, j: (i, j, HGRN_OFF_BLOCKS + o))
    return pl.pallas_call(
        functools.partial(_hgrn_kernel, layer=layer),
        grid=(bsz, seq // tb),
        in_specs=[col(0), col(1)] + [full(t) for t in params],
        out_specs=pl.BlockSpec((None, tb, gw), lambda i, j: (i, j, 0)),
        out_shape=jax.ShapeDtypeStruct((bsz, seq, gw), F32),
        scratch_shapes=[pltpu.VMEM((gw // HEAD_DIM, HEAD_DIM, HEAD_DIM), F32),
                        pltpu.VMEM((tb, gw), F32)],
        compiler_params=pltpu.CompilerParams(
            dimension_semantics=("parallel", "arbitrary"), vmem_limit_bytes=VMEM_LIMIT),
        name="hgrn2",
    )(p3, p3, *params)


def kernel(x, positions, ffn1_norm, ffn1_w_gate, ffn1_w_up, ffn1_w_down, mix_norm, w_in, rwkv_mu, rwkv_w0, rwkv_w_up, rwkv_a0, rwkv_a_up, rwkv_g_up, rwkv_k_k, rwkv_k_a, rwkv_r_k, rwkv_ln_w, rwkv_ln_b, attn_q_norm, attn_k_norm, attn_sinks, s5_lambda_re, s5_lambda_im, s5_log_step, s5_b_re, s5_b_im, s5_c_re, s5_c_im, s5_d, s5_glu_w, s5_glu_b, hgrn_lower_bounds, hgrn_g_norm, w_out, ffn2_norm, ffn2_w_gate, ffn2_w_up, ffn2_w_down):
    bsz, seq, d = x.shape
    depth = w_in.shape[0]
    n = bsz * seq
    xf = x.reshape(n, d)
    for l in range(depth):
        xf = _ffn(xf, ffn1_norm[l], ffn1_w_gate[l].astype(BF16), ffn1_w_up[l].astype(BF16),
                  ffn1_w_down[l].astype(BF16))
        p3 = _proj(xf, mix_norm[l], w_in[l].astype(BF16)).reshape(bsz, seq, D_IN)
        y_a = _rwkv(p3, rwkv_mu[l], rwkv_w0[l], rwkv_w_up[l], rwkv_a0[l], rwkv_a_up[l], rwkv_g_up[l],
                    rwkv_k_k[l], rwkv_k_a[l], rwkv_r_k[l].reshape(-1), rwkv_ln_w[l], rwkv_ln_b[l])
        y_b = _attn(p3, positions, attn_q_norm[l], attn_k_norm[l], attn_sinks[l])
        y_c = _s5(p3, s5_lambda_re[l], s5_lambda_im[l], s5_log_step[l], s5_b_re[l], s5_b_im[l],
                  s5_c_re[l], s5_c_im[l], s5_d[l], s5_glu_w[l], s5_glu_b[l])
        y_d = _hgrn(p3, hgrn_lower_bounds, hgrn_g_norm[l], l)
        ys = [t.reshape(n, GROUP_WIDTH) for t in (y_a, y_b, y_c, y_d)]
        xf = _outproj(xf, ys, w_out[l].astype(BF16))
        xf = _ffn(xf, ffn2_norm[l], ffn2_w_gate[l].astype(BF16), ffn2_w_up[l].astype(BF16),
                  ffn2_w_down[l].astype(BF16))
    return xf.reshape(bsz, seq, d)
```

```python
import functools

import jax
import jax.numpy as jnp
import numpy as np
from jax import lax
from jax.experimental import pallas as pl
from jax.experimental.pallas import tpu as pltpu

F32 = jnp.float32
BF16 = jnp.bfloat16

HEAD_DIM = 64
GROUP_WIDTH = 512
NORM_EPS = 1e-6
FFN_RES_WEIGHT = 0.5

RWKV_W_RANK = 64
RWKV_A_RANK = 64
RWKV_G_RANK = 128
RWKV_LN_EPS = 64e-5
RWKV_IN = 3 * GROUP_WIDTH + RWKV_W_RANK + RWKV_A_RANK + RWKV_G_RANK
RWKV_CHUNK = 64

ATTN_HEADS = 8
ATTN_KV_HEADS = 2
ATTN_BLOCK = 128
ROPE_THETA = 500000.0
ROPE_DIM = HEAD_DIM // 4
ATTN_COL_BLOCK = 256
ATTN_OFF_BLOCKS = RWKV_IN // ATTN_COL_BLOCK

S5_GROUP = 16
S5_GROUPS = GROUP_WIDTH // S5_GROUP
S5_STATE = 64
S5_WIDTH = S5_GROUPS * S5_STATE
S5_OFF_BLOCKS = (RWKV_IN + 768) // GROUP_WIDTH
S5_STRIP = 512

HGRN_CHUNK = 16
HGRN_TILE = 128
HGRN_OFF_BLOCKS = 3

D_IN = 5120
VMEM_LIMIT = 56 * 1024 * 1024


def _dot(a, b, dims):
    return lax.dot_general(a, b, (dims, ((), ())), preferred_element_type=F32)


_NN = ((1,), (0,))


def _mm(a, b):
    return _dot(a.astype(BF16), b.astype(BF16), _NN)


_BNN = (((2,), (1,)), ((0,), (0,)))
_BNT = (((2,), (2,)), ((0,), (0,)))
_BTN = (((1,), (1,)), ((0,), (0,)))


def _bmm(a, b, dims=_BNN):
    return lax.dot_general(a.astype(BF16), b.astype(BF16), dims, preferred_element_type=F32)


def _split3(x):
    hi = x.astype(BF16)
    r = x - hi.astype(F32)
    mid = r.astype(BF16)
    lo = (r - mid.astype(F32)).astype(BF16)
    return hi, mid, lo


def _const_lhs_mm(c, x):
    hi, mid, lo = _split3(x)
    return _dot(c, hi, _NN) + (_dot(c, mid, _NN) + _dot(c, lo, _NN))


def _const_rhs_mm(x, c):
    hi, mid, lo = _split3(x)
    return _dot(hi, c, _NN) + (_dot(mid, c, _NN) + _dot(lo, c, _NN))


def _head_sum(x, ones_bd):
    w = ones_bd.shape[0]
    parts = [_const_rhs_mm(x[:, i:i + w], ones_bd) for i in range(0, x.shape[1], w)]
    return parts[0] if len(parts) == 1 else jnp.concatenate(parts, axis=1)


def _silu(x):
    return x * jax.nn.sigmoid(x)


def _rms_rows(x, gain):
    ms = jnp.mean(x * x, axis=-1, keepdims=True)
    return x * lax.rsqrt(ms + NORM_EPS) * gain


def _ffn_kernel(x_ref, g_ref, wg_ref, wu_ref, wd_ref, o_ref, h_ref):
    @pl.when(pl.program_id(1) == 0)
    def _():
        x = x_ref[...]
        h_ref[...] = _rms_rows(x, g_ref[...]).astype(BF16)
        o_ref[...] = x

    h = h_ref[...]
    gate = jnp.dot(h, wg_ref[...], preferred_element_type=F32)
    up = jnp.dot(h, wu_ref[...], preferred_element_type=F32)
    act = (_silu(gate) * up).astype(BF16)
    o_ref[...] += FFN_RES_WEIGHT * jnp.dot(act, wd_ref[...], preferred_element_type=F32)


def _ffn(x2d, gain, wg, wu, wd):
    n, d = x2d.shape
    f = wg.shape[1]
    tm = min(512, n)
    tf = 512 if f % 512 == 0 else f
    return pl.pallas_call(
        _ffn_kernel,
        grid=(n // tm, f // tf),
        in_specs=[
            pl.BlockSpec((tm, d), lambda i, j: (i, 0)),
            pl.BlockSpec((1, d), lambda i, j: (0, 0)),
            pl.BlockSpec((d, tf), lambda i, j: (0, j)),
            pl.BlockSpec((d, tf), lambda i, j: (0, j)),
            pl.BlockSpec((tf, d), lambda i, j: (j, 0)),
        ],
        out_specs=pl.BlockSpec((tm, d), lambda i, j: (i, 0)),
        out_shape=jax.ShapeDtypeStruct((n, d), F32),
        scratch_shapes=[pltpu.VMEM((tm, d), BF16)],
        compiler_params=pltpu.CompilerParams(
            dimension_semantics=("parallel", "arbitrary"), vmem_limit_bytes=VMEM_LIMIT),
        name="ffn",
    )(x2d, gain.reshape(1, d), wg, wu, wd)


def _proj_kernel(x_ref, g_ref, w_ref, o_ref, h_ref):
    @pl.when(pl.program_id(1) == 0)
    def _():
        h_ref[...] = _rms_rows(x_ref[...], g_ref[...]).astype(BF16)

    o_ref[...] = jnp.dot(h_ref[...], w_ref[...], preferred_element_type=F32)


def _proj(x2d, gain, w):
    n, d = x2d.shape
    dout = w.shape[1]
    tm = min(512, n)
    tn = dout // 2
    return pl.pallas_call(
        _proj_kernel,
        grid=(n // tm, dout // tn),
        in_specs=[
            pl.BlockSpec((tm, d), lambda i, j: (i, 0)),
            pl.BlockSpec((1, d), lambda i, j: (0, 0)),
            pl.BlockSpec((d, tn), lambda i, j: (0, j)),
        ],
        out_specs=pl.BlockSpec((tm, tn), lambda i, j: (i, j)),
        out_shape=jax.ShapeDtypeStruct((n, dout), F32),
        scratch_shapes=[pltpu.VMEM((tm, d), BF16)],
        compiler_params=pltpu.CompilerParams(
            dimension_semantics=("parallel", "arbitrary"), vmem_limit_bytes=VMEM_LIMIT),
        name="in_proj",
    )(x2d, gain.reshape(1, d), w)


def _outproj_kernel(x_ref, ya_ref, yb_ref, yc_ref, yd_ref, w_ref, o_ref):
    gw = GROUP_WIDTH
    acc = x_ref[...]
    for m, y_ref in enumerate((ya_ref, yb_ref, yc_ref, yd_ref)):
        acc = acc + jnp.dot(y_ref[...].astype(BF16), w_ref[m * gw:(m + 1) * gw, :],
                            preferred_element_type=F32)
    o_ref[...] = acc


def _outproj(x2d, ys, w):
    n, d = x2d.shape
    tm = min(256, n)
    yspec = pl.BlockSpec((tm, GROUP_WIDTH), lambda i: (i, 0))
    return pl.pallas_call(
        _outproj_kernel,
        grid=(n // tm,),
        in_specs=[pl.BlockSpec((tm, d), lambda i: (i, 0)), yspec, yspec, yspec, yspec,
                  pl.BlockSpec(w.shape, lambda i: (0, 0))],
        out_specs=pl.BlockSpec((tm, d), lambda i: (i, 0)),
        out_shape=jax.ShapeDtypeStruct((n, d), F32),
        compiler_params=pltpu.CompilerParams(
            dimension_semantics=("parallel",), vmem_limit_bytes=VMEM_LIMIT),
        name="out_proj",
    )(x2d, *ys, w)


def _rwkv_kernel(p_ref, mu_ref, w0_ref, wup_ref, a0_ref, aup_ref, gup_ref, kk_ref, ka_ref, rk_ref,
                 lnw_ref, lnb_ref, tri_ref, bd_ref, o_ref, s_ref, prev_ref, osc_ref):
    c = RWKV_CHUNK
    gw = GROUP_WIDTH
    n = HEAD_DIM

    @pl.when(pl.program_id(1) == 0)
    def _():
        s_ref[...] = jnp.zeros_like(s_ref)
        prev_ref[...] = jnp.zeros_like(prev_ref)

    p = p_ref[...]
    row = lax.broadcasted_iota(jnp.int32, p.shape, 0)
    shifted = jnp.where(row == 0, prev_ref[...], pltpu.roll(p, 1, axis=0))
    prev_ref[...] = p[c - 1:c, :]
    p = p + (shifted - p) * mu_ref[...]

    r = p[:, 0:gw]
    k = p[:, gw:2 * gw]
    v = p[:, 2 * gw:3 * gw]
    o1 = 3 * gw
    w_lo = p[:, o1:o1 + RWKV_W_RANK]
    a_lo = p[:, o1 + RWKV_W_RANK:o1 + RWKV_W_RANK + RWKV_A_RANK]
    g_lo = p[:, o1 + RWKV_W_RANK + RWKV_A_RANK:]

    z = -(w0_ref[...] + _mm(jnp.tanh(w_lo), wup_ref[...]))
    softplus = jnp.maximum(z, 0.0) + jnp.log1p(jnp.exp(-jnp.abs(z)))
    lw = -jnp.exp(-softplus - 0.5)
    a = jax.nn.sigmoid(a0_ref[...] + _mm(a_lo, aup_ref[...]))
    g = _mm(jax.nn.sigmoid(g_lo), gup_ref[...])

    ones_bd = bd_ref[...]
    kk = k * kk_ref[...]
    kk = kk * lax.rsqrt(jnp.maximum(_head_sum(kk * kk, ones_bd), 1e-24))
    k = k * (1.0 + (a - 1.0) * ka_ref[...])
    b = kk * a

    cum = _const_lhs_mm(tri_ref[...], lw)
    cum_last = cum[c - 1:c, :]
    e_pos = jnp.exp(cum)
    e_neg = jnp.exp(-cum)
    e_end = jnp.exp(cum_last - cum)
    g_end = jnp.exp(cum_last)
    kq = kk * jnp.exp(cum - lw)
    rq = r * e_pos
    bd = b * e_neg
    kd = k * e_neg
    b_end = b * e_end
    k_end = k * e_end

    ri = lax.broadcasted_iota(jnp.int32, (c, c), 0)
    ci = lax.broadcasted_iota(jnp.int32, (c, c), 1)
    strict = ri > ci
    incl = ri >= ci
    eye = (ri == ci).astype(F32)

    heads = lambda t: jnp.stack([t[:, h * n:(h + 1) * n] for h in range(gw // n)], axis=0)
    kq_h, rq_h, bd_h, kd_h, v_h = heads(kq), heads(rq), heads(bd), heads(kd), heads(v)
    s0 = s_ref[...]
    a_bb = jnp.where(strict, _bmm(kq_h, bd_h, _BNT), 0.0)
    a_bk = jnp.where(strict, _bmm(kq_h, kd_h, _BNT), 0.0)
    a_rb = jnp.where(incl, _bmm(rq_h, bd_h, _BNT), 0.0)
    a_rk = jnp.where(incl, _bmm(rq_h, kd_h, _BNT), 0.0)
    pw = -a_bb
    t_inv = eye + pw
    for _ in range(int(np.log2(c)) - 1):
        pw = _bmm(pw, pw)
        t_inv = t_inv + _bmm(t_inv, pw)
    x = _bmm(kq_h, s0, _BNT) + _bmm(a_bk, v_h)
    u = -_bmm(t_inv, x)
    y = _bmm(rq_h, s0, _BNT) + _bmm(a_rb, u) + _bmm(a_rk, v_h)
    s_ref[...] = s0 * heads(g_end) + _bmm(u, heads(b_end), _BTN) + _bmm(v_h, heads(k_end), _BTN)
    for h in range(gw // n):
        osc_ref[:, h * n:(h + 1) * n] = y[h]

    o = osc_ref[...]
    inv_n = 1.0 / n
    mean = _head_sum(o, ones_bd) * inv_n
    dlt = o - mean
    var = _head_sum(dlt * dlt, ones_bd) * inv_n
    o = dlt * lax.rsqrt(var + RWKV_LN_EPS) * lnw_ref[...] + lnb_ref[...]
    bonus = _head_sum(r * k * rk_ref[...], ones_bd) * v
    o_ref[...] = (o + bonus) * g


def _rwkv(p3, mu, w0, w_up, a0, a_up, g_up, k_k, k_a, r_k, ln_w, ln_b):
    bsz, seq, _ = p3.shape
    c = RWKV_CHUNK
    gw = GROUP_WIDTH
    tri = jnp.asarray(np.tril(np.ones((c, c), np.float32)), BF16)
    ones_bd = jnp.asarray(np.kron(np.eye(4, dtype=np.float32), np.ones((HEAD_DIM, HEAD_DIM), np.float32)), BF16)
    row = lambda t: t.reshape(1, -1)
    full = lambda t: pl.BlockSpec(t.shape, lambda i, j: (0,) * t.ndim)
    params = [row(mu), row(w0), w_up, row(a0), a_up, g_up, row(k_k), row(k_a), row(r_k), row(ln_w),
              row(ln_b), tri, ones_bd]
    return pl.pallas_call(
        _rwkv_kernel,
        grid=(bsz, seq // c),
        in_specs=[pl.BlockSpec((None, c, RWKV_IN), lambda i, j: (i, j, 0))] + [full(t) for t in params],
        out_specs=pl.BlockSpec((None, c, gw), lambda i, j: (i, j, 0)),
        out_shape=jax.ShapeDtypeStruct((bsz, seq, gw), F32),
        scratch_shapes=[pltpu.VMEM((gw // HEAD_DIM, HEAD_DIM, HEAD_DIM), F32),
                        pltpu.VMEM((1, RWKV_IN), F32),
                        pltpu.VMEM((c, gw), F32)],
        compiler_params=pltpu.CompilerParams(
            dimension_semantics=("parallel", "arbitrary"), vmem_limit_bytes=VMEM_LIMIT),
        name="rwkv7",
    )(p3, *params)


def _attn_kernel(pos_ref, q0_ref, q1_ref, kv_ref, qn_ref, kn_ref, invf_ref, sink_ref, bd_ref,
                 o_ref, kprev_ref, vprev_ref):
    blk = ATTN_BLOCK
    d = HEAD_DIM
    half = ROPE_DIM // 2
    group = ATTN_HEADS // ATTN_KV_HEADS
    kvw = ATTN_KV_HEADS * d
    first = pl.program_id(1) == 0

    @pl.when(first)
    def _():
        kprev_ref[...] = jnp.zeros_like(kprev_ref)
        vprev_ref[...] = jnp.zeros_like(vprev_ref)

    ones_bd = bd_ref[...]
    ang = pos_ref[...].astype(F32) * invf_ref[...]
    cos = jnp.cos(ang)
    sin = jnp.sin(ang)

    def norm_rope(x, gain):
        w = x.shape[1]
        x = x * lax.rsqrt(_head_sum(x * x, ones_bd[:w, :w]) * (1.0 / d) + NORM_EPS) * gain
        ld = lax.broadcasted_iota(jnp.int32, (blk, w), 1) % d
        rot = jnp.where(ld < half, -pltpu.roll(x, w - half, axis=1),
                        jnp.where(ld < ROPE_DIM, pltpu.roll(x, half, axis=1), 0.0))
        return x * cos[:, :w] + rot * sin[:, :w]

    qs = [norm_rope(q_ref[...], qn_ref[...]) * (d ** -0.5) for q_ref in (q0_ref, q1_ref)]
    kv = kv_ref[...]
    k_cur = norm_rope(kv[:, :kvw], kn_ref[:, :kvw])
    v_cur = kv[:, kvw:]
    k_prev = kprev_ref[...]
    v_prev = vprev_ref[...]

    heads_per_ref = ATTN_COL_BLOCK // d
    q_head = lambda qh: qs[qh // heads_per_ref][:, (qh % heads_per_ref) * d:(qh % heads_per_ref + 1) * d]
    q_g = jnp.stack([jnp.concatenate([q_head(kh * group + i) for i in range(group)], axis=0)
                     for kh in range(ATTN_KV_HEADS)], axis=0)
    kv_heads = lambda t: jnp.stack([t[:, kh * d:(kh + 1) * d] for kh in range(ATTN_KV_HEADS)], axis=0)

    ri = lax.broadcasted_iota(jnp.int32, (group * blk, blk), 0) % blk
    ci = lax.broadcasted_iota(jnp.int32, (group * blk, blk), 1)
    neg_inf = -jnp.inf
    no_prev = jnp.where(first, neg_inf, 0.0)
    s_prev = jnp.where(ci > ri, _bmm(q_g, kv_heads(k_prev), _BNT), neg_inf) + no_prev
    s_cur = jnp.where(ci <= ri, _bmm(q_g, kv_heads(k_cur), _BNT), neg_inf)
    sink = sink_ref[...]
    m = jnp.maximum(jnp.maximum(jnp.max(s_prev, axis=2, keepdims=True),
                                jnp.max(s_cur, axis=2, keepdims=True)), sink)
    p_prev = jnp.exp(s_prev - m)
    p_cur = jnp.exp(s_cur - m)
    den = (jnp.sum(p_prev, axis=2, keepdims=True) + jnp.sum(p_cur, axis=2, keepdims=True)
           + jnp.exp(sink - m))
    o = (_bmm(p_prev, kv_heads(v_prev)) + _bmm(p_cur, kv_heads(v_cur))) / den
    for qh in range(ATTN_HEADS):
        o_ref[:, qh * d:(qh + 1) * d] = o[qh // group, (qh % group) * blk:(qh % group + 1) * blk, :]

    kprev_ref[...] = k_cur
    vprev_ref[...] = v_cur


def _attn(p3, positions, q_norm, k_norm, sinks):
    bsz, seq, _ = p3.shape
    blk = ATTN_BLOCK
    cw = ATTN_COL_BLOCK
    d = HEAD_DIM
    inv_freq = ROPE_THETA ** (-jnp.arange(0, ROPE_DIM, 2, dtype=F32) / ROPE_DIM)
    lane_d = np.arange(cw) % d
    invf = jnp.where(lane_d < ROPE_DIM, inv_freq[lane_d % (ROPE_DIM // 2)], 0.0).reshape(1, cw)
    qn = jnp.tile(q_norm.astype(F32), cw // d).reshape(1, cw)
    kn = jnp.tile(k_norm.astype(F32), cw // d).reshape(1, cw)
    ones_bd = jnp.asarray(np.kron(np.eye(cw // d, dtype=np.float32), np.ones((d, d), np.float32)), BF16)
    pos3 = positions.reshape(bsz, seq, 1)
    group = ATTN_HEADS // ATTN_KV_HEADS
    sink_col = jnp.repeat(sinks.astype(F32).reshape(ATTN_KV_HEADS, group), blk, axis=1)[:, :, None]
    full = lambda t: pl.BlockSpec(t.shape, lambda i, j: (0,) * t.ndim)
    col = lambda o: pl.BlockSpec((None, blk, cw), lambda i, j: (i, j, ATTN_OFF_BLOCKS + o))
    return pl.pallas_call(
        _attn_kernel,
        grid=(bsz, seq // blk),
        in_specs=[pl.BlockSpec((None, blk, 1), lambda i, j: (i, j, 0)), col(0), col(1), col(2),
                  full(qn), full(kn), full(invf), full(sink_col), full(ones_bd)],
        out_specs=pl.BlockSpec((None, blk, GROUP_WIDTH), lambda i, j: (i, j, 0)),
        out_shape=jax.ShapeDtypeStruct((bsz, seq, GROUP_WIDTH), F32),
        scratch_shapes=[pltpu.VMEM((blk, ATTN_KV_HEADS * d), F32),
                        pltpu.VMEM((blk, ATTN_KV_HEADS * d), F32)],
        compiler_params=pltpu.CompilerParams(
            dimension_semantics=("parallel", "arbitrary"), vmem_limit_bytes=VMEM_LIMIT),
        name="swa_attn",
    )(pos3, p3, p3, p3, qn, kn, invf, sink_col, ones_bd)


def _s5_prep_kernel(lre_ref, lim_ref, ls_ref, bre_ref, bim_ref, are_ref, aim_ref, obre_ref, obim_ref):
    lre = lre_ref[...]
    lim = lim_ref[...]
    dt = jnp.exp(ls_ref[...])
    mag = jnp.exp(lre * dt)
    are = mag * jnp.cos(lim * dt)
    aim = mag * jnp.sin(lim * dt)
    are_ref[...] = are
    aim_ref[...] = aim
    inv = 1.0 / (lre * lre + lim * lim)
    cre = ((are - 1.0) * lre + aim * lim) * inv
    cim = (aim * lre - (are - 1.0) * lim) * inv
    bre = bre_ref[...]
    bim = bim_ref[...]
    obre_ref[...] = cre[:, None, :] * bre - cim[:, None, :] * bim
    obim_ref[...] = cre[:, None, :] * bim + cim[:, None, :] * bre


def _s5_kernel(u_ref, are_ref, aim_ref, b_ref, c_ref, d_ref, gw_ref, gb_ref, o_ref, x_ref, s_ref):
    tc, bsz, ch = u_ref.shape
    hw = S5_WIDTH // 2
    hc = ch // 2

    @pl.when(pl.program_id(0) == 0)
    def _():
        s_ref[...] = jnp.zeros_like(s_ref)

    u = u_ref[...].reshape(tc * bsz, ch)
    for hf in range(2):
        x_ref[:, 2 * hw * hf:2 * hw * (hf + 1)] = _mm(u[:, hc * hf:hc * (hf + 1)], b_ref[hf])

    for hf in range(2):
        for s0 in range(0, hw, S5_STRIP):
            re = slice(2 * hw * hf + s0, 2 * hw * hf + s0 + S5_STRIP)
            im = slice(2 * hw * hf + hw + s0, 2 * hw * hf + hw + s0 + S5_STRIP)
            lam = slice(hw * hf + s0, hw * hf + s0 + S5_STRIP)
            ar = jnp.broadcast_to(are_ref[:, lam], (bsz, S5_STRIP))
            ai = jnp.broadcast_to(aim_ref[:, lam], (bsz, S5_STRIP))

            def step(t, carry, re=re, im=im, ar=ar, ai=ai):
                sr, si = carry
                rows = pl.ds(pl.multiple_of(t * bsz, bsz), bsz)
                nr = ar * sr - ai * si + x_ref[rows, re]
                ni = ar * si + ai * sr + x_ref[rows, im]
                x_ref[rows, re] = nr
                x_ref[rows, im] = ni
                return nr, ni

            sr, si = lax.fori_loop(0, tc, step, (s_ref[:, re], s_ref[:, im]), unroll=4)
            s_ref[:, re] = sr
            s_ref[:, im] = si

    y = jnp.concatenate([_mm(x_ref[:, 2 * hw * hf:2 * hw * (hf + 1)], c_ref[hf]) for hf in range(2)], axis=1)
    y = y + d_ref[...] * u
    z = 0.5 * y * (1.0 + lax.erf(y * (2.0 ** -0.5)))
    out = z * jax.nn.sigmoid(_mm(z, gw_ref[...]) + gb_ref[...])
    o_ref[...] = out.reshape(tc, bsz, ch)


def _s5(p3, lam_re, lam_im, log_step, b_re, b_im, c_re, c_im, d_skip, glu_w, glu_b):
    bsz, seq, _ = p3.shape
    g, st, ch = S5_GROUPS, S5_STATE, S5_GROUP
    gwd = GROUP_WIDTH
    vm = pl.BlockSpec(memory_space=pltpu.VMEM)
    a_re, a_im, bb_re, bb_im = pl.pallas_call(
        _s5_prep_kernel,
        in_specs=[vm] * 5,
        out_specs=[vm] * 4,
        out_shape=[jax.ShapeDtypeStruct((g, st), F32)] * 2 + [jax.ShapeDtypeStruct((g, ch, st), F32)] * 2,
        name="s5_prep",
    )(lam_re, lam_im, log_step.reshape(g, 1), jnp.swapaxes(b_re, 1, 2), jnp.swapaxes(b_im, 1, 2))

    gh = g // 2
    eye = jnp.eye(gh, dtype=F32)
    bd_in = lambda t: (t[:, :, None, :] * eye[:, None, :, None]).reshape(gh * ch, gh * st)
    bd_out = lambda t: (jnp.swapaxes(t, 1, 2)[:, :, None, :] * eye[:, None, :, None]).reshape(gh * st, gh * ch)
    halves = lambda t: (t[:gh], t[gh:])
    b_mat = jnp.stack([jnp.concatenate([bd_in(r), bd_in(i)], axis=1)
                       for r, i in zip(halves(bb_re), halves(bb_im))]).astype(BF16)
    c_mat = jnp.stack([jnp.concatenate([bd_out(r), -bd_out(i)], axis=0)
                       for r, i in zip(halves(c_re.astype(F32)), halves(c_im.astype(F32)))]).astype(BF16)

    tc = min(64, seq)
    u_t = jnp.swapaxes(p3[:, :, S5_OFF_BLOCKS * gwd:(S5_OFF_BLOCKS + 1) * gwd], 0, 1)
    full = lambda t: pl.BlockSpec(t.shape, lambda i: (0,) * t.ndim)
    params = [a_re.reshape(1, S5_WIDTH), a_im.reshape(1, S5_WIDTH), b_mat, c_mat,
              d_skip.reshape(1, gwd), glu_w.astype(BF16), glu_b.reshape(1, gwd)]
    y_t = pl.pallas_call(
        _s5_kernel,
        grid=(seq // tc,),
        in_specs=[pl.BlockSpec((tc, bsz, gwd), lambda i: (i, 0, 0))] + [full(t) for t in params],
        out_specs=pl.BlockSpec((tc, bsz, gwd), lambda i: (i, 0, 0)),
        out_shape=jax.ShapeDtypeStruct((seq, bsz, gwd), F32),
        scratch_shapes=[pltpu.VMEM((tc * bsz, 2 * S5_WIDTH), F32), pltpu.VMEM((bsz, 2 * S5_WIDTH), F32)],
        compiler_params=pltpu.CompilerParams(
            dimension_semantics=("arbitrary",), vmem_limit_bytes=VMEM_LIMIT),
        name="s5_scan",
    )(u_t, *params)
    return jnp.swapaxes(y_t, 0, 1)


def _hgrn_kernel(qf_ref, ig_ref, lbraw_ref, gn_ref, tri_ref, trim_ref, tril_ref, bd_ref, o_ref,
                 s_ref, osc_ref, *, layer):
    tb = HGRN_TILE
    cz = HGRN_CHUNK
    gw = GROUP_WIDTH
    n = HEAD_DIM

    @pl.when(pl.program_id(1) == 0)
    def _():
        s_ref[...] = jnp.zeros_like(s_ref)

    qf = qf_ref[...]
    ig = ig_ref[...]
    q, f = qf[:, :gw], qf[:, gw:]
    v, g = ig[:, :gw], ig[:, gw:]

    lbr = lbraw_ref[...]
    e = jnp.exp(lbr - jnp.max(lbr, axis=0, keepdims=True))
    sm = e / jnp.sum(e, axis=0, keepdims=True)
    lb = jnp.zeros((1, gw), F32)
    for i in range(1, layer + 1):
        lb = lb + sm[i:i + 1, :]

    q = _silu(q) * (n ** -0.5)
    f_gate = lb + (1.0 - lb) * jax.nn.sigmoid(f)
    log_f = jnp.log(f_gate)
    k = 1.0 - f_gate

    bcum = _const_lhs_mm(tri_ref[...], log_f)
    bmid = _const_lhs_mm(trim_ref[...], log_f)
    blast = _const_lhs_mm(tril_ref[...], log_f)
    qe = q * jnp.exp(bcum - bmid)
    ke = k * jnp.exp(bmid - bcum)
    kl = k * jnp.exp(blast - bcum)
    qb = q * jnp.exp(bcum)
    dec = jnp.exp(blast)

    ri = lax.broadcasted_iota(jnp.int32, (tb, tb), 0)
    ci = lax.broadcasted_iota(jnp.int32, (tb, tb), 1)
    mask = jnp.logical_and(ri // cz == ci // cz, ri >= ci)

    heads = lambda t: jnp.stack([t[:, h * n:(h + 1) * n] for h in range(gw // n)], axis=0)
    v_h, qb_h, kl_h, dec_h = heads(v), heads(qb), heads(kl), heads(dec)
    att = jnp.where(mask, _bmm(heads(qe), heads(ke), _BNT), 0.0)
    o_intra = _bmm(att, v_h)
    st = s_ref[...]
    o_inter = []
    for j in range(tb // cz):
        rows = slice(j * cz, (j + 1) * cz)
        o_inter.append(_bmm(qb_h[:, rows, :], st, _BNT))
        st = st * dec_h[:, j * cz:j * cz + 1, :] + _bmm(v_h[:, rows, :], kl_h[:, rows, :], _BTN)
    s_ref[...] = st
    o_heads = o_intra + jnp.concatenate(o_inter, axis=1)
    for h in range(gw // n):
        osc_ref[:, h * n:(h + 1) * n] = o_heads[h]

    o = osc_ref[...]
    ms = _head_sum(o * o, bd_ref[...]) * (1.0 / n)
    o_ref[...] = o * lax.rsqrt(ms + NORM_EPS) * gn_ref[...] * _silu(g)


def _hgrn(p3, lower_bounds, g_norm, layer):
    bsz, seq, _ = p3.shape
    tb = min(HGRN_TILE, seq)
    cz = HGRN_CHUNK
    gw = GROUP_WIDTH
    idx = np.arange(tb)
    same = (idx[:, None] // cz) == (idx[None, :] // cz)
    tri = same & (idx[:, None] >= idx[None, :])
    trim = same & ((idx[:, None] // cz) * cz + cz // 2 - 1 >= idx[None, :])
    tril = same
    consts = [jnp.asarray(t.astype(np.float32), BF16) for t in (tri, trim, tril)]
    ones_bd = jnp.asarray(np.kron(np.eye(4, dtype=np.float32), np.ones((HEAD_DIM, HEAD_DIM), np.float32)), BF16)
    gn = jnp.tile(g_norm.astype(F32), gw // HEAD_DIM).reshape(1, gw)
    params = [lower_bounds.astype(F32), gn] + consts + [ones_bd]
    full = lambda t: pl.BlockSpec(t.shape, lambda i, j: (0,) * t.ndim)
    col = lambda o: pl.BlockSpec((None, tb, 2 * gw), lambda i, j: (i, j, HGRN_OFF_BLOCKS + o))
    return pl.pallas_call(
        functools.partial(_hgrn_kernel, layer=layer),
        grid=(bsz, seq // tb),
        in_specs=[col(0), col(1)] + [full(t) for t in params],
        out_specs=pl.BlockSpec((None, tb, gw), lambda i, j: (i, j, 0)),
        out_shape=jax.ShapeDtypeStruct((bsz, seq, gw), F32),
        scratch_shapes=[pltpu.VMEM((gw // HEAD_DIM, HEAD_DIM, HEAD_DIM), F32),
                        pltpu.VMEM((tb, gw), F32)],
        compiler_params=pltpu.CompilerParams(
            dimension_semantics=("parallel", "arbitrary"), vmem_limit_bytes=VMEM_LIMIT),
        name="hgrn2",
    )(p3, p3, *params)


def kernel(x, positions, ffn1_norm, ffn1_w_gate, ffn1_w_up, ffn1_w_down, mix_norm, w_in, rwkv_mu, rwkv_w0, rwkv_w_up, rwkv_a0, rwkv_a_up, rwkv_g_up, rwkv_k_k, rwkv_k_a, rwkv_r_k, rwkv_ln_w, rwkv_ln_b, attn_q_norm, attn_k_norm, attn_sinks, s5_lambda_re, s5_lambda_im, s5_log_step, s5_b_re, s5_b_im, s5_c_re, s5_c_im, s5_d, s5_glu_w, s5_glu_b, hgrn_lower_bounds, hgrn_g_norm, w_out, ffn2_norm, ffn2_w_gate, ffn2_w_up, ffn2_w_down):
    bsz, seq, d = x.shape
    depth = w_in.shape[0]
    n = bsz * seq
    xf = x.reshape(n, d)
    for l in range(depth):
        xf = _ffn(xf, ffn1_norm[l], ffn1_w_gate[l].astype(BF16), ffn1_w_up[l].astype(BF16),
                  ffn1_w_down[l].astype(BF16))
        p3 = _proj(xf, mix_norm[l], w_in[l].astype(BF16)).reshape(bsz, seq, D_IN)
        y_a = _rwkv(p3, rwkv_mu[l], rwkv_w0[l], rwkv_w_up[l], rwkv_a0[l], rwkv_a_up[l], rwkv_g_up[l],
                    rwkv_k_k[l], rwkv_k_a[l], rwkv_r_k[l].reshape(-1), rwkv_ln_w[l], rwkv_ln_b[l])
        y_b = _attn(p3, positions, attn_q_norm[l], attn_k_norm[l], attn_sinks[l])
        y_c = _s5(p3, s5_lambda_re[l], s5_lambda_im[l], s5_log_step[l], s5_b_re[l], s5_b_im[l],
                  s5_c_re[l], s5_c_im[l], s5_d[l], s5_glu_w[l], s5_glu_b[l])
        y_d = _hgrn(p3, hgrn_lower_bounds, hgrn_g_norm[l], l)
        ys = [t.reshape(n, GROUP_WIDTH) for t in (y_a, y_b, y_c, y_d)]
        xf = _outproj(xf, ys, w_out[l].astype(BF16))
        xf = _ffn(xf, ffn2_norm[l], ffn2_w_gate[l].astype(BF16), ffn2_w_up[l].astype(BF16),
                  ffn2_w_down[l].astype(BF16))
    return xf.reshape(bsz, seq, d)
```

```python
import functools

import jax
import jax.numpy as jnp
import numpy as np
from jax import lax
from jax.experimental import pallas as pl
from jax.experimental.pallas import tpu as pltpu

F32 = jnp.float32
BF16 = jnp.bfloat16

HEAD_DIM = 64
GROUP_WIDTH = 512
NORM_EPS = 1e-6
FFN_RES_WEIGHT = 0.5

RWKV_W_RANK = 64
RWKV_A_RANK = 64
RWKV_G_RANK = 128
RWKV_LN_EPS = 64e-5
RWKV_IN = 3 * GROUP_WIDTH + RWKV_W_RANK + RWKV_A_RANK + RWKV_G_RANK
RWKV_CHUNK = 64
RWKV_SEQS = 4

ATTN_HEADS = 8
ATTN_KV_HEADS = 2
ATTN_BLOCK = 128
ATTN_SEQS = 2
ROPE_THETA = 500000.0
ROPE_DIM = HEAD_DIM // 4
ATTN_COL_BLOCK = 256
ATTN_OFF_BLOCKS = RWKV_IN // ATTN_COL_BLOCK

S5_GROUP = 16
S5_GROUPS = GROUP_WIDTH // S5_GROUP
S5_STATE = 64
S5_WIDTH = S5_GROUPS * S5_STATE
S5_OFF_BLOCKS = (RWKV_IN + 768) // GROUP_WIDTH
S5_STRIP = 512

HGRN_CHUNK = 16
HGRN_TILE = 128
HGRN_SEQS = 4
HGRN_OFF_BLOCKS = 3

D_IN = 5120
VMEM_LIMIT = 56 * 1024 * 1024


def _dot(a, b, dims):
    return lax.dot_general(a, b, (dims, ((), ())), preferred_element_type=F32)


_NN = ((1,), (0,))


def _mm(a, b):
    return _dot(a.astype(BF16), b.astype(BF16), _NN)


_BNN = (((2,), (1,)), ((0,), (0,)))
_BNT = (((2,), (2,)), ((0,), (0,)))
_BTN = (((1,), (1,)), ((0,), (0,)))


def _bmm(a, b, dims=_BNN):
    return lax.dot_general(a.astype(BF16), b.astype(BF16), dims, preferred_element_type=F32)


def _split3(x):
    hi = x.astype(BF16)
    r = x - hi.astype(F32)
    mid = r.astype(BF16)
    lo = (r - mid.astype(F32)).astype(BF16)
    return hi, mid, lo


def _const_lhs_mm(c, x):
    hi, mid, lo = _split3(x)
    return _dot(c, hi, _NN) + (_dot(c, mid, _NN) + _dot(c, lo, _NN))


def _const_rhs_mm(x, c):
    hi, mid, lo = _split3(x)
    return _dot(hi, c, _NN) + (_dot(mid, c, _NN) + _dot(lo, c, _NN))


def _head_sum(x, ones_bd):
    w = ones_bd.shape[0]
    parts = [_const_rhs_mm(x[:, i:i + w], ones_bd) for i in range(0, x.shape[1], w)]
    return parts[0] if len(parts) == 1 else jnp.concatenate(parts, axis=1)


def _silu(x):
    return x * jax.nn.sigmoid(x)


def _rms_rows(x, gain):
    ms = jnp.mean(x * x, axis=-1, keepdims=True)
    return x * lax.rsqrt(ms + NORM_EPS) * gain


def _ffn_kernel(x_ref, g_ref, wg_ref, wu_ref, wd_ref, o_ref, h_ref):
    @pl.when(pl.program_id(1) == 0)
    def _():
        x = x_ref[...]
        h_ref[...] = _rms_rows(x, g_ref[...]).astype(BF16)
        o_ref[...] = x

    h = h_ref[...]
    gate = jnp.dot(h, wg_ref[...], preferred_element_type=F32)
    up = jnp.dot(h, wu_ref[...], preferred_element_type=F32)
    act = (_silu(gate) * up).astype(BF16)
    o_ref[...] += FFN_RES_WEIGHT * jnp.dot(act, wd_ref[...], preferred_element_type=F32)


def _ffn(x2d, gain, wg, wu, wd):
    n, d = x2d.shape
    f = wg.shape[1]
    tm = min(1024, n)
    tf = 512 if f % 512 == 0 else f
    return pl.pallas_call(
        _ffn_kernel,
        grid=(n // tm, f // tf),
        in_specs=[
            pl.BlockSpec((tm, d), lambda i, j: (i, 0)),
            pl.BlockSpec((1, d), lambda i, j: (0, 0)),
            pl.BlockSpec((d, tf), lambda i, j: (0, j)),
            pl.BlockSpec((d, tf), lambda i, j: (0, j)),
            pl.BlockSpec((tf, d), lambda i, j: (j, 0)),
        ],
        out_specs=pl.BlockSpec((tm, d), lambda i, j: (i, 0)),
        out_shape=jax.ShapeDtypeStruct((n, d), F32),
        scratch_shapes=[pltpu.VMEM((tm, d), BF16)],
        compiler_params=pltpu.CompilerParams(
            dimension_semantics=("parallel", "arbitrary"), vmem_limit_bytes=VMEM_LIMIT),
        name="ffn",
    )(x2d, gain.reshape(1, d), wg, wu, wd)


def _proj_kernel(x_ref, g_ref, w_ref, o_ref, h_ref):
    @pl.when(pl.program_id(1) == 0)
    def _():
        h_ref[...] = _rms_rows(x_ref[...], g_ref[...]).astype(BF16)

    o_ref[...] = jnp.dot(h_ref[...], w_ref[...], preferred_element_type=F32)


def _proj(x2d, gain, w):
    n, d = x2d.shape
    dout = w.shape[1]
    tm = min(512, n)
    tn = dout // 2
    return pl.pallas_call(
        _proj_kernel,
        grid=(n // tm, dout // tn),
        in_specs=[
            pl.BlockSpec((tm, d), lambda i, j: (i, 0)),
            pl.BlockSpec((1, d), lambda i, j: (0, 0)),
            pl.BlockSpec((d, tn), lambda i, j: (0, j)),
        ],
        out_specs=pl.BlockSpec((tm, tn), lambda i, j: (i, j)),
        out_shape=jax.ShapeDtypeStruct((n, dout), F32),
        scratch_shapes=[pltpu.VMEM((tm, d), BF16)],
        compiler_params=pltpu.CompilerParams(
            dimension_semantics=("parallel", "arbitrary"), vmem_limit_bytes=VMEM_LIMIT),
        name="in_proj",
    )(x2d, gain.reshape(1, d), w)


def _outproj_kernel(x_ref, ya_ref, yb_ref, yc_ref, yd_ref, w_ref, o_ref):
    gw = GROUP_WIDTH
    acc = x_ref[...]
    for m, y_ref in enumerate((ya_ref, yb_ref, yc_ref, yd_ref)):
        acc = acc + jnp.dot(y_ref[...].astype(BF16), w_ref[m * gw:(m + 1) * gw, :],
                            preferred_element_type=F32)
    o_ref[...] = acc


def _outproj(x2d, ys, w):
    n, d = x2d.shape
    tm = min(256, n)
    yspec = pl.BlockSpec((tm, GROUP_WIDTH), lambda i: (i, 0))
    return pl.pallas_call(
        _outproj_kernel,
        grid=(n // tm,),
        in_specs=[pl.BlockSpec((tm, d), lambda i: (i, 0)), yspec, yspec, yspec, yspec,
                  pl.BlockSpec(w.shape, lambda i: (0, 0))],
        out_specs=pl.BlockSpec((tm, d), lambda i: (i, 0)),
        out_shape=jax.ShapeDtypeStruct((n, d), F32),
        compiler_params=pltpu.CompilerParams(
            dimension_semantics=("parallel",), vmem_limit_bytes=VMEM_LIMIT),
        name="out_proj",
    )(x2d, *ys, w)


def _rwkv_kernel(p_ref, mu_ref, w0_ref, wup_ref, a0_ref, aup_ref, gup_ref, kk_ref, ka_ref, rk_ref,
                 lnw_ref, lnb_ref, tri_ref, bd_ref, o_ref, s_ref, prev_ref, osc_ref):
    nb, c, _ = p_ref.shape
    gw = GROUP_WIDTH
    n = HEAD_DIM
    nh = gw // n

    @pl.when(pl.program_id(1) == 0)
    def _():
        s_ref[...] = jnp.zeros_like(s_ref)
        prev_ref[...] = jnp.zeros_like(prev_ref)

    p = p_ref[...].reshape(nb * c, RWKV_IN)
    row = lax.broadcasted_iota(jnp.int32, p.shape, 0)
    shifted = pltpu.roll(p, 1, axis=0)
    for s in range(nb):
        shifted = jnp.where(row == s * c, prev_ref[s], shifted)
        prev_ref[s] = p[(s + 1) * c - 1:(s + 1) * c, :]
    p = p + (shifted - p) * mu_ref[...]

    r = p[:, 0:gw]
    k = p[:, gw:2 * gw]
    v = p[:, 2 * gw:3 * gw]
    o1 = 3 * gw
    w_lo = p[:, o1:o1 + RWKV_W_RANK]
    a_lo = p[:, o1 + RWKV_W_RANK:o1 + RWKV_W_RANK + RWKV_A_RANK]
    g_lo = p[:, o1 + RWKV_W_RANK + RWKV_A_RANK:]

    z = -(w0_ref[...] + _mm(jnp.tanh(w_lo), wup_ref[...]))
    softplus = jnp.maximum(z, 0.0) + jnp.log1p(jnp.exp(-jnp.abs(z)))
    lw = -jnp.exp(-softplus - 0.5)
    a = jax.nn.sigmoid(a0_ref[...] + _mm(a_lo, aup_ref[...]))
    g = _mm(jax.nn.sigmoid(g_lo), gup_ref[...])

    ones_bd = bd_ref[...]
    kk = k * kk_ref[...]
    kk = kk * lax.rsqrt(jnp.maximum(_head_sum(kk * kk, ones_bd), 1e-24))
    k = k * (1.0 + (a - 1.0) * ka_ref[...])
    b = kk * a

    cum = _const_lhs_mm(tri_ref[...], lw)
    cum_last = jnp.concatenate(
        [jnp.broadcast_to(cum[(s + 1) * c - 1:(s + 1) * c, :], (c, gw)) for s in range(nb)], axis=0)
    e_pos = jnp.exp(cum)
    e_neg = jnp.exp(-cum)
    e_end = jnp.exp(cum_last - cum)
    g_end = jnp.exp(cum_last)
    kq = kk * jnp.exp(cum - lw)
    rq = r * e_pos
    bd = b * e_neg
    kd = k * e_neg
    b_end = b * e_end
    k_end = k * e_end

    ri = lax.broadcasted_iota(jnp.int32, (c, c), 0)
    ci = lax.broadcasted_iota(jnp.int32, (c, c), 1)
    strict = ri > ci
    incl = ri >= ci
    eye = (ri == ci).astype(F32)

    heads = lambda t, rows=c: jnp.stack(
        [t[s * c:s * c + rows, h * n:(h + 1) * n] for s in range(nb) for h in range(nh)], axis=0)
    kq_h, rq_h, bd_h, kd_h, v_h = heads(kq), heads(rq), heads(bd), heads(kd), heads(v)
    s0 = s_ref[...]
    a_bb = jnp.where(strict, _bmm(kq_h, bd_h, _BNT), 0.0)
    a_bk = jnp.where(strict, _bmm(kq_h, kd_h, _BNT), 0.0)
    a_rb = jnp.where(incl, _bmm(rq_h, bd_h, _BNT), 0.0)
    a_rk = jnp.where(incl, _bmm(rq_h, kd_h, _BNT), 0.0)
    pw = -a_bb
    t_inv = eye + pw
    for _ in range(int(np.log2(c)) - 1):
        pw = _bmm(pw, pw)
        t_inv = t_inv + _bmm(t_inv, pw)
    x = _bmm(kq_h, s0, _BNT) + _bmm(a_bk, v_h)
    u = -_bmm(t_inv, x)
    y = _bmm(rq_h, s0, _BNT) + _bmm(a_rb, u) + _bmm(a_rk, v_h)
    s_ref[...] = s0 * heads(g_end, 1) + _bmm(u, heads(b_end), _BTN) + _bmm(v_h, heads(k_end), _BTN)
    for s in range(nb):
        for h in range(nh):
            osc_ref[s * c:(s + 1) * c, h * n:(h + 1) * n] = y[s * nh + h]

    o = osc_ref[...]
    inv_n = 1.0 / n
    mean = _head_sum(o, ones_bd) * inv_n
    dlt = o - mean
    var = _head_sum(dlt * dlt, ones_bd) * inv_n
    o = dlt * lax.rsqrt(var + RWKV_LN_EPS) * lnw_ref[...] + lnb_ref[...]
    bonus = _head_sum(r * k * rk_ref[...], ones_bd) * v
    o_ref[...] = ((o + bonus) * g).reshape(nb, c, gw)


def _rwkv(p3, mu, w0, w_up, a0, a_up, g_up, k_k, k_a, r_k, ln_w, ln_b):
    bsz, seq, _ = p3.shape
    c = RWKV_CHUNK
    gw = GROUP_WIDTH
    nb = RWKV_SEQS if bsz % RWKV_SEQS == 0 else 1
    tri = jnp.asarray(np.kron(np.eye(nb, dtype=np.float32), np.tril(np.ones((c, c), np.float32))), BF16)
    ones_bd = jnp.asarray(np.kron(np.eye(4, dtype=np.float32), np.ones((HEAD_DIM, HEAD_DIM), np.float32)), BF16)
    row = lambda t: t.reshape(1, -1)
    full = lambda t: pl.BlockSpec(t.shape, lambda i, j: (0,) * t.ndim)
    params = [row(mu), row(w0), w_up, row(a0), a_up, g_up, row(k_k), row(k_a), row(r_k), row(ln_w),
              row(ln_b), tri, ones_bd]
    return pl.pallas_call(
        _rwkv_kernel,
        grid=(bsz // nb, seq // c),
        in_specs=[pl.BlockSpec((nb, c, RWKV_IN), lambda i, j: (i, j, 0))] + [full(t) for t in params],
        out_specs=pl.BlockSpec((nb, c, gw), lambda i, j: (i, j, 0)),
        out_shape=jax.ShapeDtypeStruct((bsz, seq, gw), F32),
        scratch_shapes=[pltpu.VMEM((nb * gw // HEAD_DIM, HEAD_DIM, HEAD_DIM), F32),
                        pltpu.VMEM((nb, 1, RWKV_IN), F32),
                        pltpu.VMEM((nb * c, gw), F32)],
        compiler_params=pltpu.CompilerParams(
            dimension_semantics=("parallel", "arbitrary"), vmem_limit_bytes=VMEM_LIMIT),
        name="rwkv7",
    )(p3, *params)


def _rope_table_kernel(pos_ref, invf_ref, cos_ref, sin_ref):
    ang = pos_ref[...].astype(F32) * invf_ref[...]
    cos_ref[...] = jnp.cos(ang)
    sin_ref[...] = jnp.sin(ang)


def _rope_tables(positions):
    bsz, seq = positions.shape
    d = HEAD_DIM
    w = 2 * d
    inv_freq = ROPE_THETA ** (-jnp.arange(0, ROPE_DIM, 2, dtype=F32) / ROPE_DIM)
    lane_d = np.arange(w) % d
    invf = jnp.where(lane_d < ROPE_DIM, inv_freq[lane_d % (ROPE_DIM // 2)], 0.0).reshape(1, w)
    tr = min(1024, bsz * seq)
    return pl.pallas_call(
        _rope_table_kernel,
        grid=(bsz * seq // tr,),
        in_specs=[pl.BlockSpec((tr, 1), lambda i: (i, 0)), pl.BlockSpec((1, w), lambda i: (0, 0))],
        out_specs=[pl.BlockSpec((tr, w), lambda i: (i, 0))] * 2,
        out_shape=[jax.ShapeDtypeStruct((bsz * seq, w), F32)] * 2,
        compiler_params=pltpu.CompilerParams(dimension_semantics=("parallel",)),
        name="rope_tables",
    )(positions.reshape(bsz * seq, 1), invf)


def _attn_kernel(cos_ref, sin_ref, q0_ref, q1_ref, kv_ref, qn_ref, kn_ref, sink_ref, bd_ref,
                 o_ref, kprev_ref, vprev_ref):
    nb, blk, _ = q0_ref.shape
    rows = nb * blk
    d = HEAD_DIM
    half = ROPE_DIM // 2
    group = ATTN_HEADS // ATTN_KV_HEADS
    kvw = ATTN_KV_HEADS * d
    first = pl.program_id(1) == 0

    @pl.when(first)
    def _():
        kprev_ref[...] = jnp.zeros_like(kprev_ref)
        vprev_ref[...] = jnp.zeros_like(vprev_ref)

    ones_bd = bd_ref[...]
    cos = cos_ref[...].reshape(rows, 2 * d)
    sin = sin_ref[...].reshape(rows, 2 * d)

    def norm_rope(x, gain):
        w = x.shape[1]
        x = x * lax.rsqrt(_head_sum(x * x, ones_bd[:w, :w]) * (1.0 / d) + NORM_EPS) * gain
        ld = lax.broadcasted_iota(jnp.int32, (rows, w), 1) % d
        rot = jnp.where(ld < half, -pltpu.roll(x, w - half, axis=1),
                        jnp.where(ld < ROPE_DIM, pltpu.roll(x, half, axis=1), 0.0))
        tile = lambda t: t if w == t.shape[1] else jnp.concatenate([t] * (w // t.shape[1]), axis=1)
        return x * tile(cos) + rot * tile(sin)

    qs = [norm_rope(q_ref[...].reshape(rows, ATTN_COL_BLOCK), qn_ref[...]) * (d ** -0.5)
          for q_ref in (q0_ref, q1_ref)]
    kv = kv_ref[...].reshape(rows, ATTN_COL_BLOCK)
    k_cur = norm_rope(kv[:, :kvw], kn_ref[:, :kvw])
    v_cur = kv[:, kvw:]
    k_prev = kprev_ref[...]
    v_prev = vprev_ref[...]

    heads_per_ref = ATTN_COL_BLOCK // d
    q_head = lambda s, qh: qs[qh // heads_per_ref][s * blk:(s + 1) * blk,
                                                   (qh % heads_per_ref) * d:(qh % heads_per_ref + 1) * d]
    q_g = jnp.stack([jnp.concatenate([q_head(s, kh * group + i) for i in range(group)], axis=0)
                     for s in range(nb) for kh in range(ATTN_KV_HEADS)], axis=0)
    kv_heads = lambda t: jnp.stack([t[s * blk:(s + 1) * blk, kh * d:(kh + 1) * d]
                                    for s in range(nb) for kh in range(ATTN_KV_HEADS)], axis=0)

    ri = lax.broadcasted_iota(jnp.int32, (group * blk, blk), 0) % blk
    ci = lax.broadcasted_iota(jnp.int32, (group * blk, blk), 1)
    neg_inf = -jnp.inf
    no_prev = jnp.where(first, neg_inf, 0.0)
    s_prev = jnp.where(ci > ri, _bmm(q_g, kv_heads(k_prev), _BNT), neg_inf) + no_prev
    s_cur = jnp.where(ci <= ri, _bmm(q_g, kv_heads(k_cur), _BNT), neg_inf)
    sink = sink_ref[...]
    m = jnp.maximum(jnp.maximum(jnp.max(s_prev, axis=2, keepdims=True),
                                jnp.max(s_cur, axis=2, keepdims=True)), sink)
    p_prev = jnp.exp(s_prev - m)
    p_cur = jnp.exp(s_cur - m)
    den = (jnp.sum(p_prev, axis=2, keepdims=True) + jnp.sum(p_cur, axis=2, keepdims=True)
           + jnp.exp(sink - m))
    o = (_bmm(p_prev, kv_heads(v_prev)) + _bmm(p_cur, kv_heads(v_cur))) / den
    for s in range(nb):
        for qh in range(ATTN_HEADS):
            o_ref[s, :, qh * d:(qh + 1) * d] = o[s * ATTN_KV_HEADS + qh // group,
                                                 (qh % group) * blk:(qh % group + 1) * blk, :]

    kprev_ref[...] = k_cur
    vprev_ref[...] = v_cur


def _attn(p3, rope_cos, rope_sin, q_norm, k_norm, sinks):
    bsz, seq, _ = p3.shape
    blk = ATTN_BLOCK
    cw = ATTN_COL_BLOCK
    d = HEAD_DIM
    nb = ATTN_SEQS if bsz % ATTN_SEQS == 0 else 1
    qn = jnp.tile(q_norm.astype(F32), cw // d).reshape(1, cw)
    kn = jnp.tile(k_norm.astype(F32), cw // d).reshape(1, cw)
    ones_bd = jnp.asarray(np.kron(np.eye(cw // d, dtype=np.float32), np.ones((d, d), np.float32)), BF16)
    group = ATTN_HEADS // ATTN_KV_HEADS
    sink_col = jnp.repeat(sinks.astype(F32).reshape(ATTN_KV_HEADS, group), blk, axis=1)[:, :, None]
    sink_col = jnp.tile(sink_col, (nb, 1, 1))
    tw = rope_cos.shape[-1]
    cos3 = rope_cos.reshape(bsz, seq, tw)
    sin3 = rope_sin.reshape(bsz, seq, tw)
    full = lambda t: pl.BlockSpec(t.shape, lambda i, j: (0,) * t.ndim)
    col = lambda o: pl.BlockSpec((nb, blk, cw), lambda i, j: (i, j, ATTN_OFF_BLOCKS + o))
    trig = pl.BlockSpec((nb, blk, tw), lambda i, j: (i, j, 0))
    return pl.pallas_call(
        _attn_kernel,
        grid=(bsz // nb, seq // blk),
        in_specs=[trig, trig, col(0), col(1), col(2), full(qn), full(kn), full(sink_col), full(ones_bd)],
        out_specs=pl.BlockSpec((nb, blk, GROUP_WIDTH), lambda i, j: (i, j, 0)),
        out_shape=jax.ShapeDtypeStruct((bsz, seq, GROUP_WIDTH), F32),
        scratch_shapes=[pltpu.VMEM((nb * blk, ATTN_KV_HEADS * d), F32),
                        pltpu.VMEM((nb * blk, ATTN_KV_HEADS * d), F32)],
        compiler_params=pltpu.CompilerParams(
            dimension_semantics=("parallel", "arbitrary"), vmem_limit_bytes=VMEM_LIMIT),
        name="swa_attn",
    )(cos3, sin3, p3, p3, p3, qn, kn, sink_col, ones_bd)


def _s5_prep_kernel(lre_ref, lim_ref, ls_ref, bre_ref, bim_ref, are_ref, aim_ref, obre_ref, obim_ref):
    lre = lre_ref[...]
    lim = lim_ref[...]
    dt = jnp.exp(ls_ref[...])
    mag = jnp.exp(lre * dt)
    are = mag * jnp.cos(lim * dt)
    aim = mag * jnp.sin(lim * dt)
    are_ref[...] = are
    aim_ref[...] = aim
    inv = 1.0 / (lre * lre + lim * lim)
    cre = ((are - 1.0) * lre + aim * lim) * inv
    cim = (aim * lre - (are - 1.0) * lim) * inv
    bre = bre_ref[...]
    bim = bim_ref[...]
    obre_ref[...] = cre[:, None, :] * bre - cim[:, None, :] * bim
    obim_ref[...] = cre[:, None, :] * bim + cim[:, None, :] * bre


def _s5_kernel(u_ref, are_ref, aim_ref, b_ref, c_ref, d_ref, gw_ref, gb_ref, o_ref, x_ref, s_ref):
    tc, bsz, ch = u_ref.shape
    hw = S5_WIDTH // 2
    hc = ch // 2

    @pl.when(pl.program_id(0) == 0)
    def _():
        s_ref[...] = jnp.zeros_like(s_ref)

    u = u_ref[...].reshape(tc * bsz, ch)
    for hf in range(2):
        x_ref[:, 2 * hw * hf:2 * hw * (hf + 1)] = _mm(u[:, hc * hf:hc * (hf + 1)], b_ref[hf])

    for hf in range(2):
        for s0 in range(0, hw, S5_STRIP):
            re = slice(2 * hw * hf + s0, 2 * hw * hf + s0 + S5_STRIP)
            im = slice(2 * hw * hf + hw + s0, 2 * hw * hf + hw + s0 + S5_STRIP)
            lam = slice(hw * hf + s0, hw * hf + s0 + S5_STRIP)
            ar = jnp.broadcast_to(are_ref[:, lam], (bsz, S5_STRIP))
            ai = jnp.broadcast_to(aim_ref[:, lam], (bsz, S5_STRIP))

            def step(t, carry, re=re, im=im, ar=ar, ai=ai):
                sr, si = carry
                rows = pl.ds(pl.multiple_of(t * bsz, bsz), bsz)
                nr = ar * sr - ai * si + x_ref[rows, re]
                ni = ar * si + ai * sr + x_ref[rows, im]
                x_ref[rows, re] = nr
                x_ref[rows, im] = ni
                return nr, ni

            sr, si = lax.fori_loop(0, tc, step, (s_ref[:, re], s_ref[:, im]), unroll=4)
            s_ref[:, re] = sr
            s_ref[:, im] = si

    y = jnp.concatenate([_mm(x_ref[:, 2 * hw * hf:2 * hw * (hf + 1)], c_ref[hf]) for hf in range(2)], axis=1)
    y = y + d_ref[...] * u
    z = 0.5 * y * (1.0 + lax.erf(y * (2.0 ** -0.5)))
    out = z * jax.nn.sigmoid(_mm(z, gw_ref[...]) + gb_ref[...])
    o_ref[...] = out.reshape(tc, bsz, ch)


def _s5(p3, lam_re, lam_im, log_step, b_re, b_im, c_re, c_im, d_skip, glu_w, glu_b):
    bsz, seq, _ = p3.shape
    g, st, ch = S5_GROUPS, S5_STATE, S5_GROUP
    gwd = GROUP_WIDTH
    vm = pl.BlockSpec(memory_space=pltpu.VMEM)
    a_re, a_im, bb_re, bb_im = pl.pallas_call(
        _s5_prep_kernel,
        in_specs=[vm] * 5,
        out_specs=[vm] * 4,
        out_shape=[jax.ShapeDtypeStruct((g, st), F32)] * 2 + [jax.ShapeDtypeStruct((g, ch, st), F32)] * 2,
        name="s5_prep",
    )(lam_re, lam_im, log_step.reshape(g, 1), jnp.swapaxes(b_re, 1, 2), jnp.swapaxes(b_im, 1, 2))

    gh = g // 2
    eye = jnp.eye(gh, dtype=F32)
    bd_in = lambda t: (t[:, :, None, :] * eye[:, None, :, None]).reshape(gh * ch, gh * st)
    bd_out = lambda t: (jnp.swapaxes(t, 1, 2)[:, :, None, :] * eye[:, None, :, None]).reshape(gh * st, gh * ch)
    halves = lambda t: (t[:gh], t[gh:])
    b_mat = jnp.stack([jnp.concatenate([bd_in(r), bd_in(i)], axis=1)
                       for r, i in zip(halves(bb_re), halves(bb_im))]).astype(BF16)
    c_mat = jnp.stack([jnp.concatenate([bd_out(r), -bd_out(i)], axis=0)
                       for r, i in zip(halves(c_re.astype(F32)), halves(c_im.astype(F32)))]).astype(BF16)

    tc = min(64, seq)
    u_t = jnp.swapaxes(p3[:, :, S5_OFF_BLOCKS * gwd:(S5_OFF_BLOCKS + 1) * gwd], 0, 1)
    full = lambda t: pl.BlockSpec(t.shape, lambda i: (0,) * t.ndim)
    params = [a_re.reshape(1, S5_WIDTH), a_im.reshape(1, S5_WIDTH), b_mat, c_mat,
              d_skip.reshape(1, gwd), glu_w.astype(BF16), glu_b.reshape(1, gwd)]
    y_t = pl.pallas_call(
        _s5_kernel,
        grid=(seq // tc,),
        in_specs=[pl.BlockSpec((tc, bsz, gwd), lambda i: (i, 0, 0))] + [full(t) for t in params],
        out_specs=pl.BlockSpec((tc, bsz, gwd), lambda i: (i, 0, 0)),
        out_shape=jax.ShapeDtypeStruct((seq, bsz, gwd), F32),
        scratch_shapes=[pltpu.VMEM((tc * bsz, 2 * S5_WIDTH), F32), pltpu.VMEM((bsz, 2 * S5_WIDTH), F32)],
        compiler_params=pltpu.CompilerParams(
            dimension_semantics=("arbitrary",), vmem_limit_bytes=VMEM_LIMIT),
        name="s5_scan",
    )(u_t, *params)
    return jnp.swapaxes(y_t, 0, 1)


def _hgrn_kernel(qf_ref, ig_ref, lbraw_ref, gn_ref, tri_ref, trim_ref, tril_ref, bd_ref, o_ref,
                 s_ref, osc_ref, *, layer):
    nb, tb, _ = qf_ref.shape
    cz = HGRN_CHUNK
    gw = GROUP_WIDTH
    n = HEAD_DIM
    nh = gw // n

    @pl.when(pl.program_id(1) == 0)
    def _():
        s_ref[...] = jnp.zeros_like(s_ref)

    qf = qf_ref[...].reshape(nb * tb, 2 * gw)
    ig = ig_ref[...].reshape(nb * tb, 2 * gw)
    q, f = qf[:, :gw], qf[:, gw:]
    v, g = ig[:, :gw], ig[:, gw:]

    lbr = lbraw_ref[...]
    e = jnp.exp(lbr - jnp.max(lbr, axis=0, keepdims=True))
    sm = e / jnp.sum(e, axis=0, keepdims=True)
    lb = jnp.zeros((1, gw), F32)
    for i in range(1, layer + 1):
        lb = lb + sm[i:i + 1, :]

    q = _silu(q) * (n ** -0.5)
    f_gate = lb + (1.0 - lb) * jax.nn.sigmoid(f)
    log_f = jnp.log(f_gate)
    k = 1.0 - f_gate

    parts = _split3(log_f)

    def per_seq(c_ref):
        cm = c_ref[...]
        seq_rows = lambda t, s: t[s * tb:(s + 1) * tb, :]
        return jnp.concatenate(
            [_dot(cm, seq_rows(parts[0], s), _NN)
             + (_dot(cm, seq_rows(parts[1], s), _NN) + _dot(cm, seq_rows(parts[2], s), _NN))
             for s in range(nb)], axis=0)

    bcum = per_seq(tri_ref)
    bmid = per_seq(trim_ref)
    blast = per_seq(tril_ref)
    qe = q * jnp.exp(bcum - bmid)
    ke = k * jnp.exp(bmid - bcum)
    kl = k * jnp.exp(blast - bcum)
    qb = q * jnp.exp(bcum)
    dec = jnp.exp(blast)

    ri = lax.broadcasted_iota(jnp.int32, (tb, tb), 0)
    ci = lax.broadcasted_iota(jnp.int32, (tb, tb), 1)
    mask = jnp.logical_and(ri // cz == ci // cz, ri >= ci)

    heads = lambda t: jnp.stack(
        [t[s * tb:(s + 1) * tb, h * n:(h + 1) * n] for s in range(nb) for h in range(nh)], axis=0)
    v_h, qb_h, kl_h, dec_h = heads(v), heads(qb), heads(kl), heads(dec)
    att = jnp.where(mask, _bmm(heads(qe), heads(ke), _BNT), 0.0)
    o_intra = _bmm(att, v_h)
    st = s_ref[...]
    o_inter = []
    for j in range(tb // cz):
        rows = slice(j * cz, (j + 1) * cz)
        o_inter.append(_bmm(qb_h[:, rows, :], st, _BNT))
        st = st * dec_h[:, j * cz:j * cz + 1, :] + _bmm(v_h[:, rows, :], kl_h[:, rows, :], _BTN)
    s_ref[...] = st
    o_heads = o_intra + jnp.concatenate(o_inter, axis=1)
    for s in range(nb):
        for h in range(nh):
            osc_ref[s * tb:(s + 1) * tb, h * n:(h + 1) * n] = o_heads[s * nh + h]

    o = osc_ref[...]
    ms = _head_sum(o * o, bd_ref[...]) * (1.0 / n)
    o_ref[...] = (o * lax.rsqrt(ms + NORM_EPS) * gn_ref[...] * _silu(g)).reshape(nb, tb, gw)


def _hgrn(p3, lower_bounds, g_norm, layer):
    bsz, seq, _ = p3.shape
    tb = min(HGRN_TILE, seq)
    cz = HGRN_CHUNK
    gw = GROUP_WIDTH
    idx = np.arange(tb)
    same = (idx[:, None] // cz) == (idx[None, :] // cz)
    tri = same & (idx[:, None] >= idx[None, :])
    trim = same & ((idx[:, None] // cz) * cz + cz // 2 - 1 >= idx[None, :])
    tril = same
    consts = [jnp.asarray(t.astype(np.float32), BF16) for t in (tri, trim, tril)]
    ones_bd = jnp.asarray(np.kron(np.eye(4, dtype=np.float32), np.ones((HEAD_DIM, HEAD_DIM), np.float32)), BF16)
    gn = jnp.tile(g_norm.astype(F32), gw // HEAD_DIM).reshape(1, gw)
    params = [lower_bounds.astype(F32), gn] + consts + [ones_bd]
    full = lambda t: pl.BlockSpec(t.shape, lambda i, j: (0,) * t.ndim)
    nb = HGRN_SEQS if bsz % HGRN_SEQS == 0 else 1
    col = lambda o: pl.BlockSpec((nb, tb, 2 * gw), lambda i, j: (i, j, HGRN_OFF_BLOCKS + o))
    return pl.pallas_call(
        functools.partial(_hgrn_kernel, layer=layer),
        grid=(bsz // nb, seq // tb),
        in_specs=[col(0), col(1)] + [full(t) for t in params],
        out_specs=pl.BlockSpec((nb, tb, gw), lambda i, j: (i, j, 0)),
        out_shape=jax.ShapeDtypeStruct((bsz, seq, gw), F32),
        scratch_shapes=[pltpu.VMEM((nb * gw // HEAD_DIM, HEAD_DIM, HEAD_DIM), F32),
                        pltpu.VMEM((nb * tb, gw), F32)],
        compiler_params=pltpu.CompilerParams(
            dimension_semantics=("parallel", "arbitrary"), vmem_limit_bytes=VMEM_LIMIT),
        name="hgrn2",
    )(p3, p3, *params)


def kernel(x, positions, ffn1_norm, ffn1_w_gate, ffn1_w_up, ffn1_w_down, mix_norm, w_in, rwkv_mu, rwkv_w0, rwkv_w_up, rwkv_a0, rwkv_a_up, rwkv_g_up, rwkv_k_k, rwkv_k_a, rwkv_r_k, rwkv_ln_w, rwkv_ln_b, attn_q_norm, attn_k_norm, attn_sinks, s5_lambda_re, s5_lambda_im, s5_log_step, s5_b_re, s5_b_im, s5_c_re, s5_c_im, s5_d, s5_glu_w, s5_glu_b, hgrn_lower_bounds, hgrn_g_norm, w_out, ffn2_norm, ffn2_w_gate, ffn2_w_up, ffn2_w_down):
    bsz, seq, d = x.shape
    depth = w_in.shape[0]
    n = bsz * seq
    xf = x.reshape(n, d)
    rope_cos, rope_sin = _rope_tables(positions)
    for l in range(depth):
        xf = _ffn(xf, ffn1_norm[l], ffn1_w_gate[l].astype(BF16), ffn1_w_up[l].astype(BF16),
                  ffn1_w_down[l].astype(BF16))
        p3 = _proj(xf, mix_norm[l], w_in[l].astype(BF16)).reshape(bsz, seq, D_IN)
        y_a = _rwkv(p3, rwkv_mu[l], rwkv_w0[l], rwkv_w_up[l], rwkv_a0[l], rwkv_a_up[l], rwkv_g_up[l],
                    rwkv_k_k[l], rwkv_k_a[l], rwkv_r_k[l].reshape(-1), rwkv_ln_w[l], rwkv_ln_b[l])
        y_b = _attn(p3, rope_cos, rope_sin, attn_q_norm[l], attn_k_norm[l], attn_sinks[l])
        y_c = _s5(p3, s5_lambda_re[l], s5_lambda_im[l], s5_log_step[l], s5_b_re[l], s5_b_im[l],
                  s5_c_re[l], s5_c_im[l], s5_d[l], s5_glu_w[l], s5_glu_b[l])
        y_d = _hgrn(p3, hgrn_lower_bounds, hgrn_g_norm[l], l)
        ys = [t.reshape(n, GROUP_WIDTH) for t in (y_a, y_b, y_c, y_d)]
        xf = _outproj(xf, ys, w_out[l].astype(BF16))
        xf = _ffn(xf, ffn2_norm[l], ffn2_w_gate[l].astype(BF16), ffn2_w_up[l].astype(BF16),
                  ffn2_w_down[l].astype(BF16))
    return xf.reshape(bsz, seq, d)
```

```python
import functools

import jax
import jax.numpy as jnp
import numpy as np
from jax import lax
from jax.experimental import pallas as pl
from jax.experimental.pallas import tpu as pltpu

F32 = jnp.float32
BF16 = jnp.bfloat16

HEAD_DIM = 64
GROUP_WIDTH = 512
NORM_EPS = 1e-6
FFN_RES_WEIGHT = 0.5

RWKV_W_RANK = 64
RWKV_A_RANK = 64
RWKV_G_RANK = 128
RWKV_LN_EPS = 64e-5
RWKV_IN = 3 * GROUP_WIDTH + RWKV_W_RANK + RWKV_A_RANK + RWKV_G_RANK
RWKV_CHUNK = 64
RWKV_SEQS = 4

ATTN_HEADS = 8
ATTN_KV_HEADS = 2
ATTN_BLOCK = 128
ATTN_SEQS = 2
ROPE_THETA = 500000.0
ROPE_DIM = HEAD_DIM // 4
ATTN_COL_BLOCK = 256
ATTN_OFF_BLOCKS = RWKV_IN // ATTN_COL_BLOCK

S5_GROUP = 16
S5_GROUPS = GROUP_WIDTH // S5_GROUP
S5_STATE = 64
S5_WIDTH = S5_GROUPS * S5_STATE
S5_OFF_BLOCKS = (RWKV_IN + 768) // GROUP_WIDTH
S5_STRIP = 512

HGRN_CHUNK = 16
HGRN_TILE = 128
HGRN_SEQS = 4
HGRN_OFF_BLOCKS = 3

D_IN = 5120
VMEM_LIMIT = 56 * 1024 * 1024


def _dot(a, b, dims):
    return lax.dot_general(a, b, (dims, ((), ())), preferred_element_type=F32)


_NN = ((1,), (0,))


def _mm(a, b):
    return _dot(a.astype(BF16), b.astype(BF16), _NN)


_BNN = (((2,), (1,)), ((0,), (0,)))
_BNT = (((2,), (2,)), ((0,), (0,)))
_BTN = (((1,), (1,)), ((0,), (0,)))


def _bmm(a, b, dims=_BNN):
    return lax.dot_general(a.astype(BF16), b.astype(BF16), dims, preferred_element_type=F32)


def _split3(x):
    hi = x.astype(BF16)
    r = x - hi.astype(F32)
    mid = r.astype(BF16)
    lo = (r - mid.astype(F32)).astype(BF16)
    return hi, mid, lo


def _const_lhs_mm(c, x):
    hi, mid, lo = _split3(x)
    return _dot(c, hi, _NN) + (_dot(c, mid, _NN) + _dot(c, lo, _NN))


def _const_rhs_mm(x, c):
    hi, mid, lo = _split3(x)
    return _dot(hi, c, _NN) + (_dot(mid, c, _NN) + _dot(lo, c, _NN))


def _head_sum(x, ones_bd):
    w = ones_bd.shape[0]
    parts = [_const_rhs_mm(x[:, i:i + w], ones_bd) for i in range(0, x.shape[1], w)]
    return parts[0] if len(parts) == 1 else jnp.concatenate(parts, axis=1)


def _silu(x):
    return x * jax.nn.sigmoid(x)


def _rms_rows(x, gain):
    ms = jnp.mean(x * x, axis=-1, keepdims=True)
    return x * lax.rsqrt(ms + NORM_EPS) * gain


def _ffn_kernel(x_ref, g_ref, wg_ref, wu_ref, wd_ref, o_ref, h_ref):
    @pl.when(pl.program_id(1) == 0)
    def _():
        x = x_ref[...]
        h_ref[...] = _rms_rows(x, g_ref[...]).astype(BF16)
        o_ref[...] = x

    h = h_ref[...]
    gate = jnp.dot(h, wg_ref[...], preferred_element_type=F32)
    up = jnp.dot(h, wu_ref[...], preferred_element_type=F32)
    act = (_silu(gate) * up).astype(BF16)
    o_ref[...] += FFN_RES_WEIGHT * jnp.dot(act, wd_ref[...], preferred_element_type=F32)


def _ffn(x2d, gain, wg, wu, wd):
    n, d = x2d.shape
    f = wg.shape[1]
    tm = min(1024, n)
    tf = 512 if f % 512 == 0 else f
    return pl.pallas_call(
        _ffn_kernel,
        grid=(n // tm, f // tf),
        in_specs=[
            pl.BlockSpec((tm, d), lambda i, j: (i, 0)),
            pl.BlockSpec((1, d), lambda i, j: (0, 0)),
            pl.BlockSpec((d, tf), lambda i, j: (0, j)),
            pl.BlockSpec((d, tf), lambda i, j: (0, j)),
            pl.BlockSpec((tf, d), lambda i, j: (j, 0)),
        ],
        out_specs=pl.BlockSpec((tm, d), lambda i, j: (i, 0)),
        out_shape=jax.ShapeDtypeStruct((n, d), F32),
        scratch_shapes=[pltpu.VMEM((tm, d), BF16)],
        compiler_params=pltpu.CompilerParams(
            dimension_semantics=("parallel", "arbitrary"), vmem_limit_bytes=VMEM_LIMIT),
        name="ffn",
    )(x2d, gain.reshape(1, d), wg, wu, wd)


def _proj_kernel(x_ref, g_ref, w_ref, o_ref, h_ref):
    @pl.when(pl.program_id(1) == 0)
    def _():
        h_ref[...] = _rms_rows(x_ref[...], g_ref[...]).astype(BF16)

    o_ref[...] = jnp.dot(h_ref[...], w_ref[...], preferred_element_type=F32)


def _proj(x2d, gain, w):
    n, d = x2d.shape
    dout = w.shape[1]
    tm = min(1024, n)
    tn = dout // 4
    return pl.pallas_call(
        _proj_kernel,
        grid=(n // tm, dout // tn),
        in_specs=[
            pl.BlockSpec((tm, d), lambda i, j: (i, 0)),
            pl.BlockSpec((1, d), lambda i, j: (0, 0)),
            pl.BlockSpec((d, tn), lambda i, j: (0, j)),
        ],
        out_specs=pl.BlockSpec((tm, tn), lambda i, j: (i, j)),
        out_shape=jax.ShapeDtypeStruct((n, dout), F32),
        scratch_shapes=[pltpu.VMEM((tm, d), BF16)],
        compiler_params=pltpu.CompilerParams(
            dimension_semantics=("parallel", "arbitrary"), vmem_limit_bytes=VMEM_LIMIT),
        name="in_proj",
    )(x2d, gain.reshape(1, d), w)


def _outproj_kernel(x_ref, ya_ref, yb_ref, yc_ref, yd_ref, w_ref, o_ref):
    gw = GROUP_WIDTH
    acc = x_ref[...]
    for m, y_ref in enumerate((ya_ref, yb_ref, yc_ref, yd_ref)):
        acc = acc + jnp.dot(y_ref[...].astype(BF16), w_ref[m * gw:(m + 1) * gw, :],
                            preferred_element_type=F32)
    o_ref[...] = acc


def _outproj(x2d, ys, w):
    n, d = x2d.shape
    tm = min(256, n)
    yspec = pl.BlockSpec((tm, GROUP_WIDTH), lambda i: (i, 0))
    return pl.pallas_call(
        _outproj_kernel,
        grid=(n // tm,),
        in_specs=[pl.BlockSpec((tm, d), lambda i: (i, 0)), yspec, yspec, yspec, yspec,
                  pl.BlockSpec(w.shape, lambda i: (0, 0))],
        out_specs=pl.BlockSpec((tm, d), lambda i: (i, 0)),
        out_shape=jax.ShapeDtypeStruct((n, d), F32),
        compiler_params=pltpu.CompilerParams(
            dimension_semantics=("parallel",), vmem_limit_bytes=VMEM_LIMIT),
        name="out_proj",
    )(x2d, *ys, w)


def _rwkv_kernel(p_ref, mu_ref, w0_ref, wup_ref, a0_ref, aup_ref, gup_ref, kk_ref, ka_ref, rk_ref,
                 lnw_ref, lnb_ref, tri_ref, bd_ref, o_ref, s_ref, prev_ref, osc_ref):
    nb, c, _ = p_ref.shape
    gw = GROUP_WIDTH
    n = HEAD_DIM
    nh = gw // n

    @pl.when(pl.program_id(1) == 0)
    def _():
        s_ref[...] = jnp.zeros_like(s_ref)
        prev_ref[...] = jnp.zeros_like(prev_ref)

    p = p_ref[...].reshape(nb * c, RWKV_IN)
    row = lax.broadcasted_iota(jnp.int32, p.shape, 0)
    shifted = pltpu.roll(p, 1, axis=0)
    for s in range(nb):
        shifted = jnp.where(row == s * c, prev_ref[s], shifted)
        prev_ref[s] = p[(s + 1) * c - 1:(s + 1) * c, :]
    p = p + (shifted - p) * mu_ref[...]

    r = p[:, 0:gw]
    k = p[:, gw:2 * gw]
    v = p[:, 2 * gw:3 * gw]
    o1 = 3 * gw
    w_lo = p[:, o1:o1 + RWKV_W_RANK]
    a_lo = p[:, o1 + RWKV_W_RANK:o1 + RWKV_W_RANK + RWKV_A_RANK]
    g_lo = p[:, o1 + RWKV_W_RANK + RWKV_A_RANK:]

    z = -(w0_ref[...] + _mm(jnp.tanh(w_lo), wup_ref[...]))
    softplus = jnp.maximum(z, 0.0) + jnp.log1p(jnp.exp(-jnp.abs(z)))
    lw = -jnp.exp(-softplus - 0.5)
    a = jax.nn.sigmoid(a0_ref[...] + _mm(a_lo, aup_ref[...]))
    g = _mm(jax.nn.sigmoid(g_lo), gup_ref[...])

    ones_bd = bd_ref[...]
    kk = k * kk_ref[...]
    kk = kk * lax.rsqrt(jnp.maximum(_head_sum(kk * kk, ones_bd), 1e-24))
    k = k * (1.0 + (a - 1.0) * ka_ref[...])
    b = kk * a

    cum = _const_lhs_mm(tri_ref[...], lw)
    cum_last = jnp.concatenate(
        [jnp.broadcast_to(cum[(s + 1) * c - 1:(s + 1) * c, :], (c, gw)) for s in range(nb)], axis=0)
    e_pos = jnp.exp(cum)
    e_neg = jnp.exp(-cum)
    e_end = jnp.exp(cum_last - cum)
    g_end = jnp.exp(cum_last)
    kq = kk * jnp.exp(cum - lw)
    rq = r * e_pos
    bd = b * e_neg
    kd = k * e_neg
    b_end = b * e_end
    k_end = k * e_end

    ri = lax.broadcasted_iota(jnp.int32, (c, c), 0)
    ci = lax.broadcasted_iota(jnp.int32, (c, c), 1)
    strict = ri > ci
    incl = ri >= ci
    eye = (ri == ci).astype(F32)

    heads = lambda t, rows=c: jnp.stack(
        [t[s * c:s * c + rows, h * n:(h + 1) * n] for s in range(nb) for h in range(nh)], axis=0)
    kq_h, rq_h, bd_h, kd_h, v_h = heads(kq), heads(rq), heads(bd), heads(kd), heads(v)
    s0 = s_ref[...]
    gmat = _bmm(jnp.concatenate([kq_h, rq_h], axis=1), jnp.concatenate([bd_h, kd_h], axis=1), _BNT)
    a_bb = jnp.where(strict, gmat[:, :c, :c], 0.0)
    a_bk = jnp.where(strict, gmat[:, :c, c:], 0.0)
    a_rb = jnp.where(incl, gmat[:, c:, :c], 0.0)
    a_rk = jnp.where(incl, gmat[:, c:, c:], 0.0)
    pw = -a_bb
    t_inv = eye + pw
    for _ in range(int(np.log2(c)) - 1):
        pw = _bmm(pw, pw)
        t_inv = t_inv + _bmm(t_inv, pw)
    x = _bmm(kq_h, s0, _BNT) + _bmm(a_bk, v_h)
    u = -_bmm(t_inv, x)
    y = _bmm(rq_h, s0, _BNT) + _bmm(a_rb, u) + _bmm(a_rk, v_h)
    s_ref[...] = s0 * heads(g_end, 1) + _bmm(jnp.concatenate([u, v_h], axis=1),
                                             jnp.concatenate([heads(b_end), heads(k_end)], axis=1), _BTN)
    for s in range(nb):
        for h in range(nh):
            osc_ref[s * c:(s + 1) * c, h * n:(h + 1) * n] = y[s * nh + h]

    o = osc_ref[...]
    inv_n = 1.0 / n
    mean = _head_sum(o, ones_bd) * inv_n
    dlt = o - mean
    var = _head_sum(dlt * dlt, ones_bd) * inv_n
    o = dlt * lax.rsqrt(var + RWKV_LN_EPS) * lnw_ref[...] + lnb_ref[...]
    bonus = _head_sum(r * k * rk_ref[...], ones_bd) * v
    o_ref[...] = ((o + bonus) * g).reshape(nb, c, gw)


def _rwkv(p3, mu, w0, w_up, a0, a_up, g_up, k_k, k_a, r_k, ln_w, ln_b):
    bsz, seq, _ = p3.shape
    c = RWKV_CHUNK
    gw = GROUP_WIDTH
    nb = RWKV_SEQS if bsz % RWKV_SEQS == 0 else 1
    tri = jnp.asarray(np.kron(np.eye(nb, dtype=np.float32), np.tril(np.ones((c, c), np.float32))), BF16)
    ones_bd = jnp.asarray(np.kron(np.eye(4, dtype=np.float32), np.ones((HEAD_DIM, HEAD_DIM), np.float32)), BF16)
    row = lambda t: t.reshape(1, -1)
    full = lambda t: pl.BlockSpec(t.shape, lambda i, j: (0,) * t.ndim)
    params = [row(mu), row(w0), w_up, row(a0), a_up, g_up, row(k_k), row(k_a), row(r_k), row(ln_w),
              row(ln_b), tri, ones_bd]
    return pl.pallas_call(
        _rwkv_kernel,
        grid=(bsz // nb, seq // c),
        in_specs=[pl.BlockSpec((nb, c, RWKV_IN), lambda i, j: (i, j, 0))] + [full(t) for t in params],
        out_specs=pl.BlockSpec((nb, c, gw), lambda i, j: (i, j, 0)),
        out_shape=jax.ShapeDtypeStruct((bsz, seq, gw), F32),
        scratch_shapes=[pltpu.VMEM((nb * gw // HEAD_DIM, HEAD_DIM, HEAD_DIM), F32),
                        pltpu.VMEM((nb, 1, RWKV_IN), F32),
                        pltpu.VMEM((nb * c, gw), F32)],
        compiler_params=pltpu.CompilerParams(
            dimension_semantics=("parallel", "arbitrary"), vmem_limit_bytes=VMEM_LIMIT),
        name="rwkv7",
    )(p3, *params)


def _rope_table_kernel(pos_ref, invf_ref, cos_ref, sin_ref):
    ang = pos_ref[...].astype(F32) * invf_ref[...]
    cos_ref[...] = jnp.cos(ang)
    sin_ref[...] = jnp.sin(ang)


def _rope_tables(positions):
    bsz, seq = positions.shape
    d = HEAD_DIM
    w = 2 * d
    inv_freq = ROPE_THETA ** (-jnp.arange(0, ROPE_DIM, 2, dtype=F32) / ROPE_DIM)
    lane_d = np.arange(w) % d
    invf = jnp.where(lane_d < ROPE_DIM, inv_freq[lane_d % (ROPE_DIM // 2)], 0.0).reshape(1, w)
    tr = min(1024, bsz * seq)
    return pl.pallas_call(
        _rope_table_kernel,
        grid=(bsz * seq // tr,),
        in_specs=[pl.BlockSpec((tr, 1), lambda i: (i, 0)), pl.BlockSpec((1, w), lambda i: (0, 0))],
        out_specs=[pl.BlockSpec((tr, w), lambda i: (i, 0))] * 2,
        out_shape=[jax.ShapeDtypeStruct((bsz * seq, w), F32)] * 2,
        compiler_params=pltpu.CompilerParams(dimension_semantics=("parallel",)),
        name="rope_tables",
    )(positions.reshape(bsz * seq, 1), invf)


def _attn_kernel(cos_ref, sin_ref, q0_ref, q1_ref, kv_ref, qn_ref, kn_ref, sink_ref, bd_ref,
                 o_ref, kprev_ref, vprev_ref):
    nb, blk, _ = q0_ref.shape
    rows = nb * blk
    d = HEAD_DIM
    half = ROPE_DIM // 2
    group = ATTN_HEADS // ATTN_KV_HEADS
    kvw = ATTN_KV_HEADS * d
    first = pl.program_id(1) == 0

    @pl.when(first)
    def _():
        kprev_ref[...] = jnp.zeros_like(kprev_ref)
        vprev_ref[...] = jnp.zeros_like(vprev_ref)

    ones_bd = bd_ref[...]
    cos = cos_ref[...].reshape(rows, 2 * d)
    sin = sin_ref[...].reshape(rows, 2 * d)

    def norm_rope(x, gain):
        w = x.shape[1]
        x = x * lax.rsqrt(_head_sum(x * x, ones_bd[:w, :w]) * (1.0 / d) + NORM_EPS) * gain
        ld = lax.broadcasted_iota(jnp.int32, (rows, w), 1) % d
        rot = jnp.where(ld < half, -pltpu.roll(x, w - half, axis=1),
                        jnp.where(ld < ROPE_DIM, pltpu.roll(x, half, axis=1), 0.0))
        tile = lambda t: t if w == t.shape[1] else jnp.concatenate([t] * (w // t.shape[1]), axis=1)
        return x * tile(cos) + rot * tile(sin)

    qs = [norm_rope(q_ref[...].reshape(rows, ATTN_COL_BLOCK), qn_ref[...]) * (d ** -0.5)
          for q_ref in (q0_ref, q1_ref)]
    kv = kv_ref[...].reshape(rows, ATTN_COL_BLOCK)
    k_cur = norm_rope(kv[:, :kvw], kn_ref[:, :kvw])
    v_cur = kv[:, kvw:]
    k_prev = kprev_ref[...]
    v_prev = vprev_ref[...]

    heads_per_ref = ATTN_COL_BLOCK // d
    q_head = lambda s, qh: qs[qh // heads_per_ref][s * blk:(s + 1) * blk,
                                                   (qh % heads_per_ref) * d:(qh % heads_per_ref + 1) * d]
    q_g = jnp.stack([jnp.concatenate([q_head(s, kh * group + i) for i in range(group)], axis=0)
                     for s in range(nb) for kh in range(ATTN_KV_HEADS)], axis=0)
    kv_heads = lambda t: jnp.stack([t[s * blk:(s + 1) * blk, kh * d:(kh + 1) * d]
                                    for s in range(nb) for kh in range(ATTN_KV_HEADS)], axis=0)

    ri = lax.broadcasted_iota(jnp.int32, (group * blk, blk), 0) % blk
    ci = lax.broadcasted_iota(jnp.int32, (group * blk, blk), 1)
    neg_inf = -jnp.inf
    no_prev = jnp.where(first, neg_inf, 0.0)
    s_prev = jnp.where(ci > ri, _bmm(q_g, kv_heads(k_prev), _BNT), neg_inf) + no_prev
    s_cur = jnp.where(ci <= ri, _bmm(q_g, kv_heads(k_cur), _BNT), neg_inf)
    sink = sink_ref[...]
    m = jnp.maximum(jnp.maximum(jnp.max(s_prev, axis=2, keepdims=True),
                                jnp.max(s_cur, axis=2, keepdims=True)), sink)
    p_prev = jnp.exp(s_prev - m)
    p_cur = jnp.exp(s_cur - m)
    den = (jnp.sum(p_prev, axis=2, keepdims=True) + jnp.sum(p_cur, axis=2, keepdims=True)
           + jnp.exp(sink - m))
    o = (_bmm(p_prev, kv_heads(v_prev)) + _bmm(p_cur, kv_heads(v_cur))) / den
    for s in range(nb):
        for qh in range(ATTN_HEADS):
            o_ref[s, :, qh * d:(qh + 1) * d] = o[s * ATTN_KV_HEADS + qh // group,
                                                 (qh % group) * blk:(qh % group + 1) * blk, :]

    kprev_ref[...] = k_cur
    vprev_ref[...] = v_cur


def _attn(p3, rope_cos, rope_sin, q_norm, k_norm, sinks):
    bsz, seq, _ = p3.shape
    blk = ATTN_BLOCK
    cw = ATTN_COL_BLOCK
    d = HEAD_DIM
    nb = ATTN_SEQS if bsz % ATTN_SEQS == 0 else 1
    qn = jnp.tile(q_norm.astype(F32), cw // d).reshape(1, cw)
    kn = jnp.tile(k_norm.astype(F32), cw // d).reshape(1, cw)
    ones_bd = jnp.asarray(np.kron(np.eye(cw // d, dtype=np.float32), np.ones((d, d), np.float32)), BF16)
    group = ATTN_HEADS // ATTN_KV_HEADS
    sink_col = jnp.repeat(sinks.astype(F32).reshape(ATTN_KV_HEADS, group), blk, axis=1)[:, :, None]
    sink_col = jnp.tile(sink_col, (nb, 1, 1))
    tw = rope_cos.shape[-1]
    cos3 = rope_cos.reshape(bsz, seq, tw)
    sin3 = rope_sin.reshape(bsz, seq, tw)
    full = lambda t: pl.BlockSpec(t.shape, lambda i, j: (0,) * t.ndim)
    col = lambda o: pl.BlockSpec((nb, blk, cw), lambda i, j: (i, j, ATTN_OFF_BLOCKS + o))
    trig = pl.BlockSpec((nb, blk, tw), lambda i, j: (i, j, 0))
    return pl.pallas_call(
        _attn_kernel,
        grid=(bsz // nb, seq // blk),
        in_specs=[trig, trig, col(0), col(1), col(2), full(qn), full(kn), full(sink_col), full(ones_bd)],
        out_specs=pl.BlockSpec((nb, blk, GROUP_WIDTH), lambda i, j: (i, j, 0)),
        out_shape=jax.ShapeDtypeStruct((bsz, seq, GROUP_WIDTH), F32),
        scratch_shapes=[pltpu.VMEM((nb * blk, ATTN_KV_HEADS * d), F32),
                        pltpu.VMEM((nb * blk, ATTN_KV_HEADS * d), F32)],
        compiler_params=pltpu.CompilerParams(
            dimension_semantics=("parallel", "arbitrary"), vmem_limit_bytes=VMEM_LIMIT),
        name="swa_attn",
    )(cos3, sin3, p3, p3, p3, qn, kn, sink_col, ones_bd)


def _s5_prep_kernel(lre_ref, lim_ref, ls_ref, bre_ref, bim_ref, are_ref, aim_ref, obre_ref, obim_ref):
    lre = lre_ref[...]
    lim = lim_ref[...]
    dt = jnp.exp(ls_ref[...])
    mag = jnp.exp(lre * dt)
    are = mag * jnp.cos(lim * dt)
    aim = mag * jnp.sin(lim * dt)
    are_ref[...] = are
    aim_ref[...] = aim
    inv = 1.0 / (lre * lre + lim * lim)
    cre = ((are - 1.0) * lre + aim * lim) * inv
    cim = (aim * lre - (are - 1.0) * lim) * inv
    bre = bre_ref[...]
    bim = bim_ref[...]
    obre_ref[...] = cre[:, None, :] * bre - cim[:, None, :] * bim
    obim_ref[...] = cre[:, None, :] * bim + cim[:, None, :] * bre


def _s5_kernel(u_ref, are_ref, aim_ref, b_ref, c_ref, d_ref, gw_ref, gb_ref, o_ref, x_ref, s_ref):
    bsz, tc, ch = u_ref.shape
    hw = S5_WIDTH // 2
    hc = ch // 2

    @pl.when(pl.program_id(0) == 0)
    def _():
        s_ref[...] = jnp.zeros_like(s_ref)

    u = jnp.swapaxes(u_ref[...], 0, 1).reshape(tc * bsz, ch)
    for hf in range(2):
        x_ref[:, 2 * hw * hf:2 * hw * (hf + 1)] = _mm(u[:, hc * hf:hc * (hf + 1)], b_ref[hf])

    for hf in range(2):
        for s0 in range(0, hw, S5_STRIP):
            re = slice(2 * hw * hf + s0, 2 * hw * hf + s0 + S5_STRIP)
            im = slice(2 * hw * hf + hw + s0, 2 * hw * hf + hw + s0 + S5_STRIP)
            lam = slice(hw * hf + s0, hw * hf + s0 + S5_STRIP)
            ar = jnp.broadcast_to(are_ref[:, lam], (bsz, S5_STRIP))
            ai = jnp.broadcast_to(aim_ref[:, lam], (bsz, S5_STRIP))

            def step(t, carry, re=re, im=im, ar=ar, ai=ai):
                sr, si = carry
                rows = pl.ds(pl.multiple_of(t * bsz, bsz), bsz)
                nr = ar * sr - ai * si + x_ref[rows, re]
                ni = ar * si + ai * sr + x_ref[rows, im]
                x_ref[rows, re] = nr
                x_ref[rows, im] = ni
                return nr, ni

            sr, si = lax.fori_loop(0, tc, step, (s_ref[:, re], s_ref[:, im]), unroll=4)
            s_ref[:, re] = sr
            s_ref[:, im] = si

    y = jnp.concatenate([_mm(x_ref[:, 2 * hw * hf:2 * hw * (hf + 1)], c_ref[hf]) for hf in range(2)], axis=1)
    y = y + d_ref[...] * u
    z = 0.5 * y * (1.0 + lax.erf(y * (2.0 ** -0.5)))
    out = z * jax.nn.sigmoid(_mm(z, gw_ref[...]) + gb_ref[...])
    o_ref[...] = jnp.swapaxes(out.reshape(tc, bsz, ch), 0, 1)


def _s5(p3, lam_re, lam_im, log_step, b_re, b_im, c_re, c_im, d_skip, glu_w, glu_b):
    bsz, seq, _ = p3.shape
    g, st, ch = S5_GROUPS, S5_STATE, S5_GROUP
    gwd = GROUP_WIDTH
    vm = pl.BlockSpec(memory_space=pltpu.VMEM)
    a_re, a_im, bb_re, bb_im = pl.pallas_call(
        _s5_prep_kernel,
        in_specs=[vm] * 5,
        out_specs=[vm] * 4,
        out_shape=[jax.ShapeDtypeStruct((g, st), F32)] * 2 + [jax.ShapeDtypeStruct((g, ch, st), F32)] * 2,
        name="s5_prep",
    )(lam_re, lam_im, log_step.reshape(g, 1), jnp.swapaxes(b_re, 1, 2), jnp.swapaxes(b_im, 1, 2))

    gh = g // 2
    eye = jnp.eye(gh, dtype=F32)
    bd_in = lambda t: (t[:, :, None, :] * eye[:, None, :, None]).reshape(gh * ch, gh * st)
    bd_out = lambda t: (jnp.swapaxes(t, 1, 2)[:, :, None, :] * eye[:, None, :, None]).reshape(gh * st, gh * ch)
    halves = lambda t: (t[:gh], t[gh:])
    b_mat = jnp.stack([jnp.concatenate([bd_in(r), bd_in(i)], axis=1)
                       for r, i in zip(halves(bb_re), halves(bb_im))]).astype(BF16)
    c_mat = jnp.stack([jnp.concatenate([bd_out(r), -bd_out(i)], axis=0)
                       for r, i in zip(halves(c_re.astype(F32)), halves(c_im.astype(F32)))]).astype(BF16)

    tc = min(64, seq)
    full = lambda t: pl.BlockSpec(t.shape, lambda i: (0,) * t.ndim)
    params = [a_re.reshape(1, S5_WIDTH), a_im.reshape(1, S5_WIDTH), b_mat, c_mat,
              d_skip.reshape(1, gwd), glu_w.astype(BF16), glu_b.reshape(1, gwd)]
    return pl.pallas_call(
        _s5_kernel,
        grid=(seq // tc,),
        in_specs=[pl.BlockSpec((bsz, tc, gwd), lambda i: (0, i, S5_OFF_BLOCKS))] + [full(t) for t in params],
        out_specs=pl.BlockSpec((bsz, tc, gwd), lambda i: (0, i, 0)),
        out_shape=jax.ShapeDtypeStruct((bsz, seq, gwd), F32),
        scratch_shapes=[pltpu.VMEM((tc * bsz, 2 * S5_WIDTH), F32), pltpu.VMEM((bsz, 2 * S5_WIDTH), F32)],
        compiler_params=pltpu.CompilerParams(
            dimension_semantics=("arbitrary",), vmem_limit_bytes=VMEM_LIMIT),
        name="s5_scan",
    )(p3, *params)


def _hgrn_kernel(qf_ref, ig_ref, lbraw_ref, gn_ref, tri_ref, bd_ref, o_ref, s_ref, osc_ref, *, layer):
    nb, tb, _ = qf_ref.shape
    cz = HGRN_CHUNK
    gw = GROUP_WIDTH
    n = HEAD_DIM
    nh = gw // n

    @pl.when(pl.program_id(1) == 0)
    def _():
        s_ref[...] = jnp.zeros_like(s_ref)

    qf = qf_ref[...].reshape(nb * tb, 2 * gw)
    ig = ig_ref[...].reshape(nb * tb, 2 * gw)
    q, f = qf[:, :gw], qf[:, gw:]
    v, g = ig[:, :gw], ig[:, gw:]

    lbr = lbraw_ref[...]
    e = jnp.exp(lbr - jnp.max(lbr, axis=0, keepdims=True))
    sm = e / jnp.sum(e, axis=0, keepdims=True)
    lb = jnp.zeros((1, gw), F32)
    for i in range(1, layer + 1):
        lb = lb + sm[i:i + 1, :]

    q = _silu(q) * (n ** -0.5)
    f_gate = lb + (1.0 - lb) * jax.nn.sigmoid(f)
    log_f = jnp.log(f_gate)
    k = 1.0 - f_gate

    parts = _split3(log_f)

    def per_seq(c_ref):
        cm = c_ref[...]
        seq_rows = lambda t, s: t[s * tb:(s + 1) * tb, :]
        return jnp.concatenate(
            [_dot(cm, seq_rows(parts[0], s), _NN)
             + (_dot(cm, seq_rows(parts[1], s), _NN) + _dot(cm, seq_rows(parts[2], s), _NN))
             for s in range(nb)], axis=0)

    bcum = per_seq(tri_ref)
    b3 = bcum.reshape(nb * tb // cz, cz, gw)
    chunk_row = lambda r: jnp.broadcast_to(b3[:, r:r + 1, :], b3.shape).reshape(nb * tb, gw)
    bmid = chunk_row(cz // 2 - 1)
    blast = chunk_row(cz - 1)
    qe = q * jnp.exp(bcum - bmid)
    ke = k * jnp.exp(bmid - bcum)
    kl = k * jnp.exp(blast - bcum)
    qb = q * jnp.exp(bcum)
    dec = jnp.exp(blast)

    ri = lax.broadcasted_iota(jnp.int32, (tb, tb), 0)
    ci = lax.broadcasted_iota(jnp.int32, (tb, tb), 1)
    mask = jnp.logical_and(ri // cz == ci // cz, ri >= ci)

    heads = lambda t: jnp.stack(
        [t[s * tb:(s + 1) * tb, h * n:(h + 1) * n] for s in range(nb) for h in range(nh)], axis=0)
    v_h, qb_h, kl_h, dec_h = heads(v), heads(qb), heads(kl), heads(dec)
    att = jnp.where(mask, _bmm(heads(qe), heads(ke), _BNT), 0.0)
    o_intra = _bmm(att, v_h)
    nchunk = tb // cz
    ti = lax.broadcasted_iota(jnp.int32, (tb, nchunk * n), 0)
    li = lax.broadcasted_iota(jnp.int32, (tb, nchunk * n), 1)
    kl_spread = jnp.where(ti // cz == li // n, jnp.concatenate([kl_h] * nchunk, axis=2), 0.0)
    kv_all = _bmm(v_h, kl_spread, _BTN)
    st = s_ref[...]
    o_inter = []
    for j in range(nchunk):
        rows = slice(j * cz, (j + 1) * cz)
        o_inter.append(_bmm(qb_h[:, rows, :], st, _BNT))
        st = st * dec_h[:, j * cz:j * cz + 1, :] + kv_all[:, :, j * n:(j + 1) * n]
    s_ref[...] = st
    o_heads = o_intra + jnp.concatenate(o_inter, axis=1)
    for s in range(nb):
        for h in range(nh):
            osc_ref[s * tb:(s + 1) * tb, h * n:(h + 1) * n] = o_heads[s * nh + h]

    o = osc_ref[...]
    ms = _head_sum(o * o, bd_ref[...]) * (1.0 / n)
    o_ref[...] = (o * lax.rsqrt(ms + NORM_EPS) * gn_ref[...] * _silu(g)).reshape(nb, tb, gw)


def _hgrn(p3, lower_bounds, g_norm, layer):
    bsz, seq, _ = p3.shape
    tb = min(HGRN_TILE, seq)
    cz = HGRN_CHUNK
    gw = GROUP_WIDTH
    idx = np.arange(tb)
    same = (idx[:, None] // cz) == (idx[None, :] // cz)
    tri = same & (idx[:, None] >= idx[None, :])
    consts = [jnp.asarray(tri.astype(np.float32), BF16)]
    ones_bd = jnp.asarray(np.kron(np.eye(4, dtype=np.float32), np.ones((HEAD_DIM, HEAD_DIM), np.float32)), BF16)
    gn = jnp.tile(g_norm.astype(F32), gw // HEAD_DIM).reshape(1, gw)
    params = [lower_bounds.astype(F32), gn] + consts + [ones_bd]
    full = lambda t: pl.BlockSpec(t.shape, lambda i, j: (0,) * t.ndim)
    nb = HGRN_SEQS if bsz % HGRN_SEQS == 0 else 1
    col = lambda o: pl.BlockSpec((nb, tb, 2 * gw), lambda i, j: (i, j, HGRN_OFF_BLOCKS + o))
    return pl.pallas_call(
        functools.partial(_hgrn_kernel, layer=layer),
        grid=(bsz // nb, seq // tb),
        in_specs=[col(0), col(1)] + [full(t) for t in params],
        out_specs=pl.BlockSpec((nb, tb, gw), lambda i, j: (i, j, 0)),
        out_shape=jax.ShapeDtypeStruct((bsz, seq, gw), F32),
        scratch_shapes=[pltpu.VMEM((nb * gw // HEAD_DIM, HEAD_DIM, HEAD_DIM), F32),
                        pltpu.VMEM((nb * tb, gw), F32)],
        compiler_params=pltpu.CompilerParams(
            dimension_semantics=("parallel", "arbitrary"), vmem_limit_bytes=VMEM_LIMIT),
        name="hgrn2",
    )(p3, p3, *params)


def kernel(x, positions, ffn1_norm, ffn1_w_gate, ffn1_w_up, ffn1_w_down, mix_norm, w_in, rwkv_mu, rwkv_w0, rwkv_w_up, rwkv_a0, rwkv_a_up, rwkv_g_up, rwkv_k_k, rwkv_k_a, rwkv_r_k, rwkv_ln_w, rwkv_ln_b, attn_q_norm, attn_k_norm, attn_sinks, s5_lambda_re, s5_lambda_im, s5_log_step, s5_b_re, s5_b_im, s5_c_re, s5_c_im, s5_d, s5_glu_w, s5_glu_b, hgrn_lower_bounds, hgrn_g_norm, w_out, ffn2_norm, ffn2_w_gate, ffn2_w_up, ffn2_w_down):
    bsz, seq, d = x.shape
    depth = w_in.shape[0]
    n = bsz * seq
    xf = x.reshape(n, d)
    rope_cos, rope_sin = _rope_tables(positions)
    for l in range(depth):
        xf = _ffn(xf, ffn1_norm[l], ffn1_w_gate[l].astype(BF16), ffn1_w_up[l].astype(BF16),
                  ffn1_w_down[l].astype(BF16))
        p3 = _proj(xf, mix_norm[l], w_in[l].astype(BF16)).reshape(bsz, seq, D_IN)
        y_a = _rwkv(p3, rwkv_mu[l], rwkv_w0[l], rwkv_w_up[l], rwkv_a0[l], rwkv_a_up[l], rwkv_g_up[l],
                    rwkv_k_k[l], rwkv_k_a[l], rwkv_r_k[l].reshape(-1), rwkv_ln_w[l], rwkv_ln_b[l])
        y_b = _attn(p3, rope_cos, rope_sin, attn_q_norm[l], attn_k_norm[l], attn_sinks[l])
        y_c = _s5(p3, s5_lambda_re[l], s5_lambda_im[l], s5_log_step[l], s5_b_re[l], s5_b_im[l],
                  s5_c_re[l], s5_c_im[l], s5_d[l], s5_glu_w[l], s5_glu_b[l])
        y_d = _hgrn(p3, hgrn_lower_bounds, hgrn_g_norm[l], l)
        ys = [t.reshape(n, GROUP_WIDTH) for t in (y_a, y_b, y_c, y_d)]
        xf = _outproj(xf, ys, w_out[l].astype(BF16))
        xf = _ffn(xf, ffn2_norm[l], ffn2_w_gate[l].astype(BF16), ffn2_w_up[l].astype(BF16),
                  ffn2_w_down[l].astype(BF16))
    return xf.reshape(bsz, seq, d)
```

```python
import functools

import jax
import jax.numpy as jnp
import numpy as np
from jax import lax
from jax.experimental import pallas as pl
from jax.experimental.pallas import tpu as pltpu

F32 = jnp.float32
BF16 = jnp.bfloat16

HEAD_DIM = 64
GROUP_WIDTH = 512
NORM_EPS = 1e-6
FFN_RES_WEIGHT = 0.5

RWKV_W_RANK = 64
RWKV_A_RANK = 64
RWKV_G_RANK = 128
RWKV_LN_EPS = 64e-5
RWKV_IN = 3 * GROUP_WIDTH + RWKV_W_RANK + RWKV_A_RANK + RWKV_G_RANK
RWKV_CHUNK = 64
RWKV_SEQS = 4

ATTN_HEADS = 8
ATTN_KV_HEADS = 2
ATTN_BLOCK = 128
ATTN_SEQS = 2
ROPE_THETA = 500000.0
ROPE_DIM = HEAD_DIM // 4
ATTN_COL_BLOCK = 256
ATTN_OFF_BLOCKS = RWKV_IN // ATTN_COL_BLOCK

S5_GROUP = 16
S5_GROUPS = GROUP_WIDTH // S5_GROUP
S5_STATE = 64
S5_WIDTH = S5_GROUPS * S5_STATE
S5_OFF_BLOCKS = (RWKV_IN + 768) // GROUP_WIDTH
S5_STRIP = 512

HGRN_CHUNK = 16
HGRN_TILE = 128
HGRN_SEQS = 4
HGRN_OFF_BLOCKS = 3

D_IN = 5120
VMEM_LIMIT = 56 * 1024 * 1024


def _dot(a, b, dims):
    return lax.dot_general(a, b, (dims, ((), ())), preferred_element_type=F32)


_NN = ((1,), (0,))


def _mm(a, b):
    return _dot(a.astype(BF16), b.astype(BF16), _NN)


_BNN = (((2,), (1,)), ((0,), (0,)))
_BNT = (((2,), (2,)), ((0,), (0,)))
_BTN = (((1,), (1,)), ((0,), (0,)))


def _bmm(a, b, dims=_BNN):
    return lax.dot_general(a.astype(BF16), b.astype(BF16), dims, preferred_element_type=F32)


def _split3(x):
    hi = x.astype(BF16)
    r = x - hi.astype(F32)
    mid = r.astype(BF16)
    lo = (r - mid.astype(F32)).astype(BF16)
    return hi, mid, lo


def _const_lhs_mm(c, x):
    hi, mid, lo = _split3(x)
    return _dot(c, hi, _NN) + (_dot(c, mid, _NN) + _dot(c, lo, _NN))


def _const_rhs_mm(x, c):
    hi, mid, lo = _split3(x)
    return _dot(hi, c, _NN) + (_dot(mid, c, _NN) + _dot(lo, c, _NN))


def _head_sum(x, ones_bd):
    w = ones_bd.shape[0]
    parts = [_const_rhs_mm(x[:, i:i + w], ones_bd) for i in range(0, x.shape[1], w)]
    return parts[0] if len(parts) == 1 else jnp.concatenate(parts, axis=1)


def _silu(x):
    return x * jax.nn.sigmoid(x)


def _rms_rows(x, gain):
    ms = jnp.mean(x * x, axis=-1, keepdims=True)
    return x * lax.rsqrt(ms + NORM_EPS) * gain


def _ffn_kernel(x_ref, g_ref, wg_ref, wu_ref, wd_ref, o_ref, h_ref):
    @pl.when(pl.program_id(1) == 0)
    def _():
        x = x_ref[...]
        h_ref[...] = _rms_rows(x, g_ref[...]).astype(BF16)
        o_ref[...] = x

    h = h_ref[...]
    gate = jnp.dot(h, wg_ref[...], preferred_element_type=F32)
    up = jnp.dot(h, wu_ref[...], preferred_element_type=F32)
    act = (_silu(gate) * up).astype(BF16)
    o_ref[...] += FFN_RES_WEIGHT * jnp.dot(act, wd_ref[...], preferred_element_type=F32)


def _ffn(x2d, gain, wg, wu, wd):
    n, d = x2d.shape
    f = wg.shape[1]
    tm = min(1024, n)
    tf = 512 if f % 512 == 0 else f
    return pl.pallas_call(
        _ffn_kernel,
        grid=(n // tm, f // tf),
        in_specs=[
            pl.BlockSpec((tm, d), lambda i, j: (i, 0)),
            pl.BlockSpec((1, d), lambda i, j: (0, 0)),
            pl.BlockSpec((d, tf), lambda i, j: (0, j)),
            pl.BlockSpec((d, tf), lambda i, j: (0, j)),
            pl.BlockSpec((tf, d), lambda i, j: (j, 0)),
        ],
        out_specs=pl.BlockSpec((tm, d), lambda i, j: (i, 0)),
        out_shape=jax.ShapeDtypeStruct((n, d), F32),
        scratch_shapes=[pltpu.VMEM((tm, d), BF16)],
        compiler_params=pltpu.CompilerParams(
            dimension_semantics=("parallel", "arbitrary"), vmem_limit_bytes=VMEM_LIMIT),
        name="ffn",
    )(x2d, gain.reshape(1, d), wg, wu, wd)


def _proj_kernel(x_ref, g_ref, w_ref, o_ref, h_ref):
    @pl.when(pl.program_id(1) == 0)
    def _():
        h_ref[...] = _rms_rows(x_ref[...], g_ref[...]).astype(BF16)

    o_ref[...] = jnp.dot(h_ref[...], w_ref[...], preferred_element_type=F32).astype(o_ref.dtype)


def _proj(x2d, gain, w):
    n, d = x2d.shape
    dout = w.shape[1]
    tm = min(1024, n)
    tn = dout // 4
    return pl.pallas_call(
        _proj_kernel,
        grid=(n // tm, dout // tn),
        in_specs=[
            pl.BlockSpec((tm, d), lambda i, j: (i, 0)),
            pl.BlockSpec((1, d), lambda i, j: (0, 0)),
            pl.BlockSpec((d, tn), lambda i, j: (0, j)),
        ],
        out_specs=pl.BlockSpec((tm, tn), lambda i, j: (i, j)),
        out_shape=jax.ShapeDtypeStruct((n, dout), BF16),
        scratch_shapes=[pltpu.VMEM((tm, d), BF16)],
        compiler_params=pltpu.CompilerParams(
            dimension_semantics=("parallel", "arbitrary"), vmem_limit_bytes=VMEM_LIMIT),
        name="in_proj",
    )(x2d, gain.reshape(1, d), w)


def _outproj_kernel(x_ref, ya_ref, yb_ref, yc_ref, yd_ref, w_ref, o_ref):
    gw = GROUP_WIDTH
    acc = x_ref[...]
    for m, y_ref in enumerate((ya_ref, yb_ref, yc_ref, yd_ref)):
        acc = acc + jnp.dot(y_ref[...].astype(BF16), w_ref[m * gw:(m + 1) * gw, :],
                            preferred_element_type=F32)
    o_ref[...] = acc


def _outproj(x2d, ys, w):
    n, d = x2d.shape
    tm = min(512, n)
    yspec = pl.BlockSpec((tm, GROUP_WIDTH), lambda i: (i, 0))
    return pl.pallas_call(
        _outproj_kernel,
        grid=(n // tm,),
        in_specs=[pl.BlockSpec((tm, d), lambda i: (i, 0)), yspec, yspec, yspec, yspec,
                  pl.BlockSpec(w.shape, lambda i: (0, 0))],
        out_specs=pl.BlockSpec((tm, d), lambda i: (i, 0)),
        out_shape=jax.ShapeDtypeStruct((n, d), F32),
        compiler_params=pltpu.CompilerParams(
            dimension_semantics=("parallel",), vmem_limit_bytes=VMEM_LIMIT),
        name="out_proj",
    )(x2d, *ys, w)


def _rwkv_kernel(p_ref, mu_ref, w0_ref, wup_ref, a0_ref, aup_ref, gup_ref, kk_ref, ka_ref, rk_ref,
                 lnw_ref, lnb_ref, tri_ref, bd_ref, o_ref, s_ref, prev_ref, osc_ref):
    nb, c, _ = p_ref.shape
    gw = GROUP_WIDTH
    n = HEAD_DIM
    nh = gw // n

    @pl.when(pl.program_id(1) == 0)
    def _():
        s_ref[...] = jnp.zeros_like(s_ref)
        prev_ref[...] = jnp.zeros_like(prev_ref)

    p = p_ref[...].astype(F32).reshape(nb * c, RWKV_IN)
    row = lax.broadcasted_iota(jnp.int32, p.shape, 0)
    shifted = pltpu.roll(p, 1, axis=0)
    for s in range(nb):
        shifted = jnp.where(row == s * c, prev_ref[s], shifted)
        prev_ref[s] = p[(s + 1) * c - 1:(s + 1) * c, :]
    p = p + (shifted - p) * mu_ref[...]

    r = p[:, 0:gw]
    k = p[:, gw:2 * gw]
    v = p[:, 2 * gw:3 * gw]
    o1 = 3 * gw
    w_lo = p[:, o1:o1 + RWKV_W_RANK]
    a_lo = p[:, o1 + RWKV_W_RANK:o1 + RWKV_W_RANK + RWKV_A_RANK]
    g_lo = p[:, o1 + RWKV_W_RANK + RWKV_A_RANK:]

    z = -(w0_ref[...] + _mm(jnp.tanh(w_lo), wup_ref[...]))
    softplus = jnp.maximum(z, 0.0) + jnp.log1p(jnp.exp(-jnp.abs(z)))
    lw = -jnp.exp(-softplus - 0.5)
    a = jax.nn.sigmoid(a0_ref[...] + _mm(a_lo, aup_ref[...]))
    g = _mm(jax.nn.sigmoid(g_lo), gup_ref[...])

    ones_bd = bd_ref[...]
    kk = k * kk_ref[...]
    kk = kk * lax.rsqrt(jnp.maximum(_head_sum(kk * kk, ones_bd), 1e-24))
    k = k * (1.0 + (a - 1.0) * ka_ref[...])
    b = kk * a

    cum = _const_lhs_mm(tri_ref[...], lw)
    cum_last = jnp.concatenate(
        [jnp.broadcast_to(cum[(s + 1) * c - 1:(s + 1) * c, :], (c, gw)) for s in range(nb)], axis=0)
    e_pos = jnp.exp(cum)
    e_neg = jnp.exp(-cum)
    e_end = jnp.exp(cum_last - cum)
    g_end = jnp.exp(cum_last)
    kq = kk * jnp.exp(cum - lw)
    rq = r * e_pos
    bd = b * e_neg
    kd = k * e_neg
    b_end = b * e_end
    k_end = k * e_end

    ri = lax.broadcasted_iota(jnp.int32, (c, c), 0)
    ci = lax.broadcasted_iota(jnp.int32, (c, c), 1)
    strict = ri > ci
    incl = ri >= ci
    eye = (ri == ci).astype(F32)

    heads = lambda t, rows=c: jnp.stack(
        [t[s * c:s * c + rows, h * n:(h + 1) * n] for s in range(nb) for h in range(nh)], axis=0)
    kq_h, rq_h, bd_h, kd_h, v_h = heads(kq), heads(rq), heads(bd), heads(kd), heads(v)
    s0 = s_ref[...]
    gmat = _bmm(jnp.concatenate([kq_h, rq_h], axis=1), jnp.concatenate([bd_h, kd_h], axis=1), _BNT)
    a_bb = jnp.where(strict, gmat[:, :c, :c], 0.0)
    a_bk = jnp.where(strict, gmat[:, :c, c:], 0.0)
    a_rb = jnp.where(incl, gmat[:, c:, :c], 0.0)
    a_rk = jnp.where(incl, gmat[:, c:, c:], 0.0)
    pw = -a_bb
    t_inv = eye + pw
    for _ in range(int(np.log2(c)) - 1):
        pw = _bmm(pw, pw)
        t_inv = t_inv + _bmm(t_inv, pw)
    x = _bmm(kq_h, s0, _BNT) + _bmm(a_bk, v_h)
    u = -_bmm(t_inv, x)
    y = _bmm(rq_h, s0, _BNT) + _bmm(a_rb, u) + _bmm(a_rk, v_h)
    s_ref[...] = s0 * heads(g_end, 1) + _bmm(jnp.concatenate([u, v_h], axis=1),
                                             jnp.concatenate([heads(b_end), heads(k_end)], axis=1), _BTN)
    for s in range(nb):
        for h in range(nh):
            osc_ref[s * c:(s + 1) * c, h * n:(h + 1) * n] = y[s * nh + h]

    o = osc_ref[...]
    inv_n = 1.0 / n
    mean = _head_sum(o, ones_bd) * inv_n
    dlt = o - mean
    var = _head_sum(dlt * dlt, ones_bd) * inv_n
    o = dlt * lax.rsqrt(var + RWKV_LN_EPS) * lnw_ref[...] + lnb_ref[...]
    bonus = _head_sum(r * k * rk_ref[...], ones_bd) * v
    o_ref[...] = ((o + bonus) * g).reshape(nb, c, gw).astype(o_ref.dtype)


def _rwkv(p3, mu, w0, w_up, a0, a_up, g_up, k_k, k_a, r_k, ln_w, ln_b):
    bsz, seq, _ = p3.shape
    c = RWKV_CHUNK
    gw = GROUP_WIDTH
    nb = RWKV_SEQS if bsz % RWKV_SEQS == 0 else 1
    tri = jnp.asarray(np.kron(np.eye(nb, dtype=np.float32), np.tril(np.ones((c, c), np.float32))), BF16)
    ones_bd = jnp.asarray(np.kron(np.eye(4, dtype=np.float32), np.ones((HEAD_DIM, HEAD_DIM), np.float32)), BF16)
    row = lambda t: t.reshape(1, -1)
    full = lambda t: pl.BlockSpec(t.shape, lambda i, j: (0,) * t.ndim)
    params = [row(mu), row(w0), w_up, row(a0), a_up, g_up, row(k_k), row(k_a), row(r_k), row(ln_w),
              row(ln_b), tri, ones_bd]
    return pl.pallas_call(
        _rwkv_kernel,
        grid=(bsz // nb, seq // c),
        in_specs=[pl.BlockSpec((nb, c, RWKV_IN), lambda i, j: (i, j, 0))] + [full(t) for t in params],
        out_specs=pl.BlockSpec((nb, c, gw), lambda i, j: (i, j, 0)),
        out_shape=jax.ShapeDtypeStruct((bsz, seq, gw), BF16),
        scratch_shapes=[pltpu.VMEM((nb * gw // HEAD_DIM, HEAD_DIM, HEAD_DIM), F32),
                        pltpu.VMEM((nb, 1, RWKV_IN), F32),
                        pltpu.VMEM((nb * c, gw), F32)],
        compiler_params=pltpu.CompilerParams(
            dimension_semantics=("parallel", "arbitrary"), vmem_limit_bytes=VMEM_LIMIT),
        name="rwkv7",
    )(p3, *params)


def _rope_table_kernel(pos_ref, invf_ref, cos_ref, sin_ref):
    ang = pos_ref[...].astype(F32) * invf_ref[...]
    cos_ref[...] = jnp.cos(ang)
    sin_ref[...] = jnp.sin(ang)


def _rope_tables(positions):
    bsz, seq = positions.shape
    d = HEAD_DIM
    w = 2 * d
    inv_freq = ROPE_THETA ** (-jnp.arange(0, ROPE_DIM, 2, dtype=F32) / ROPE_DIM)
    lane_d = np.arange(w) % d
    invf = jnp.where(lane_d < ROPE_DIM, inv_freq[lane_d % (ROPE_DIM // 2)], 0.0).reshape(1, w)
    tr = min(1024, bsz * seq)
    return pl.pallas_call(
        _rope_table_kernel,
        grid=(bsz * seq // tr,),
        in_specs=[pl.BlockSpec((tr, 1), lambda i: (i, 0)), pl.BlockSpec((1, w), lambda i: (0, 0))],
        out_specs=[pl.BlockSpec((tr, w), lambda i: (i, 0))] * 2,
        out_shape=[jax.ShapeDtypeStruct((bsz * seq, w), F32)] * 2,
        compiler_params=pltpu.CompilerParams(dimension_semantics=("parallel",)),
        name="rope_tables",
    )(positions.reshape(bsz * seq, 1), invf)


def _attn_kernel(cos_ref, sin_ref, q0_ref, q1_ref, kv_ref, qn_ref, kn_ref, sink_ref, bd_ref,
                 o_ref, kprev_ref, vprev_ref):
    nb, blk, _ = q0_ref.shape
    rows = nb * blk
    d = HEAD_DIM
    half = ROPE_DIM // 2
    group = ATTN_HEADS // ATTN_KV_HEADS
    kvw = ATTN_KV_HEADS * d
    first = pl.program_id(1) == 0

    @pl.when(first)
    def _():
        kprev_ref[...] = jnp.zeros_like(kprev_ref)
        vprev_ref[...] = jnp.zeros_like(vprev_ref)

    ones_bd = bd_ref[...]
    cos = cos_ref[...].reshape(rows, 2 * d)
    sin = sin_ref[...].reshape(rows, 2 * d)

    def norm_rope(x, gain):
        w = x.shape[1]
        x = x * lax.rsqrt(_head_sum(x * x, ones_bd[:w, :w]) * (1.0 / d) + NORM_EPS) * gain
        ld = lax.broadcasted_iota(jnp.int32, (rows, w), 1) % d
        rot = jnp.where(ld < half, -pltpu.roll(x, w - half, axis=1),
                        jnp.where(ld < ROPE_DIM, pltpu.roll(x, half, axis=1), 0.0))
        tile = lambda t: t if w == t.shape[1] else jnp.concatenate([t] * (w // t.shape[1]), axis=1)
        return x * tile(cos) + rot * tile(sin)

    qs = [norm_rope(q_ref[...].astype(F32).reshape(rows, ATTN_COL_BLOCK), qn_ref[...]) * (d ** -0.5)
          for q_ref in (q0_ref, q1_ref)]
    kv = kv_ref[...].astype(F32).reshape(rows, ATTN_COL_BLOCK)
    k_cur = norm_rope(kv[:, :kvw], kn_ref[:, :kvw])
    v_cur = kv[:, kvw:]
    k_prev = kprev_ref[...]
    v_prev = vprev_ref[...]

    heads_per_ref = ATTN_COL_BLOCK // d
    q_head = lambda s, qh: qs[qh // heads_per_ref][s * blk:(s + 1) * blk,
                                                   (qh % heads_per_ref) * d:(qh % heads_per_ref + 1) * d]
    q_g = jnp.stack([jnp.concatenate([q_head(s, kh * group + i) for i in range(group)], axis=0)
                     for s in range(nb) for kh in range(ATTN_KV_HEADS)], axis=0)
    kv_heads = lambda t: jnp.stack([t[s * blk:(s + 1) * blk, kh * d:(kh + 1) * d]
                                    for s in range(nb) for kh in range(ATTN_KV_HEADS)], axis=0)

    ri = lax.broadcasted_iota(jnp.int32, (group * blk, blk), 0) % blk
    ci = lax.broadcasted_iota(jnp.int32, (group * blk, blk), 1)
    neg_inf = -jnp.inf
    no_prev = jnp.where(first, neg_inf, 0.0)
    s_prev = jnp.where(ci > ri, _bmm(q_g, kv_heads(k_prev), _BNT), neg_inf) + no_prev
    s_cur = jnp.where(ci <= ri, _bmm(q_g, kv_heads(k_cur), _BNT), neg_inf)
    sink = sink_ref[...]
    m = jnp.maximum(jnp.maximum(jnp.max(s_prev, axis=2, keepdims=True),
                                jnp.max(s_cur, axis=2, keepdims=True)), sink)
    p_prev = jnp.exp(s_prev - m)
    p_cur = jnp.exp(s_cur - m)
    den = (jnp.sum(p_prev, axis=2, keepdims=True) + jnp.sum(p_cur, axis=2, keepdims=True)
           + jnp.exp(sink - m))
    o = ((_bmm(p_prev, kv_heads(v_prev)) + _bmm(p_cur, kv_heads(v_cur))) / den).astype(o_ref.dtype)
    for s in range(nb):
        for qh in range(ATTN_HEADS):
            o_ref[s, :, qh * d:(qh + 1) * d] = o[s * ATTN_KV_HEADS + qh // group,
                                                 (qh % group) * blk:(qh % group + 1) * blk, :]

    kprev_ref[...] = k_cur
    vprev_ref[...] = v_cur


def _attn(p3, rope_cos, rope_sin, q_norm, k_norm, sinks):
    bsz, seq, _ = p3.shape
    blk = ATTN_BLOCK
    cw = ATTN_COL_BLOCK
    d = HEAD_DIM
    nb = ATTN_SEQS if bsz % ATTN_SEQS == 0 else 1
    qn = jnp.tile(q_norm.astype(F32), cw // d).reshape(1, cw)
    kn = jnp.tile(k_norm.astype(F32), cw // d).reshape(1, cw)
    ones_bd = jnp.asarray(np.kron(np.eye(cw // d, dtype=np.float32), np.ones((d, d), np.float32)), BF16)
    group = ATTN_HEADS // ATTN_KV_HEADS
    sink_col = jnp.repeat(sinks.astype(F32).reshape(ATTN_KV_HEADS, group), blk, axis=1)[:, :, None]
    sink_col = jnp.tile(sink_col, (nb, 1, 1))
    tw = rope_cos.shape[-1]
    cos3 = rope_cos.reshape(bsz, seq, tw)
    sin3 = rope_sin.reshape(bsz, seq, tw)
    full = lambda t: pl.BlockSpec(t.shape, lambda i, j: (0,) * t.ndim)
    col = lambda o: pl.BlockSpec((nb, blk, cw), lambda i, j: (i, j, ATTN_OFF_BLOCKS + o))
    trig = pl.BlockSpec((nb, blk, tw), lambda i, j: (i, j, 0))
    return pl.pallas_call(
        _attn_kernel,
        grid=(bsz // nb, seq // blk),
        in_specs=[trig, trig, col(0), col(1), col(2), full(qn), full(kn), full(sink_col), full(ones_bd)],
        out_specs=pl.BlockSpec((nb, blk, GROUP_WIDTH), lambda i, j: (i, j, 0)),
        out_shape=jax.ShapeDtypeStruct((bsz, seq, GROUP_WIDTH), BF16),
        scratch_shapes=[pltpu.VMEM((nb * blk, ATTN_KV_HEADS * d), F32),
                        pltpu.VMEM((nb * blk, ATTN_KV_HEADS * d), F32)],
        compiler_params=pltpu.CompilerParams(
            dimension_semantics=("parallel", "arbitrary"), vmem_limit_bytes=VMEM_LIMIT),
        name="swa_attn",
    )(cos3, sin3, p3, p3, p3, qn, kn, sink_col, ones_bd)


def _s5_prep_kernel(lre_ref, lim_ref, ls_ref, bre_ref, bim_ref, are_ref, aim_ref, obre_ref, obim_ref):
    lre = lre_ref[...]
    lim = lim_ref[...]
    dt = jnp.exp(ls_ref[...])
    mag = jnp.exp(lre * dt)
    are = mag * jnp.cos(lim * dt)
    aim = mag * jnp.sin(lim * dt)
    are_ref[...] = are
    aim_ref[...] = aim
    inv = 1.0 / (lre * lre + lim * lim)
    cre = ((are - 1.0) * lre + aim * lim) * inv
    cim = (aim * lre - (are - 1.0) * lim) * inv
    bre = bre_ref[...]
    bim = bim_ref[...]
    obre_ref[...] = cre[:, None, :] * bre - cim[:, None, :] * bim
    obim_ref[...] = cre[:, None, :] * bim + cim[:, None, :] * bre


def _s5_kernel(u_ref, are_ref, aim_ref, b_ref, c_ref, d_ref, gw_ref, gb_ref, o_ref, x_ref, s_ref):
    bsz, tc, ch = u_ref.shape
    hw = S5_WIDTH // 2
    hc = ch // 2

    @pl.when(pl.program_id(0) == 0)
    def _():
        s_ref[...] = jnp.zeros_like(s_ref)

    u = jnp.swapaxes(u_ref[...].astype(F32), 0, 1).reshape(tc * bsz, ch)
    for hf in range(2):
        x_ref[:, 2 * hw * hf:2 * hw * (hf + 1)] = _mm(u[:, hc * hf:hc * (hf + 1)], b_ref[hf])

    for hf in range(2):
        for s0 in range(0, hw, S5_STRIP):
            re = slice(2 * hw * hf + s0, 2 * hw * hf + s0 + S5_STRIP)
            im = slice(2 * hw * hf + hw + s0, 2 * hw * hf + hw + s0 + S5_STRIP)
            lam = slice(hw * hf + s0, hw * hf + s0 + S5_STRIP)
            ar = jnp.broadcast_to(are_ref[:, lam], (bsz, S5_STRIP))
            ai = jnp.broadcast_to(aim_ref[:, lam], (bsz, S5_STRIP))

            def step(t, carry, re=re, im=im, ar=ar, ai=ai):
                sr, si = carry
                rows = pl.ds(pl.multiple_of(t * bsz, bsz), bsz)
                nr = ar * sr - ai * si + x_ref[rows, re]
                ni = ar * si + ai * sr + x_ref[rows, im]
                x_ref[rows, re] = nr
                x_ref[rows, im] = ni
                return nr, ni

            sr, si = lax.fori_loop(0, tc, step, (s_ref[:, re], s_ref[:, im]), unroll=4)
            s_ref[:, re] = sr
            s_ref[:, im] = si

    y = jnp.concatenate([_mm(x_ref[:, 2 * hw * hf:2 * hw * (hf + 1)], c_ref[hf]) for hf in range(2)], axis=1)
    y = y + d_ref[...] * u
    z = 0.5 * y * (1.0 + lax.erf(y * (2.0 ** -0.5)))
    out = z * jax.nn.sigmoid(_mm(z, gw_ref[...]) + gb_ref[...])
    o_ref[...] = jnp.swapaxes(out.reshape(tc, bsz, ch), 0, 1).astype(o_ref.dtype)


def _s5(p3, lam_re, lam_im, log_step, b_re, b_im, c_re, c_im, d_skip, glu_w, glu_b):
    bsz, seq, _ = p3.shape
    g, st, ch = S5_GROUPS, S5_STATE, S5_GROUP
    gwd = GROUP_WIDTH
    vm = pl.BlockSpec(memory_space=pltpu.VMEM)
    a_re, a_im, bb_re, bb_im = pl.pallas_call(
        _s5_prep_kernel,
        in_specs=[vm] * 5,
        out_specs=[vm] * 4,
        out_shape=[jax.ShapeDtypeStruct((g, st), F32)] * 2 + [jax.ShapeDtypeStruct((g, ch, st), F32)] * 2,
        name="s5_prep",
    )(lam_re, lam_im, log_step.reshape(g, 1), jnp.swapaxes(b_re, 1, 2), jnp.swapaxes(b_im, 1, 2))

    gh = g // 2
    eye = jnp.eye(gh, dtype=F32)
    bd_in = lambda t: (t[:, :, None, :] * eye[:, None, :, None]).reshape(gh * ch, gh * st)
    bd_out = lambda t: (jnp.swapaxes(t, 1, 2)[:, :, None, :] * eye[:, None, :, None]).reshape(gh * st, gh * ch)
    halves = lambda t: (t[:gh], t[gh:])
    b_mat = jnp.stack([jnp.concatenate([bd_in(r), bd_in(i)], axis=1)
                       for r, i in zip(halves(bb_re), halves(bb_im))]).astype(BF16)
    c_mat = jnp.stack([jnp.concatenate([bd_out(r), -bd_out(i)], axis=0)
                       for r, i in zip(halves(c_re.astype(F32)), halves(c_im.astype(F32)))]).astype(BF16)

    tc = min(64, seq)
    full = lambda t: pl.BlockSpec(t.shape, lambda i: (0,) * t.ndim)
    params = [a_re.reshape(1, S5_WIDTH), a_im.reshape(1, S5_WIDTH), b_mat, c_mat,
              d_skip.reshape(1, gwd), glu_w.astype(BF16), glu_b.reshape(1, gwd)]
    return pl.pallas_call(
        _s5_kernel,
        grid=(seq // tc,),
        in_specs=[pl.BlockSpec((bsz, tc, gwd), lambda i: (0, i, S5_OFF_BLOCKS))] + [full(t) for t in params],
        out_specs=pl.BlockSpec((bsz, tc, gwd), lambda i: (0, i, 0)),
        out_shape=jax.ShapeDtypeStruct((bsz, seq, gwd), BF16),
        scratch_shapes=[pltpu.VMEM((tc * bsz, 2 * S5_WIDTH), F32), pltpu.VMEM((bsz, 2 * S5_WIDTH), F32)],
        compiler_params=pltpu.CompilerParams(
            dimension_semantics=("arbitrary",), vmem_limit_bytes=VMEM_LIMIT),
        name="s5_scan",
    )(p3, *params)


def _hgrn_kernel(qf_ref, ig_ref, lbraw_ref, gn_ref, tri_ref, bd_ref, o_ref, s_ref, osc_ref, *, layer):
    nb, tb, _ = qf_ref.shape
    cz = HGRN_CHUNK
    gw = GROUP_WIDTH
    n = HEAD_DIM
    nh = gw // n

    @pl.when(pl.program_id(1) == 0)
    def _():
        s_ref[...] = jnp.zeros_like(s_ref)

    qf = qf_ref[...].astype(F32).reshape(nb * tb, 2 * gw)
    ig = ig_ref[...].astype(F32).reshape(nb * tb, 2 * gw)
    q, f = qf[:, :gw], qf[:, gw:]
    v, g = ig[:, :gw], ig[:, gw:]

    lbr = lbraw_ref[...]
    e = jnp.exp(lbr - jnp.max(lbr, axis=0, keepdims=True))
    sm = e / jnp.sum(e, axis=0, keepdims=True)
    lb = jnp.zeros((1, gw), F32)
    for i in range(1, layer + 1):
        lb = lb + sm[i:i + 1, :]

    q = _silu(q) * (n ** -0.5)
    f_gate = lb + (1.0 - lb) * jax.nn.sigmoid(f)
    log_f = jnp.log(f_gate)
    k = 1.0 - f_gate

    parts = _split3(log_f)

    def per_seq(c_ref):
        cm = c_ref[...]
        seq_rows = lambda t, s: t[s * tb:(s + 1) * tb, :]
        return jnp.concatenate(
            [_dot(cm, seq_rows(parts[0], s), _NN)
             + (_dot(cm, seq_rows(parts[1], s), _NN) + _dot(cm, seq_rows(parts[2], s), _NN))
             for s in range(nb)], axis=0)

    bcum = per_seq(tri_ref)
    b3 = bcum.reshape(nb * tb // cz, cz, gw)
    chunk_row = lambda r: jnp.broadcast_to(b3[:, r:r + 1, :], b3.shape).reshape(nb * tb, gw)
    bmid = chunk_row(cz // 2 - 1)
    blast = chunk_row(cz - 1)
    qe = q * jnp.exp(bcum - bmid)
    ke = k * jnp.exp(bmid - bcum)
    kl = k * jnp.exp(blast - bcum)
    qb = q * jnp.exp(bcum)
    dec = jnp.exp(blast)

    ri = lax.broadcasted_iota(jnp.int32, (tb, tb), 0)
    ci = lax.broadcasted_iota(jnp.int32, (tb, tb), 1)
    mask = jnp.logical_and(ri // cz == ci // cz, ri >= ci)

    heads = lambda t: jnp.stack(
        [t[s * tb:(s + 1) * tb, h * n:(h + 1) * n] for s in range(nb) for h in range(nh)], axis=0)
    v_h, qb_h, kl_h, dec_h = heads(v), heads(qb), heads(kl), heads(dec)
    att = jnp.where(mask, _bmm(heads(qe), heads(ke), _BNT), 0.0)
    o_intra = _bmm(att, v_h)
    nchunk = tb // cz
    ti = lax.broadcasted_iota(jnp.int32, (tb, nchunk * n), 0)
    li = lax.broadcasted_iota(jnp.int32, (tb, nchunk * n), 1)
    kl_spread = jnp.where(ti // cz == li // n, jnp.concatenate([kl_h] * nchunk, axis=2), 0.0)
    kv_all = _bmm(v_h, kl_spread, _BTN)
    st = s_ref[...]
    o_inter = []
    for j in range(nchunk):
        rows = slice(j * cz, (j + 1) * cz)
        o_inter.append(_bmm(qb_h[:, rows, :], st, _BNT))
        st = st * dec_h[:, j * cz:j * cz + 1, :] + kv_all[:, :, j * n:(j + 1) * n]
    s_ref[...] = st
    o_heads = o_intra + jnp.concatenate(o_inter, axis=1)
    for s in range(nb):
        for h in range(nh):
            osc_ref[s * tb:(s + 1) * tb, h * n:(h + 1) * n] = o_heads[s * nh + h]

    o = osc_ref[...]
    ms = _head_sum(o * o, bd_ref[...]) * (1.0 / n)
    o_ref[...] = (o * lax.rsqrt(ms + NORM_EPS) * gn_ref[...] * _silu(g)).reshape(nb, tb, gw).astype(o_ref.dtype)


def _hgrn(p3, lower_bounds, g_norm, layer):
    bsz, seq, _ = p3.shape
    tb = min(HGRN_TILE, seq)
    cz = HGRN_CHUNK
    gw = GROUP_WIDTH
    idx = np.arange(tb)
    same = (idx[:, None] // cz) == (idx[None, :] // cz)
    tri = same & (idx[:, None] >= idx[None, :])
    consts = [jnp.asarray(tri.astype(np.float32), BF16)]
    ones_bd = jnp.asarray(np.kron(np.eye(4, dtype=np.float32), np.ones((HEAD_DIM, HEAD_DIM), np.float32)), BF16)
    gn = jnp.tile(g_norm.astype(F32), gw // HEAD_DIM).reshape(1, gw)
    params = [lower_bounds.astype(F32), gn] + consts + [ones_bd]
    full = lambda t: pl.BlockSpec(t.shape, lambda i, j: (0,) * t.ndim)
    nb = HGRN_SEQS if bsz % HGRN_SEQS == 0 else 1
    col = lambda o: pl.BlockSpec((nb, tb, 2 * gw), lambda i, j: (i, j, HGRN_OFF_BLOCKS + o))
    return pl.pallas_call(
        functools.partial(_hgrn_kernel, layer=layer),
        grid=(bsz // nb, seq // tb),
        in_specs=[col(0), col(1)] + [full(t) for t in params],
        out_specs=pl.BlockSpec((nb, tb, gw), lambda i, j: (i, j, 0)),
        out_shape=jax.ShapeDtypeStruct((bsz, seq, gw), BF16),
        scratch_shapes=[pltpu.VMEM((nb * gw // HEAD_DIM, HEAD_DIM, HEAD_DIM), F32),
                        pltpu.VMEM((nb * tb, gw), F32)],
        compiler_params=pltpu.CompilerParams(
            dimension_semantics=("parallel", "arbitrary"), vmem_limit_bytes=VMEM_LIMIT),
        name="hgrn2",
    )(p3, p3, *params)


def kernel(x, positions, ffn1_norm, ffn1_w_gate, ffn1_w_up, ffn1_w_down, mix_norm, w_in, rwkv_mu, rwkv_w0, rwkv_w_up, rwkv_a0, rwkv_a_up, rwkv_g_up, rwkv_k_k, rwkv_k_a, rwkv_r_k, rwkv_ln_w, rwkv_ln_b, attn_q_norm, attn_k_norm, attn_sinks, s5_lambda_re, s5_lambda_im, s5_log_step, s5_b_re, s5_b_im, s5_c_re, s5_c_im, s5_d, s5_glu_w, s5_glu_b, hgrn_lower_bounds, hgrn_g_norm, w_out, ffn2_norm, ffn2_w_gate, ffn2_w_up, ffn2_w_down):
    bsz, seq, d = x.shape
    depth = w_in.shape[0]
    n = bsz * seq
    xf = x.reshape(n, d)
    rope_cos, rope_sin = _rope_tables(positions)
    for l in range(depth):
        xf = _ffn(xf, ffn1_norm[l], ffn1_w_gate[l].astype(BF16), ffn1_w_up[l].astype(BF16),
                  ffn1_w_down[l].astype(BF16))
        p3 = _proj(xf, mix_norm[l], w_in[l].astype(BF16)).reshape(bsz, seq, D_IN)
        y_a = _rwkv(p3, rwkv_mu[l], rwkv_w0[l], rwkv_w_up[l], rwkv_a0[l], rwkv_a_up[l], rwkv_g_up[l],
                    rwkv_k_k[l], rwkv_k_a[l], rwkv_r_k[l].reshape(-1), rwkv_ln_w[l], rwkv_ln_b[l])
        y_b = _attn(p3, rope_cos, rope_sin, attn_q_norm[l], attn_k_norm[l], attn_sinks[l])
        y_c = _s5(p3, s5_lambda_re[l], s5_lambda_im[l], s5_log_step[l], s5_b_re[l], s5_b_im[l],
                  s5_c_re[l], s5_c_im[l], s5_d[l], s5_glu_w[l], s5_glu_b[l])
        y_d = _hgrn(p3, hgrn_lower_bounds, hgrn_g_norm[l], l)
        ys = [t.reshape(n, GROUP_WIDTH) for t in (y_a, y_b, y_c, y_d)]
        xf = _outproj(xf, ys, w_out[l].astype(BF16))
        xf = _ffn(xf, ffn2_norm[l], ffn2_w_gate[l].astype(BF16), ffn2_w_up[l].astype(BF16),
                  ffn2_w_down[l].astype(BF16))
    return xf.reshape(bsz, seq, d)
```

```python
import functools

import jax
import jax.numpy as jnp
import numpy as np
from jax import lax
from jax.experimental import pallas as pl
from jax.experimental.pallas import tpu as pltpu

F32 = jnp.float32
BF16 = jnp.bfloat16

HEAD_DIM = 64
GROUP_WIDTH = 512
NORM_EPS = 1e-6
FFN_RES_WEIGHT = 0.5

RWKV_W_RANK = 64
RWKV_A_RANK = 64
RWKV_G_RANK = 128
RWKV_LN_EPS = 64e-5
RWKV_IN = 3 * GROUP_WIDTH + RWKV_W_RANK + RWKV_A_RANK + RWKV_G_RANK
RWKV_CHUNK = 64
RWKV_SEQS = 4

ATTN_HEADS = 8
ATTN_KV_HEADS = 2
ATTN_BLOCK = 128
ATTN_SEQS = 4
ROPE_THETA = 500000.0
ROPE_DIM = HEAD_DIM // 4
ATTN_COL_BLOCK = 256
ATTN_OFF_BLOCKS = RWKV_IN // ATTN_COL_BLOCK

S5_GROUP = 16
S5_GROUPS = GROUP_WIDTH // S5_GROUP
S5_STATE = 64
S5_WIDTH = S5_GROUPS * S5_STATE
S5_OFF_BLOCKS = (RWKV_IN + 768) // GROUP_WIDTH
S5_STRIP = 512

HGRN_CHUNK = 16
HGRN_TILE = 128
HGRN_SEQS = 4
HGRN_OFF_BLOCKS = 3

D_IN = 5120
VMEM_LIMIT = 56 * 1024 * 1024


def _dot(a, b, dims):
    return lax.dot_general(a, b, (dims, ((), ())), preferred_element_type=F32)


_NN = ((1,), (0,))


def _mm(a, b):
    return _dot(a.astype(BF16), b.astype(BF16), _NN)


_BNN = (((2,), (1,)), ((0,), (0,)))
_BNT = (((2,), (2,)), ((0,), (0,)))
_BTN = (((1,), (1,)), ((0,), (0,)))


def _bmm(a, b, dims=_BNN):
    return lax.dot_general(a.astype(BF16), b.astype(BF16), dims, preferred_element_type=F32)


def _split2(x):
    hi = x.astype(BF16)
    lo = (x - hi.astype(F32)).astype(BF16)
    return hi, lo


def _const_lhs_mm(c, x):
    hi, lo = _split2(x)
    return _dot(c, hi, _NN) + _dot(c, lo, _NN)


def _const_rhs_mm(x, c):
    hi, lo = _split2(x)
    return _dot(hi, c, _NN) + _dot(lo, c, _NN)


def _head_sum(x, ones_bd):
    w = ones_bd.shape[0]
    parts = [_const_rhs_mm(x[:, i:i + w], ones_bd) for i in range(0, x.shape[1], w)]
    return parts[0] if len(parts) == 1 else jnp.concatenate(parts, axis=1)


def _silu(x):
    return x * jax.nn.sigmoid(x)


def _rms_rows(x, gain):
    ms = jnp.mean(x * x, axis=-1, keepdims=True)
    return x * lax.rsqrt(ms + NORM_EPS) * gain


def _ffn_kernel(x_ref, g_ref, wg_ref, wu_ref, wd_ref, o_ref, h_ref):
    @pl.when(pl.program_id(1) == 0)
    def _():
        x = x_ref[...]
        h_ref[...] = _rms_rows(x, g_ref[...]).astype(BF16)
        o_ref[...] = x

    h = h_ref[...]
    gate = jnp.dot(h, wg_ref[...], preferred_element_type=F32)
    up = jnp.dot(h, wu_ref[...], preferred_element_type=F32)
    act = (_silu(gate) * up).astype(BF16)
    o_ref[...] += FFN_RES_WEIGHT * jnp.dot(act, wd_ref[...], preferred_element_type=F32)


def _ffn(x2d, gain, wg, wu, wd):
    n, d = x2d.shape
    f = wg.shape[1]
    tm = min(1024, n)
    tf = 512 if f % 512 == 0 else f
    return pl.pallas_call(
        _ffn_kernel,
        grid=(n // tm, f // tf),
        in_specs=[
            pl.BlockSpec((tm, d), lambda i, j: (i, 0)),
            pl.BlockSpec((1, d), lambda i, j: (0, 0)),
            pl.BlockSpec((d, tf), lambda i, j: (0, j)),
            pl.BlockSpec((d, tf), lambda i, j: (0, j)),
            pl.BlockSpec((tf, d), lambda i, j: (j, 0)),
        ],
        out_specs=pl.BlockSpec((tm, d), lambda i, j: (i, 0)),
        out_shape=jax.ShapeDtypeStruct((n, d), F32),
        scratch_shapes=[pltpu.VMEM((tm, d), BF16)],
        compiler_params=pltpu.CompilerParams(
            dimension_semantics=("parallel", "arbitrary"), vmem_limit_bytes=VMEM_LIMIT),
        name="ffn",
    )(x2d, gain.reshape(1, d), wg, wu, wd)


def _proj_kernel(x_ref, g_ref, w_ref, o_ref, h_ref):
    @pl.when(pl.program_id(1) == 0)
    def _():
        h_ref[...] = _rms_rows(x_ref[...], g_ref[...]).astype(BF16)

    o_ref[...] = jnp.dot(h_ref[...], w_ref[...], preferred_element_type=F32).astype(o_ref.dtype)


def _proj(x2d, gain, w):
    n, d = x2d.shape
    dout = w.shape[1]
    tm = min(1024, n)
    tn = dout // 4
    return pl.pallas_call(
        _proj_kernel,
        grid=(n // tm, dout // tn),
        in_specs=[
            pl.BlockSpec((tm, d), lambda i, j: (i, 0)),
            pl.BlockSpec((1, d), lambda i, j: (0, 0)),
            pl.BlockSpec((d, tn), lambda i, j: (0, j)),
        ],
        out_specs=pl.BlockSpec((tm, tn), lambda i, j: (i, j)),
        out_shape=jax.ShapeDtypeStruct((n, dout), BF16),
        scratch_shapes=[pltpu.VMEM((tm, d), BF16)],
        compiler_params=pltpu.CompilerParams(
            dimension_semantics=("parallel", "arbitrary"), vmem_limit_bytes=VMEM_LIMIT),
        name="in_proj",
    )(x2d, gain.reshape(1, d), w)


def _outproj_kernel(x_ref, ya_ref, yb_ref, yc_ref, yd_ref, w_ref, o_ref):
    gw = GROUP_WIDTH
    acc = x_ref[...]
    for m, y_ref in enumerate((ya_ref, yb_ref, yc_ref, yd_ref)):
        acc = acc + jnp.dot(y_ref[...].astype(BF16), w_ref[m * gw:(m + 1) * gw, :],
                            preferred_element_type=F32)
    o_ref[...] = acc


def _outproj(x2d, ys, w):
    n, d = x2d.shape
    tm = min(512, n)
    yspec = pl.BlockSpec((tm, GROUP_WIDTH), lambda i: (i, 0))
    return pl.pallas_call(
        _outproj_kernel,
        grid=(n // tm,),
        in_specs=[pl.BlockSpec((tm, d), lambda i: (i, 0)), yspec, yspec, yspec, yspec,
                  pl.BlockSpec(w.shape, lambda i: (0, 0))],
        out_specs=pl.BlockSpec((tm, d), lambda i: (i, 0)),
        out_shape=jax.ShapeDtypeStruct((n, d), F32),
        compiler_params=pltpu.CompilerParams(
            dimension_semantics=("parallel",), vmem_limit_bytes=VMEM_LIMIT),
        name="out_proj",
    )(x2d, *ys, w)


def _rwkv_kernel(p_ref, mu_ref, w0_ref, wup_ref, a0_ref, aup_ref, gup_ref, kk_ref, ka_ref, rk_ref,
                 lnw_ref, lnb_ref, tri_ref, bd_ref, o_ref, s_ref, prev_ref, osc_ref):
    nb, c, _ = p_ref.shape
    gw = GROUP_WIDTH
    n = HEAD_DIM
    nh = gw // n

    @pl.when(pl.program_id(1) == 0)
    def _():
        s_ref[...] = jnp.zeros_like(s_ref)
        prev_ref[...] = jnp.zeros_like(prev_ref)

    p = p_ref[...].astype(F32).reshape(nb * c, RWKV_IN)
    row = lax.broadcasted_iota(jnp.int32, p.shape, 0)
    shifted = pltpu.roll(p, 1, axis=0)
    for s in range(nb):
        shifted = jnp.where(row == s * c, prev_ref[s], shifted)
        prev_ref[s] = p[(s + 1) * c - 1:(s + 1) * c, :]
    p = p + (shifted - p) * mu_ref[...]

    r = p[:, 0:gw]
    k = p[:, gw:2 * gw]
    v = p[:, 2 * gw:3 * gw]
    o1 = 3 * gw
    w_lo = p[:, o1:o1 + RWKV_W_RANK]
    a_lo = p[:, o1 + RWKV_W_RANK:o1 + RWKV_W_RANK + RWKV_A_RANK]
    g_lo = p[:, o1 + RWKV_W_RANK + RWKV_A_RANK:]

    z = -(w0_ref[...] + _mm(jnp.tanh(w_lo), wup_ref[...]))
    softplus = jnp.maximum(z, 0.0) + jnp.log1p(jnp.exp(-jnp.abs(z)))
    lw = -jnp.exp(-softplus - 0.5)
    a = jax.nn.sigmoid(a0_ref[...] + _mm(a_lo, aup_ref[...]))
    g = _mm(jax.nn.sigmoid(g_lo), gup_ref[...])

    ones_bd = bd_ref[...]
    kk = k * kk_ref[...]
    kk = kk * lax.rsqrt(jnp.maximum(_head_sum(kk * kk, ones_bd), 1e-24))
    k = k * (1.0 + (a - 1.0) * ka_ref[...])
    b = kk * a

    cum = _const_lhs_mm(tri_ref[...], lw)
    cum_last = jnp.concatenate(
        [jnp.broadcast_to(cum[(s + 1) * c - 1:(s + 1) * c, :], (c, gw)) for s in range(nb)], axis=0)
    e_pos = jnp.exp(cum)
    e_neg = jnp.exp(-cum)
    e_end = jnp.exp(cum_last - cum)
    g_end = jnp.exp(cum_last)
    kq = kk * jnp.exp(cum - lw)
    rq = r * e_pos
    bd = b * e_neg
    kd = k * e_neg
    b_end = b * e_end
    k_end = k * e_end

    ri = lax.broadcasted_iota(jnp.int32, (c, c), 0)
    ci = lax.broadcasted_iota(jnp.int32, (c, c), 1)
    strict = ri > ci
    incl = ri >= ci
    eye = (ri == ci).astype(F32)

    heads = lambda t, rows=c: jnp.stack(
        [t[s * c:s * c + rows, h * n:(h + 1) * n] for s in range(nb) for h in range(nh)], axis=0)
    kq_h, rq_h, bd_h, kd_h, v_h = heads(kq), heads(rq), heads(bd), heads(kd), heads(v)
    s0 = s_ref[...]
    gmat = _bmm(jnp.concatenate([kq_h, rq_h], axis=1), jnp.concatenate([bd_h, kd_h], axis=1), _BNT)
    a_bb = jnp.where(strict, gmat[:, :c, :c], 0.0)
    a_bk = jnp.where(strict, gmat[:, :c, c:], 0.0)
    a_rb = jnp.where(incl, gmat[:, c:, :c], 0.0)
    a_rk = jnp.where(incl, gmat[:, c:, c:], 0.0)
    pw = -a_bb
    t_inv = eye + pw
    for _ in range(int(np.log2(c)) - 1):
        pw = _bmm(pw, pw)
        t_inv = t_inv + _bmm(t_inv, pw)
    x = _bmm(kq_h, s0, _BNT) + _bmm(a_bk, v_h)
    u = -_bmm(t_inv, x)
    y = _bmm(rq_h, s0, _BNT) + _bmm(a_rb, u) + _bmm(a_rk, v_h)
    s_ref[...] = s0 * heads(g_end, 1) + _bmm(jnp.concatenate([u, v_h], axis=1),
                                             jnp.concatenate([heads(b_end), heads(k_end)], axis=1), _BTN)
    for s in range(nb):
        for h in range(nh):
            osc_ref[s * c:(s + 1) * c, h * n:(h + 1) * n] = y[s * nh + h]

    o = osc_ref[...]
    inv_n = 1.0 / n
    mean = _head_sum(o, ones_bd) * inv_n
    dlt = o - mean
    var = _head_sum(dlt * dlt, ones_bd) * inv_n
    o = dlt * lax.rsqrt(var + RWKV_LN_EPS) * lnw_ref[...] + lnb_ref[...]
    bonus = _head_sum(r * k * rk_ref[...], ones_bd) * v
    o_ref[...] = ((o + bonus) * g).reshape(nb, c, gw).astype(o_ref.dtype)


def _rwkv(p3, mu, w0, w_up, a0, a_up, g_up, k_k, k_a, r_k, ln_w, ln_b):
    bsz, seq, _ = p3.shape
    c = RWKV_CHUNK
    gw = GROUP_WIDTH
    nb = RWKV_SEQS if bsz % RWKV_SEQS == 0 else 1
    tri = jnp.asarray(np.kron(np.eye(nb, dtype=np.float32), np.tril(np.ones((c, c), np.float32))), BF16)
    ones_bd = jnp.asarray(np.kron(np.eye(4, dtype=np.float32), np.ones((HEAD_DIM, HEAD_DIM), np.float32)), BF16)
    row = lambda t: t.reshape(1, -1)
    full = lambda t: pl.BlockSpec(t.shape, lambda i, j: (0,) * t.ndim)
    params = [row(mu), row(w0), w_up, row(a0), a_up, g_up, row(k_k), row(k_a), row(r_k), row(ln_w),
              row(ln_b), tri, ones_bd]
    return pl.pallas_call(
        _rwkv_kernel,
        grid=(bsz // nb, seq // c),
        in_specs=[pl.BlockSpec((nb, c, RWKV_IN), lambda i, j: (i, j, 0))] + [full(t) for t in params],
        out_specs=pl.BlockSpec((nb, c, gw), lambda i, j: (i, j, 0)),
        out_shape=jax.ShapeDtypeStruct((bsz, seq, gw), BF16),
        scratch_shapes=[pltpu.VMEM((nb * gw // HEAD_DIM, HEAD_DIM, HEAD_DIM), F32),
                        pltpu.VMEM((nb, 1, RWKV_IN), F32),
                        pltpu.VMEM((nb * c, gw), F32)],
        compiler_params=pltpu.CompilerParams(
            dimension_semantics=("parallel", "arbitrary"), vmem_limit_bytes=VMEM_LIMIT),
        name="rwkv7",
    )(p3, *params)


def _rope_table_kernel(pos_ref, invf_ref, cos_ref, sin_ref):
    ang = pos_ref[...].astype(F32) * invf_ref[...]
    cos_ref[...] = jnp.cos(ang)
    sin_ref[...] = jnp.sin(ang)


def _rope_tables(positions):
    bsz, seq = positions.shape
    d = HEAD_DIM
    w = 2 * d
    inv_freq = ROPE_THETA ** (-jnp.arange(0, ROPE_DIM, 2, dtype=F32) / ROPE_DIM)
    lane_d = np.arange(w) % d
    invf = jnp.where(lane_d < ROPE_DIM, inv_freq[lane_d % (ROPE_DIM // 2)], 0.0).reshape(1, w)
    tr = min(1024, bsz * seq)
    return pl.pallas_call(
        _rope_table_kernel,
        grid=(bsz * seq // tr,),
        in_specs=[pl.BlockSpec((tr, 1), lambda i: (i, 0)), pl.BlockSpec((1, w), lambda i: (0, 0))],
        out_specs=[pl.BlockSpec((tr, w), lambda i: (i, 0))] * 2,
        out_shape=[jax.ShapeDtypeStruct((bsz * seq, w), F32)] * 2,
        compiler_params=pltpu.CompilerParams(dimension_semantics=("parallel",)),
        name="rope_tables",
    )(positions.reshape(bsz * seq, 1), invf)


def _attn_kernel(cos_ref, sin_ref, q0_ref, q1_ref, kv_ref, qn_ref, kn_ref, sink_ref, bd_ref,
                 o_ref, kprev_ref, vprev_ref):
    nb, blk, _ = q0_ref.shape
    rows = nb * blk
    d = HEAD_DIM
    half = ROPE_DIM // 2
    group = ATTN_HEADS // ATTN_KV_HEADS
    kvw = ATTN_KV_HEADS * d
    first = pl.program_id(1) == 0

    @pl.when(first)
    def _():
        kprev_ref[...] = jnp.zeros_like(kprev_ref)
        vprev_ref[...] = jnp.zeros_like(vprev_ref)

    ones_bd = bd_ref[...]
    cos = cos_ref[...].reshape(rows, 2 * d)
    sin = sin_ref[...].reshape(rows, 2 * d)

    def norm_rope(x, gain):
        w = x.shape[1]
        x = x * lax.rsqrt(_head_sum(x * x, ones_bd[:w, :w]) * (1.0 / d) + NORM_EPS) * gain
        ld = lax.broadcasted_iota(jnp.int32, (rows, w), 1) % d
        rot = jnp.where(ld < half, -pltpu.roll(x, w - half, axis=1),
                        jnp.where(ld < ROPE_DIM, pltpu.roll(x, half, axis=1), 0.0))
        tile = lambda t: t if w == t.shape[1] else jnp.concatenate([t] * (w // t.shape[1]), axis=1)
        return x * tile(cos) + rot * tile(sin)

    qs = [norm_rope(q_ref[...].astype(F32).reshape(rows, ATTN_COL_BLOCK), qn_ref[...]) * (d ** -0.5)
          for q_ref in (q0_ref, q1_ref)]
    kv = kv_ref[...].astype(F32).reshape(rows, ATTN_COL_BLOCK)
    k_cur = norm_rope(kv[:, :kvw], kn_ref[:, :kvw])
    v_cur = kv[:, kvw:]
    k_prev = kprev_ref[...]
    v_prev = vprev_ref[...]

    heads_per_ref = ATTN_COL_BLOCK // d
    q_head = lambda s, qh: qs[qh // heads_per_ref][s * blk:(s + 1) * blk,
                                                   (qh % heads_per_ref) * d:(qh % heads_per_ref + 1) * d]
    q_g = jnp.stack([jnp.concatenate([q_head(s, kh * group + i) for i in range(group)], axis=0)
                     for s in range(nb) for kh in range(ATTN_KV_HEADS)], axis=0)
    kv_heads = lambda t: jnp.stack([t[s * blk:(s + 1) * blk, kh * d:(kh + 1) * d]
                                    for s in range(nb) for kh in range(ATTN_KV_HEADS)], axis=0)

    ri = lax.broadcasted_iota(jnp.int32, (group * blk, blk), 0) % blk
    ci = lax.broadcasted_iota(jnp.int32, (group * blk, blk), 1)
    neg_inf = -jnp.inf
    no_prev = jnp.where(first, neg_inf, 0.0)
    s_prev = jnp.where(ci > ri, _bmm(q_g, kv_heads(k_prev), _BNT), neg_inf) + no_prev
    s_cur = jnp.where(ci <= ri, _bmm(q_g, kv_heads(k_cur), _BNT), neg_inf)
    sink = sink_ref[...]
    m = jnp.maximum(jnp.maximum(jnp.max(s_prev, axis=2, keepdims=True),
                                jnp.max(s_cur, axis=2, keepdims=True)), sink)
    p_prev = jnp.exp(s_prev - m)
    p_cur = jnp.exp(s_cur - m)
    ones = jnp.ones((nb * ATTN_KV_HEADS, blk, d), BF16)
    den = _bmm(p_prev, ones) + _bmm(p_cur, ones) + jnp.exp(sink - m)
    o = ((_bmm(p_prev, kv_heads(v_prev)) + _bmm(p_cur, kv_heads(v_cur))) / den).astype(o_ref.dtype)
    for s in range(nb):
        for qh in range(ATTN_HEADS):
            o_ref[s, :, qh * d:(qh + 1) * d] = o[s * ATTN_KV_HEADS + qh // group,
                                                 (qh % group) * blk:(qh % group + 1) * blk, :]

    kprev_ref[...] = k_cur
    vprev_ref[...] = v_cur


def _attn(p3, rope_cos, rope_sin, q_norm, k_norm, sinks):
    bsz, seq, _ = p3.shape
    blk = ATTN_BLOCK
    cw = ATTN_COL_BLOCK
    d = HEAD_DIM
    nb = ATTN_SEQS if bsz % ATTN_SEQS == 0 else 1
    qn = jnp.tile(q_norm.astype(F32), cw // d).reshape(1, cw)
    kn = jnp.tile(k_norm.astype(F32), cw // d).reshape(1, cw)
    ones_bd = jnp.asarray(np.kron(np.eye(cw // d, dtype=np.float32), np.ones((d, d), np.float32)), BF16)
    group = ATTN_HEADS // ATTN_KV_HEADS
    sink_col = jnp.repeat(sinks.astype(F32).reshape(ATTN_KV_HEADS, group), blk, axis=1)[:, :, None]
    sink_col = jnp.tile(sink_col, (nb, 1, 1))
    tw = rope_cos.shape[-1]
    cos3 = rope_cos.reshape(bsz, seq, tw)
    sin3 = rope_sin.reshape(bsz, seq, tw)
    full = lambda t: pl.BlockSpec(t.shape, lambda i, j: (0,) * t.ndim)
    col = lambda o: pl.BlockSpec((nb, blk, cw), lambda i, j: (i, j, ATTN_OFF_BLOCKS + o))
    trig = pl.BlockSpec((nb, blk, tw), lambda i, j: (i, j, 0))
    return pl.pallas_call(
        _attn_kernel,
        grid=(bsz // nb, seq // blk),
        in_specs=[trig, trig, col(0), col(1), col(2), full(qn), full(kn), full(sink_col), full(ones_bd)],
        out_specs=pl.BlockSpec((nb, blk, GROUP_WIDTH), lambda i, j: (i, j, 0)),
        out_shape=jax.ShapeDtypeStruct((bsz, seq, GROUP_WIDTH), BF16),
        scratch_shapes=[pltpu.VMEM((nb * blk, ATTN_KV_HEADS * d), F32),
                        pltpu.VMEM((nb * blk, ATTN_KV_HEADS * d), F32)],
        compiler_params=pltpu.CompilerParams(
            dimension_semantics=("parallel", "arbitrary"), vmem_limit_bytes=VMEM_LIMIT),
        name="swa_attn",
    )(cos3, sin3, p3, p3, p3, qn, kn, sink_col, ones_bd)


def _s5_prep_kernel(lre_ref, lim_ref, ls_ref, bre_ref, bim_ref, are_ref, aim_ref, obre_ref, obim_ref):
    lre = lre_ref[...]
    lim = lim_ref[...]
    dt = jnp.exp(ls_ref[...])
    mag = jnp.exp(lre * dt)
    are = mag * jnp.cos(lim * dt)
    aim = mag * jnp.sin(lim * dt)
    are_ref[...] = are
    aim_ref[...] = aim
    inv = 1.0 / (lre * lre + lim * lim)
    cre = ((are - 1.0) * lre + aim * lim) * inv
    cim = (aim * lre - (are - 1.0) * lim) * inv
    bre = bre_ref[...]
    bim = bim_ref[...]
    obre_ref[...] = cre[:, None, :] * bre - cim[:, None, :] * bim
    obim_ref[...] = cre[:, None, :] * bim + cim[:, None, :] * bre


def _s5_kernel(u_ref, are_ref, aim_ref, b_ref, c_ref, d_ref, gw_ref, gb_ref, o_ref, x_ref, s_ref):
    bsz, tc, ch = u_ref.shape
    hw = S5_WIDTH // 2
    hc = ch // 2

    @pl.when(pl.program_id(0) == 0)
    def _():
        s_ref[...] = jnp.zeros_like(s_ref)

    u = jnp.swapaxes(u_ref[...].astype(F32), 0, 1).reshape(tc * bsz, ch)
    for hf in range(2):
        x_ref[:, 2 * hw * hf:2 * hw * (hf + 1)] = _mm(u[:, hc * hf:hc * (hf + 1)], b_ref[hf])

    for hf in range(2):
        for s0 in range(0, hw, S5_STRIP):
            re = slice(2 * hw * hf + s0, 2 * hw * hf + s0 + S5_STRIP)
            im = slice(2 * hw * hf + hw + s0, 2 * hw * hf + hw + s0 + S5_STRIP)
            lam = slice(hw * hf + s0, hw * hf + s0 + S5_STRIP)
            ar = jnp.broadcast_to(are_ref[:, lam], (bsz, S5_STRIP))
            ai = jnp.broadcast_to(aim_ref[:, lam], (bsz, S5_STRIP))

            def step(t, carry, re=re, im=im, ar=ar, ai=ai):
                sr, si = carry
                rows = pl.ds(pl.multiple_of(t * bsz, bsz), bsz)
                nr = ar * sr - ai * si + x_ref[rows, re]
                ni = ar * si + ai * sr + x_ref[rows, im]
                x_ref[rows, re] = nr
                x_ref[rows, im] = ni
                return nr, ni

            sr, si = lax.fori_loop(0, tc, step, (s_ref[:, re], s_ref[:, im]), unroll=4)
            s_ref[:, re] = sr
            s_ref[:, im] = si

    y = jnp.concatenate([_mm(x_ref[:, 2 * hw * hf:2 * hw * (hf + 1)], c_ref[hf]) for hf in range(2)], axis=1)
    y = y + d_ref[...] * u
    z = 0.5 * y * (1.0 + lax.erf(y * (2.0 ** -0.5)))
    out = z * jax.nn.sigmoid(_mm(z, gw_ref[...]) + gb_ref[...])
    o_ref[...] = jnp.swapaxes(out.reshape(tc, bsz, ch), 0, 1).astype(o_ref.dtype)


def _s5(p3, lam_re, lam_im, log_step, b_re, b_im, c_re, c_im, d_skip, glu_w, glu_b):
    bsz, seq, _ = p3.shape
    g, st, ch = S5_GROUPS, S5_STATE, S5_GROUP
    gwd = GROUP_WIDTH
    vm = pl.BlockSpec(memory_space=pltpu.VMEM)
    a_re, a_im, bb_re, bb_im = pl.pallas_call(
        _s5_prep_kernel,
        in_specs=[vm] * 5,
        out_specs=[vm] * 4,
        out_shape=[jax.ShapeDtypeStruct((g, st), F32)] * 2 + [jax.ShapeDtypeStruct((g, ch, st), F32)] * 2,
        name="s5_prep",
    )(lam_re, lam_im, log_step.reshape(g, 1), jnp.swapaxes(b_re, 1, 2), jnp.swapaxes(b_im, 1, 2))

    gh = g // 2
    eye = jnp.eye(gh, dtype=F32)
    bd_in = lambda t: (t[:, :, None, :] * eye[:, None, :, None]).reshape(gh * ch, gh * st)
    bd_out = lambda t: (jnp.swapaxes(t, 1, 2)[:, :, None, :] * eye[:, None, :, None]).reshape(gh * st, gh * ch)
    halves = lambda t: (t[:gh], t[gh:])
    b_mat = jnp.stack([jnp.concatenate([bd_in(r), bd_in(i)], axis=1)
                       for r, i in zip(halves(bb_re), halves(bb_im))]).astype(BF16)
    c_mat = jnp.stack([jnp.concatenate([bd_out(r), -bd_out(i)], axis=0)
                       for r, i in zip(halves(c_re.astype(F32)), halves(c_im.astype(F32)))]).astype(BF16)

    tc = min(64, seq)
    full = lambda t: pl.BlockSpec(t.shape, lambda i: (0,) * t.ndim)
    params = [a_re.reshape(1, S5_WIDTH), a_im.reshape(1, S5_WIDTH), b_mat, c_mat,
              d_skip.reshape(1, gwd), glu_w.astype(BF16), glu_b.reshape(1, gwd)]
    return pl.pallas_call(
        _s5_kernel,
        grid=(seq // tc,),
        in_specs=[pl.BlockSpec((bsz, tc, gwd), lambda i: (0, i, S5_OFF_BLOCKS))] + [full(t) for t in params],
        out_specs=pl.BlockSpec((bsz, tc, gwd), lambda i: (0, i, 0)),
        out_shape=jax.ShapeDtypeStruct((bsz, seq, gwd), BF16),
        scratch_shapes=[pltpu.VMEM((tc * bsz, 2 * S5_WIDTH), F32), pltpu.VMEM((bsz, 2 * S5_WIDTH), F32)],
        compiler_params=pltpu.CompilerParams(
            dimension_semantics=("arbitrary",), vmem_limit_bytes=VMEM_LIMIT),
        name="s5_scan",
    )(p3, *params)


def _hgrn_kernel(qf_ref, ig_ref, lbraw_ref, gn_ref, tri_ref, bd_ref, o_ref, s_ref, osc_ref, *, layer):
    nb, tb, _ = qf_ref.shape
    cz = HGRN_CHUNK
    gw = GROUP_WIDTH
    n = HEAD_DIM
    nh = gw // n

    @pl.when(pl.program_id(1) == 0)
    def _():
        s_ref[...] = jnp.zeros_like(s_ref)

    qf = qf_ref[...].astype(F32).reshape(nb * tb, 2 * gw)
    ig = ig_ref[...].astype(F32).reshape(nb * tb, 2 * gw)
    q, f = qf[:, :gw], qf[:, gw:]
    v, g = ig[:, :gw], ig[:, gw:]

    lbr = lbraw_ref[...]
    e = jnp.exp(lbr - jnp.max(lbr, axis=0, keepdims=True))
    sm = e / jnp.sum(e, axis=0, keepdims=True)
    lb = jnp.zeros((1, gw), F32)
    for i in range(1, layer + 1):
        lb = lb + sm[i:i + 1, :]

    q = _silu(q) * (n ** -0.5)
    f_gate = lb + (1.0 - lb) * jax.nn.sigmoid(f)
    log_f = jnp.log(f_gate)
    k = 1.0 - f_gate

    parts = _split2(log_f)

    def per_seq(c_ref):
        cm = c_ref[...]
        seq_rows = lambda t, s: t[s * tb:(s + 1) * tb, :]
        return jnp.concatenate(
            [_dot(cm, seq_rows(parts[0], s), _NN) + _dot(cm, seq_rows(parts[1], s), _NN)
             for s in range(nb)], axis=0)

    bcum = per_seq(tri_ref)
    b3 = bcum.reshape(nb * tb // cz, cz, gw)
    chunk_row = lambda r: jnp.broadcast_to(b3[:, r:r + 1, :], b3.shape).reshape(nb * tb, gw)
    bmid = chunk_row(cz // 2 - 1)
    blast = chunk_row(cz - 1)
    qe = q * jnp.exp(bcum - bmid)
    ke = k * jnp.exp(bmid - bcum)
    kl = k * jnp.exp(blast - bcum)
    qb = q * jnp.exp(bcum)
    dec = jnp.exp(blast)

    ri = lax.broadcasted_iota(jnp.int32, (tb, tb), 0)
    ci = lax.broadcasted_iota(jnp.int32, (tb, tb), 1)
    mask = jnp.logical_and(ri // cz == ci // cz, ri >= ci)

    heads = lambda t: jnp.stack(
        [t[s * tb:(s + 1) * tb, h * n:(h + 1) * n] for s in range(nb) for h in range(nh)], axis=0)
    v_h, qb_h, kl_h, dec_h = heads(v), heads(qb), heads(kl), heads(dec)
    att = jnp.where(mask, _bmm(heads(qe), heads(ke), _BNT), 0.0)
    o_intra = _bmm(att, v_h)
    nchunk = tb // cz
    ti = lax.broadcasted_iota(jnp.int32, (tb, nchunk * n), 0)
    li = lax.broadcasted_iota(jnp.int32, (tb, nchunk * n), 1)
    kl_spread = jnp.where(ti // cz == li // n, jnp.concatenate([kl_h] * nchunk, axis=2), 0.0)
    kv_all = _bmm(v_h, kl_spread, _BTN)
    st = s_ref[...]
    o_inter = []
    for j in range(nchunk):
        rows = slice(j * cz, (j + 1) * cz)
        o_inter.append(_bmm(qb_h[:, rows, :], st, _BNT))
        st = st * dec_h[:, j * cz:j * cz + 1, :] + kv_all[:, :, j * n:(j + 1) * n]
    s_ref[...] = st
    o_heads = o_intra + jnp.concatenate(o_inter, axis=1)
    for s in range(nb):
        for h in range(nh):
            osc_ref[s * tb:(s + 1) * tb, h * n:(h + 1) * n] = o_heads[s * nh + h]

    o = osc_ref[...]
    ms = _head_sum(o * o, bd_ref[...]) * (1.0 / n)
    o_ref[...] = (o * lax.rsqrt(ms + NORM_EPS) * gn_ref[...] * _silu(g)).reshape(nb, tb, gw).astype(o_ref.dtype)


def _hgrn(p3, lower_bounds, g_norm, layer):
    bsz, seq, _ = p3.shape
    tb = min(HGRN_TILE, seq)
    cz = HGRN_CHUNK
    gw = GROUP_WIDTH
    idx = np.arange(tb)
    same = (idx[:, None] // cz) == (idx[None, :] // cz)
    tri = same & (idx[:, None] >= idx[None, :])
    consts = [jnp.asarray(tri.astype(np.float32), BF16)]
    ones_bd = jnp.asarray(np.kron(np.eye(4, dtype=np.float32), np.ones((HEAD_DIM, HEAD_DIM), np.float32)), BF16)
    gn = jnp.tile(g_norm.astype(F32), gw // HEAD_DIM).reshape(1, gw)
    params = [lower_bounds.astype(F32), gn] + consts + [ones_bd]
    full = lambda t: pl.BlockSpec(t.shape, lambda i, j: (0,) * t.ndim)
    nb = HGRN_SEQS if bsz % HGRN_SEQS == 0 else 1
    col = lambda o: pl.BlockSpec((nb, tb, 2 * gw), lambda i, j: (i, j, HGRN_OFF_BLOCKS + o))
    return pl.pallas_call(
        functools.partial(_hgrn_kernel, layer=layer),
        grid=(bsz // nb, seq // tb),
        in_specs=[col(0), col(1)] + [full(t) for t in params],
        out_specs=pl.BlockSpec((nb, tb, gw), lambda i, j: (i, j, 0)),
        out_shape=jax.ShapeDtypeStruct((bsz, seq, gw), BF16),
        scratch_shapes=[pltpu.VMEM((nb * gw // HEAD_DIM, HEAD_DIM, HEAD_DIM), F32),
                        pltpu.VMEM((nb * tb, gw), F32)],
        compiler_params=pltpu.CompilerParams(
            dimension_semantics=("parallel", "arbitrary"), vmem_limit_bytes=VMEM_LIMIT),
        name="hgrn2",
    )(p3, p3, *params)


def kernel(x, positions, ffn1_norm, ffn1_w_gate, ffn1_w_up, ffn1_w_down, mix_norm, w_in, rwkv_mu, rwkv_w0, rwkv_w_up, rwkv_a0, rwkv_a_up, rwkv_g_up, rwkv_k_k, rwkv_k_a, rwkv_r_k, rwkv_ln_w, rwkv_ln_b, attn_q_norm, attn_k_norm, attn_sinks, s5_lambda_re, s5_lambda_im, s5_log_step, s5_b_re, s5_b_im, s5_c_re, s5_c_im, s5_d, s5_glu_w, s5_glu_b, hgrn_lower_bounds, hgrn_g_norm, w_out, ffn2_norm, ffn2_w_gate, ffn2_w_up, ffn2_w_down):
    bsz, seq, d = x.shape
    depth = w_in.shape[0]
    n = bsz * seq
    xf = x.reshape(n, d)
    rope_cos, rope_sin = _rope_tables(positions)
    for l in range(depth):
        xf = _ffn(xf, ffn1_norm[l], ffn1_w_gate[l].astype(BF16), ffn1_w_up[l].astype(BF16),
                  ffn1_w_down[l].astype(BF16))
        p3 = _proj(xf, mix_norm[l], w_in[l].astype(BF16)).reshape(bsz, seq, D_IN)
        y_a = _rwkv(p3, rwkv_mu[l], rwkv_w0[l], rwkv_w_up[l], rwkv_a0[l], rwkv_a_up[l], rwkv_g_up[l],
                    rwkv_k_k[l], rwkv_k_a[l], rwkv_r_k[l].reshape(-1), rwkv_ln_w[l], rwkv_ln_b[l])
        y_b = _attn(p3, rope_cos, rope_sin, attn_q_norm[l], attn_k_norm[l], attn_sinks[l])
        y_c = _s5(p3, s5_lambda_re[l], s5_lambda_im[l], s5_log_step[l], s5_b_re[l], s5_b_im[l],
                  s5_c_re[l], s5_c_im[l], s5_d[l], s5_glu_w[l], s5_glu_b[l])
        y_d = _hgrn(p3, hgrn_lower_bounds, hgrn_g_norm[l], l)
        ys = [t.reshape(n, GROUP_WIDTH) for t in (y_a, y_b, y_c, y_d)]
        xf = _outproj(xf, ys, w_out[l].astype(BF16))
        xf = _ffn(xf, ffn2_norm[l], ffn2_w_gate[l].astype(BF16), ffn2_w_up[l].astype(BF16),
                  ffn2_w_down[l].astype(BF16))
    return xf.reshape(bsz, seq, d)
```

```python
import functools

import jax
import jax.numpy as jnp
import numpy as np
from jax import lax
from jax.experimental import pallas as pl
from jax.experimental.pallas import tpu as pltpu

F32 = jnp.float32
BF16 = jnp.bfloat16

HEAD_DIM = 64
GROUP_WIDTH = 512
NORM_EPS = 1e-6
FFN_RES_WEIGHT = 0.5

RWKV_W_RANK = 64
RWKV_A_RANK = 64
RWKV_G_RANK = 128
RWKV_LN_EPS = 64e-5
RWKV_IN = 3 * GROUP_WIDTH + RWKV_W_RANK + RWKV_A_RANK + RWKV_G_RANK
RWKV_CHUNK = 64
RWKV_BLOCK = 128
MIXER_SEQS = 4

ATTN_HEADS = 8
ATTN_KV_HEADS = 2
ATTN_BLOCK = 128
ROPE_THETA = 500000.0
ROPE_DIM = HEAD_DIM // 4
ATTN_COL_BLOCK = 256
ATTN_OFF_BLOCKS = RWKV_IN // ATTN_COL_BLOCK

S5_GROUP = 16
S5_GROUPS = GROUP_WIDTH // S5_GROUP
S5_STATE = 64
S5_WIDTH = S5_GROUPS * S5_STATE
S5_OFF_BLOCKS = (RWKV_IN + 768) // GROUP_WIDTH
S5_STRIP = 512

HGRN_CHUNK = 16
HGRN_TILE = 128
HGRN_OFF_BLOCKS = 3

D_IN = 5120
VMEM_LIMIT = 56 * 1024 * 1024


def _dot(a, b, dims):
    return lax.dot_general(a, b, (dims, ((), ())), preferred_element_type=F32)


_NN = ((1,), (0,))


def _mm(a, b):
    return _dot(a.astype(BF16), b.astype(BF16), _NN)


_BNN = (((2,), (1,)), ((0,), (0,)))
_BNT = (((2,), (2,)), ((0,), (0,)))
_BTN = (((1,), (1,)), ((0,), (0,)))


def _bmm(a, b, dims=_BNN):
    return lax.dot_general(a.astype(BF16), b.astype(BF16), dims, preferred_element_type=F32)


def _split2(x):
    hi = x.astype(BF16)
    lo = (x - hi.astype(F32)).astype(BF16)
    return hi, lo


def _const_lhs_mm(c, x):
    hi, lo = _split2(x)
    return _dot(c, hi, _NN) + _dot(c, lo, _NN)


def _const_rhs_mm(x, c):
    hi, lo = _split2(x)
    return _dot(hi, c, _NN) + _dot(lo, c, _NN)


def _head_sum(x, ones_bd):
    w = ones_bd.shape[0]
    parts = [_const_rhs_mm(x[:, i:i + w], ones_bd) for i in range(0, x.shape[1], w)]
    return parts[0] if len(parts) == 1 else jnp.concatenate(parts, axis=1)


def _silu(x):
    return x * jax.nn.sigmoid(x)


def _rms_rows(x, gain):
    ms = jnp.mean(x * x, axis=-1, keepdims=True)
    return x * lax.rsqrt(ms + NORM_EPS) * gain


def _ffn_kernel(x_ref, g_ref, wg_ref, wu_ref, wd_ref, *rest):
    n_riders = (len(rest) - 2) // 2
    o_ref, h_ref = rest[n_riders], rest[-1]

    @pl.when(pl.program_id(1) == 0)
    def _():
        x = x_ref[...]
        h_ref[...] = _rms_rows(x, g_ref[...]).astype(BF16)
        o_ref[...] = x

    h = h_ref[...]
    gate = jnp.dot(h, wg_ref[...], preferred_element_type=F32)
    up = jnp.dot(h, wu_ref[...], preferred_element_type=F32)
    act = (_silu(gate) * up).astype(BF16)
    o_ref[...] += FFN_RES_WEIGHT * jnp.dot(act, wd_ref[...], preferred_element_type=F32)
    for src, dst in zip(rest[:n_riders], rest[n_riders + 1:-1]):
        dst[...] = src[...].astype(BF16)


def _rider_spec(shape, gi, gj):
    r, c = shape
    steps = gi * gj
    if r % steps == 0 and (r // steps) % 16 == 0:
        return pl.BlockSpec((r // steps, c), lambda i, j: (i * gj + j, 0))
    if r % gi == 0 and (r // gi) % 16 == 0 and c % gj == 0 and (c // gj) % 128 == 0:
        return pl.BlockSpec((r // gi, c // gj), lambda i, j: (i, j))
    return None


def _ffn(x2d, gain, wg, wu, wd, riders=()):
    n, d = x2d.shape
    f = wg.shape[1]
    tm = min(1024, n)
    tf = 512 if f % 512 == 0 else f
    gi, gj = n // tm, f // tf
    specs = [_rider_spec(r.shape, gi, gj) for r in riders]
    ride = [k for k, s in enumerate(specs) if s is not None]
    outs = pl.pallas_call(
        _ffn_kernel,
        grid=(gi, gj),
        in_specs=[
            pl.BlockSpec((tm, d), lambda i, j: (i, 0)),
            pl.BlockSpec((1, d), lambda i, j: (0, 0)),
            pl.BlockSpec((d, tf), lambda i, j: (0, j)),
            pl.BlockSpec((d, tf), lambda i, j: (0, j)),
            pl.BlockSpec((tf, d), lambda i, j: (j, 0)),
        ] + [specs[k] for k in ride],
        out_specs=[pl.BlockSpec((tm, d), lambda i, j: (i, 0))] + [specs[k] for k in ride],
        out_shape=[jax.ShapeDtypeStruct((n, d), F32)]
        + [jax.ShapeDtypeStruct(riders[k].shape, BF16) for k in ride],
        scratch_shapes=[pltpu.VMEM((tm, d), BF16)],
        compiler_params=pltpu.CompilerParams(
            dimension_semantics=("parallel", "arbitrary"), vmem_limit_bytes=VMEM_LIMIT),
        name="ffn",
    )(x2d, gain.reshape(1, d), wg, wu, wd, *[riders[k] for k in ride])
    cast = [r.astype(BF16) for r in riders]
    for pos, k in enumerate(ride):
        cast[k] = outs[1 + pos]
    return outs[0], cast


def _proj_kernel(x_ref, g_ref, w_ref, o_ref, h_ref):
    @pl.when(pl.program_id(1) == 0)
    def _():
        h_ref[...] = _rms_rows(x_ref[...], g_ref[...]).astype(BF16)

    o_ref[...] = jnp.dot(h_ref[...], w_ref[...], preferred_element_type=F32).astype(o_ref.dtype)


def _proj(x2d, gain, w):
    n, d = x2d.shape
    dout = w.shape[1]
    tm = min(1024, n)
    tn = dout // 4
    return pl.pallas_call(
        _proj_kernel,
        grid=(n // tm, dout // tn),
        in_specs=[
            pl.BlockSpec((tm, d), lambda i, j: (i, 0)),
            pl.BlockSpec((1, d), lambda i, j: (0, 0)),
            pl.BlockSpec((d, tn), lambda i, j: (0, j)),
        ],
        out_specs=pl.BlockSpec((tm, tn), lambda i, j: (i, j)),
        out_shape=jax.ShapeDtypeStruct((n, dout), BF16),
        scratch_shapes=[pltpu.VMEM((tm, d), BF16)],
        compiler_params=pltpu.CompilerParams(
            dimension_semantics=("parallel", "arbitrary"), vmem_limit_bytes=VMEM_LIMIT),
        name="in_proj",
    )(x2d, gain.reshape(1, d), w)


def _outproj_kernel(x_ref, ya_ref, yb_ref, yc_ref, yd_ref, w_ref, o_ref):
    gw = GROUP_WIDTH
    acc = x_ref[...]
    for m, y_ref in enumerate((ya_ref, yb_ref, yc_ref, yd_ref)):
        acc = acc + jnp.dot(y_ref[...].astype(BF16), w_ref[m * gw:(m + 1) * gw, :],
                            preferred_element_type=F32)
    o_ref[...] = acc


def _outproj(x2d, ys, w):
    n, d = x2d.shape
    tm = min(512, n)
    yspec = pl.BlockSpec((tm, GROUP_WIDTH), lambda i: (i, 0))
    return pl.pallas_call(
        _outproj_kernel,
        grid=(n // tm,),
        in_specs=[pl.BlockSpec((tm, d), lambda i: (i, 0)), yspec, yspec, yspec, yspec,
                  pl.BlockSpec(w.shape, lambda i: (0, 0))],
        out_specs=pl.BlockSpec((tm, d), lambda i: (i, 0)),
        out_shape=jax.ShapeDtypeStruct((n, d), F32),
        compiler_params=pltpu.CompilerParams(
            dimension_semantics=("parallel",), vmem_limit_bytes=VMEM_LIMIT),
        name="out_proj",
    )(x2d, *ys, w)


def _advance(fillers):
    for gen in fillers:
        if next(gen, _DONE) is not _DONE:
            return


_DONE = object()


def _rwkv_kernel(p_ref, mu_ref, w0_ref, wup_ref, a0_ref, aup_ref, gup_ref, kk_ref, ka_ref, rk_ref,
                 lnw_ref, lnb_ref, tri_ref, bd_ref, o_ref, s_ref, prev_ref, osc_ref):
    c = RWKV_CHUNK
    nck = p_ref.shape[1] // c
    prm = dict(mu=mu_ref, w0=w0_ref, wup=wup_ref, a0=a0_ref, aup=aup_ref, gup=gup_ref, kk=kk_ref, ka=ka_ref,
               rk=rk_ref, lnw=lnw_ref, lnb=lnb_ref, tri=tri_ref, bd=bd_ref)
    chunk = lambda ref, ck: ref.at[:, pl.ds(ck * c, c), :]
    preps = [dict() for _ in range(nck)]
    for _ in _rwkv_prologue(chunk(p_ref, 0), prm, prev_ref, preps[0]):
        pass
    epilogue = iter(())
    for ck in range(nck):
        fillers = [epilogue]
        if ck + 1 < nck:
            fillers.append(_rwkv_prologue(chunk(p_ref, ck + 1), prm, prev_ref, preps[ck + 1]))
        y = _rwkv_solve(preps[ck], s_ref, functools.partial(_advance, fillers))
        for gen in fillers:
            for _ in gen:
                pass
        epilogue = _rwkv_epilogue(preps[ck], y, prm, chunk(o_ref, ck), osc_ref)
    for _ in epilogue:
        pass


def _rwkv_prologue(p_ref, prm, prev_ref, out):
    nb, c, _ = p_ref.shape
    gw = GROUP_WIDTH
    n = HEAD_DIM
    nh = gw // n

    p = p_ref[...].astype(F32).reshape(nb * c, RWKV_IN)
    row = lax.broadcasted_iota(jnp.int32, p.shape, 0)
    shifted = pltpu.roll(p, 1, axis=0)
    for s in range(nb):
        shifted = jnp.where(row == s * c, prev_ref[s], shifted)
        prev_ref[s] = p[(s + 1) * c - 1:(s + 1) * c, :]
    p = p + (shifted - p) * prm["mu"][...]
    yield

    r = p[:, 0:gw]
    k = p[:, gw:2 * gw]
    v = p[:, 2 * gw:3 * gw]
    o1 = 3 * gw
    w_lo = p[:, o1:o1 + RWKV_W_RANK]
    a_lo = p[:, o1 + RWKV_W_RANK:o1 + RWKV_W_RANK + RWKV_A_RANK]
    g_lo = p[:, o1 + RWKV_W_RANK + RWKV_A_RANK:]

    z = -(prm["w0"][...] + _mm(jnp.tanh(w_lo), prm["wup"][...]))
    softplus = jnp.maximum(z, 0.0) + jnp.log1p(jnp.exp(-jnp.abs(z)))
    lw = -jnp.exp(-softplus - 0.5)
    yield
    a = jax.nn.sigmoid(prm["a0"][...] + _mm(a_lo, prm["aup"][...]))
    g = _mm(jax.nn.sigmoid(g_lo), prm["gup"][...])
    yield

    kk = k * prm["kk"][...]
    kk = kk * lax.rsqrt(jnp.maximum(_head_sum(kk * kk, prm["bd"][...]), 1e-24))
    yield
    k = k * (1.0 + (a - 1.0) * prm["ka"][...])
    b = kk * a
    cum = _const_lhs_mm(prm["tri"][...], lw)
    yield
    cum_last = jnp.concatenate(
        [jnp.broadcast_to(cum[(s + 1) * c - 1:(s + 1) * c, :], (c, gw)) for s in range(nb)], axis=0)
    heads = lambda t, rows=c: jnp.stack(
        [t[s * c:s * c + rows, h * n:(h + 1) * n] for s in range(nb) for h in range(nh)], axis=0)
    out.update(r=r, k=k, v=v, g=g)
    e_neg = jnp.exp(-cum)
    out["bd_h"] = heads(b * e_neg)
    yield
    out["kd_h"] = heads(k * e_neg)
    yield
    out["kq_h"] = heads(kk * jnp.exp(cum - lw))
    yield
    out["rq_h"] = heads(r * jnp.exp(cum))
    yield
    out["v_h"] = heads(v)
    yield
    e_end = jnp.exp(cum_last - cum)
    out["be_h"] = heads(b * e_end)
    yield
    out["ke_h"] = heads(k * e_end)
    out["ge_h"] = heads(jnp.exp(cum_last), 1)
    yield


def _rwkv_solve(pp, s_ref, tick):
    kq_h, rq_h, bd_h, kd_h, v_h = pp["kq_h"], pp["rq_h"], pp["bd_h"], pp["kd_h"], pp["v_h"]
    c = kq_h.shape[1]
    ri = lax.broadcasted_iota(jnp.int32, (c, c), 0)
    ci = lax.broadcasted_iota(jnp.int32, (c, c), 1)
    strict = ri > ci
    incl = ri >= ci
    eye = (ri == ci).astype(F32)

    s0 = s_ref[...]
    gmat = _bmm(jnp.concatenate([kq_h, rq_h], axis=1), jnp.concatenate([bd_h, kd_h], axis=1), _BNT)
    tick()
    a_bb = jnp.where(strict, gmat[:, :c, :c], 0.0)
    a_bk = jnp.where(strict, gmat[:, :c, c:], 0.0)
    a_rb = jnp.where(incl, gmat[:, c:, :c], 0.0)
    a_rk = jnp.where(incl, gmat[:, c:, c:], 0.0)
    pw = -a_bb
    t_inv = eye + pw
    for _ in range(int(np.log2(c)) - 1):
        pw = _bmm(pw, pw)
        tick()
        t_inv = t_inv + _bmm(t_inv, pw)
        tick()
    x = _bmm(kq_h, s0, _BNT)
    tick()
    x = x + _bmm(a_bk, v_h)
    tick()
    u = -_bmm(t_inv, x)
    tick()
    y = _bmm(rq_h, s0, _BNT)
    tick()
    y = y + _bmm(a_rb, u)
    tick()
    y = y + _bmm(a_rk, v_h)
    tick()
    s_ref[...] = s0 * pp["ge_h"] + _bmm(jnp.concatenate([u, v_h], axis=1),
                                        jnp.concatenate([pp["be_h"], pp["ke_h"]], axis=1), _BTN)
    tick()
    return y


def _rwkv_epilogue(pp, y, prm, o_ref, osc_ref):
    nb, c, gw = o_ref.shape
    n = HEAD_DIM
    nh = gw // n
    ones_bd = prm["bd"][...]
    for s in range(nb):
        for h in range(nh):
            osc_ref[s * c:(s + 1) * c, h * n:(h + 1) * n] = y[s * nh + h]
    yield
    o = osc_ref[...]
    inv_n = 1.0 / n
    mean = _head_sum(o, ones_bd) * inv_n
    yield
    dlt = o - mean
    var = _head_sum(dlt * dlt, ones_bd) * inv_n
    yield
    o = dlt * lax.rsqrt(var + RWKV_LN_EPS) * prm["lnw"][...] + prm["lnb"][...]
    yield
    bonus = _head_sum(pp["r"] * pp["k"] * prm["rk"][...], ones_bd) * pp["v"]
    yield
    o_ref[...] = ((o + bonus) * pp["g"]).reshape(nb, c, gw).astype(o_ref.dtype)
    yield


def _full_spec(t):
    return pl.BlockSpec(t.shape, lambda i, j: (0,) * t.ndim)


def _mixer_call(parts, bsz, seq, nb, tblk, name):
    n_in = [len(p[2]) for p in parts]
    n_sc = [len(p[3]) for p in parts]

    def kern(*refs):
        ins, outs, scr = refs[:sum(n_in)], refs[sum(n_in):sum(n_in) + len(parts)], refs[sum(n_in) + len(parts):]

        @pl.when(pl.program_id(1) == 0)
        def _():
            for ref in scr:
                ref[...] = jnp.zeros_like(ref)

        i0 = s0 = 0
        for idx, part in enumerate(parts):
            part[0](*ins[i0:i0 + n_in[idx]], outs[idx], *scr[s0:s0 + n_sc[idx]])
            i0 += n_in[idx]
            s0 += n_sc[idx]

    out_spec = pl.BlockSpec((nb, tblk, GROUP_WIDTH), lambda i, j: (i, j, 0))
    return pl.pallas_call(
        kern,
        grid=(bsz // nb, seq // tblk),
        in_specs=[s for p in parts for s in p[1]],
        out_specs=[out_spec] * len(parts),
        out_shape=[jax.ShapeDtypeStruct((bsz, seq, GROUP_WIDTH), BF16)] * len(parts),
        scratch_shapes=[s for p in parts for s in p[3]],
        compiler_params=pltpu.CompilerParams(
            dimension_semantics=("parallel", "arbitrary"), vmem_limit_bytes=VMEM_LIMIT),
        name=name,
    )(*[a for p in parts for a in p[2]])


def _rwkv_part(p3, nb, tblk, mu, w0, w_up, a0, a_up, g_up, k_k, k_a, r_k, ln_w, ln_b):
    c = RWKV_CHUNK
    gw = GROUP_WIDTH
    tri = jnp.asarray(np.kron(np.eye(nb, dtype=np.float32), np.tril(np.ones((c, c), np.float32))), BF16)
    ones_bd = jnp.asarray(np.kron(np.eye(4, dtype=np.float32), np.ones((HEAD_DIM, HEAD_DIM), np.float32)), BF16)
    row = lambda t: t.reshape(1, -1)
    params = [row(mu), row(w0), w_up, row(a0), a_up, g_up, row(k_k), row(k_a), row(r_k), row(ln_w),
              row(ln_b), tri, ones_bd]
    in_specs = [pl.BlockSpec((nb, tblk, RWKV_IN), lambda i, j: (i, j, 0))] + [_full_spec(t) for t in params]
    scratch = [pltpu.VMEM((nb * gw // HEAD_DIM, HEAD_DIM, HEAD_DIM), F32),
               pltpu.VMEM((nb, 1, RWKV_IN), F32),
               pltpu.VMEM((nb * c, gw), F32)]
    return _rwkv_kernel, in_specs, [p3] + params, scratch


def _rwkv(p3, *params):
    bsz, seq, _ = p3.shape
    nb = MIXER_SEQS if bsz % MIXER_SEQS == 0 else 1
    tblk = min(RWKV_BLOCK, seq)
    return _mixer_call([_rwkv_part(p3, nb, tblk, *params)], bsz, seq, nb, tblk, "rwkv7")[0]


def _rope_table_kernel(pos_ref, invf_ref, cos_ref, sin_ref):
    ang = pos_ref[...].astype(F32) * invf_ref[...]
    cos_ref[...] = jnp.cos(ang)
    sin_ref[...] = jnp.sin(ang)


def _rope_tables(positions):
    bsz, seq = positions.shape
    d = HEAD_DIM
    w = 2 * d
    inv_freq = ROPE_THETA ** (-jnp.arange(0, ROPE_DIM, 2, dtype=F32) / ROPE_DIM)
    lane_d = np.arange(w) % d
    invf = jnp.where(lane_d < ROPE_DIM, inv_freq[lane_d % (ROPE_DIM // 2)], 0.0).reshape(1, w)
    tr = min(1024, bsz * seq)
    return pl.pallas_call(
        _rope_table_kernel,
        grid=(bsz * seq // tr,),
        in_specs=[pl.BlockSpec((tr, 1), lambda i: (i, 0)), pl.BlockSpec((1, w), lambda i: (0, 0))],
        out_specs=[pl.BlockSpec((tr, w), lambda i: (i, 0))] * 2,
        out_shape=[jax.ShapeDtypeStruct((bsz * seq, w), F32)] * 2,
        compiler_params=pltpu.CompilerParams(dimension_semantics=("parallel",)),
        name="rope_tables",
    )(positions.reshape(bsz * seq, 1), invf)


def _attn_kernel(cos_ref, sin_ref, q0_ref, q1_ref, kv_ref, qn_ref, kn_ref, sink_ref, bd_ref,
                 o_ref, kprev_ref, vprev_ref):
    nb, blk, _ = q0_ref.shape
    rows = nb * blk
    d = HEAD_DIM
    half = ROPE_DIM // 2
    group = ATTN_HEADS // ATTN_KV_HEADS
    kvw = ATTN_KV_HEADS * d
    first = pl.program_id(1) == 0
    ones_bd = bd_ref[...]
    cos = cos_ref[...].reshape(rows, 2 * d)
    sin = sin_ref[...].reshape(rows, 2 * d)

    def norm_rope(x, gain):
        w = x.shape[1]
        x = x * lax.rsqrt(_head_sum(x * x, ones_bd[:w, :w]) * (1.0 / d) + NORM_EPS) * gain
        ld = lax.broadcasted_iota(jnp.int32, (rows, w), 1) % d
        rot = jnp.where(ld < half, -pltpu.roll(x, w - half, axis=1),
                        jnp.where(ld < ROPE_DIM, pltpu.roll(x, half, axis=1), 0.0))
        tile = lambda t: t if w == t.shape[1] else jnp.concatenate([t] * (w // t.shape[1]), axis=1)
        return x * tile(cos) + rot * tile(sin)

    qs = [norm_rope(q_ref[...].astype(F32).reshape(rows, ATTN_COL_BLOCK), qn_ref[...]) * (d ** -0.5)
          for q_ref in (q0_ref, q1_ref)]
    kv = kv_ref[...].astype(F32).reshape(rows, ATTN_COL_BLOCK)
    k_cur = norm_rope(kv[:, :kvw], kn_ref[:, :kvw])
    v_cur = kv[:, kvw:]
    k_prev = kprev_ref[...]
    v_prev = vprev_ref[...]

    heads_per_ref = ATTN_COL_BLOCK // d
    q_head = lambda s, qh: qs[qh // heads_per_ref][s * blk:(s + 1) * blk,
                                                   (qh % heads_per_ref) * d:(qh % heads_per_ref + 1) * d]
    q_g = jnp.stack([jnp.concatenate([q_head(s, kh * group + i) for i in range(group)], axis=0)
                     for s in range(nb) for kh in range(ATTN_KV_HEADS)], axis=0)
    kv_heads = lambda t: jnp.stack([t[s * blk:(s + 1) * blk, kh * d:(kh + 1) * d]
                                    for s in range(nb) for kh in range(ATTN_KV_HEADS)], axis=0)

    ri = lax.broadcasted_iota(jnp.int32, (group * blk, blk), 0) % blk
    ci = lax.broadcasted_iota(jnp.int32, (group * blk, blk), 1)
    neg_inf = -jnp.inf
    no_prev = jnp.where(first, neg_inf, 0.0)
    s_prev = jnp.where(ci > ri, _bmm(q_g, kv_heads(k_prev), _BNT), neg_inf) + no_prev
    s_cur = jnp.where(ci <= ri, _bmm(q_g, kv_heads(k_cur), _BNT), neg_inf)
    sink = sink_ref[...]
    m = jnp.maximum(jnp.maximum(jnp.max(s_prev, axis=2, keepdims=True),
                                jnp.max(s_cur, axis=2, keepdims=True)), sink)
    p_prev = jnp.exp(s_prev - m)
    p_cur = jnp.exp(s_cur - m)
    ones = jnp.ones((nb * ATTN_KV_HEADS, blk, d), BF16)
    den = _bmm(p_prev, ones) + _bmm(p_cur, ones) + jnp.exp(sink - m)
    o = ((_bmm(p_prev, kv_heads(v_prev)) + _bmm(p_cur, kv_heads(v_cur))) / den).astype(o_ref.dtype)
    for s in range(nb):
        for qh in range(ATTN_HEADS):
            o_ref[s, :, qh * d:(qh + 1) * d] = o[s * ATTN_KV_HEADS + qh // group,
                                                 (qh % group) * blk:(qh % group + 1) * blk, :]

    kprev_ref[...] = k_cur
    vprev_ref[...] = v_cur


def _attn_part(p3, nb, rope_cos, rope_sin, q_norm, k_norm, sinks):
    bsz, seq, _ = p3.shape
    blk = ATTN_BLOCK
    cw = ATTN_COL_BLOCK
    d = HEAD_DIM
    qn = jnp.tile(q_norm.astype(F32), cw // d).reshape(1, cw)
    kn = jnp.tile(k_norm.astype(F32), cw // d).reshape(1, cw)
    ones_bd = jnp.asarray(np.kron(np.eye(cw // d, dtype=np.float32), np.ones((d, d), np.float32)), BF16)
    group = ATTN_HEADS // ATTN_KV_HEADS
    sink_col = jnp.repeat(sinks.astype(F32).reshape(ATTN_KV_HEADS, group), blk, axis=1)[:, :, None]
    sink_col = jnp.tile(sink_col, (nb, 1, 1))
    tw = rope_cos.shape[-1]
    cos3 = rope_cos.reshape(bsz, seq, tw)
    sin3 = rope_sin.reshape(bsz, seq, tw)
    col = lambda o: pl.BlockSpec((nb, blk, cw), lambda i, j: (i, j, ATTN_OFF_BLOCKS + o))
    trig = pl.BlockSpec((nb, blk, tw), lambda i, j: (i, j, 0))
    consts = [qn, kn, sink_col, ones_bd]
    in_specs = [trig, trig, col(0), col(1), col(2)] + [_full_spec(t) for t in consts]
    scratch = [pltpu.VMEM((nb * blk, ATTN_KV_HEADS * d), F32), pltpu.VMEM((nb * blk, ATTN_KV_HEADS * d), F32)]
    return _attn_kernel, in_specs, [cos3, sin3, p3, p3, p3] + consts, scratch


def _attn(p3, *params):
    bsz, seq, _ = p3.shape
    nb = MIXER_SEQS if bsz % MIXER_SEQS == 0 else 1
    return _mixer_call([_attn_part(p3, nb, *params)], bsz, seq, nb, ATTN_BLOCK, "swa_attn")[0]


def _s5_prep_kernel(lre_ref, lim_ref, ls_ref, bre_ref, bim_ref, are_ref, aim_ref, obre_ref, obim_ref):
    lre = lre_ref[...]
    lim = lim_ref[...]
    dt = jnp.exp(ls_ref[...])
    mag = jnp.exp(lre * dt)
    are = mag * jnp.cos(lim * dt)
    aim = mag * jnp.sin(lim * dt)
    are_ref[...] = are
    aim_ref[...] = aim
    inv = 1.0 / (lre * lre + lim * lim)
    cre = ((are - 1.0) * lre + aim * lim) * inv
    cim = (aim * lre - (are - 1.0) * lim) * inv
    bre = bre_ref[...]
    bim = bim_ref[...]
    obre_ref[...] = cre[:, None, :] * bre - cim[:, None, :] * bim
    obim_ref[...] = cre[:, None, :] * bim + cim[:, None, :] * bre


def _s5_kernel(u_ref, are_ref, aim_ref, b_ref, c_ref, d_ref, gw_ref, gb_ref, o_ref, x_ref, s_ref):
    bsz, tc, ch = u_ref.shape
    hw = S5_WIDTH // 2
    hc = ch // 2

    @pl.when(pl.program_id(0) == 0)
    def _():
        s_ref[...] = jnp.zeros_like(s_ref)

    u = jnp.swapaxes(u_ref[...].astype(F32), 0, 1).reshape(tc * bsz, ch)
    for hf in range(2):
        x_ref[:, 2 * hw * hf:2 * hw * (hf + 1)] = _mm(u[:, hc * hf:hc * (hf + 1)], b_ref[hf])

    for hf in range(2):
        for s0 in range(0, hw, S5_STRIP):
            re = slice(2 * hw * hf + s0, 2 * hw * hf + s0 + S5_STRIP)
            im = slice(2 * hw * hf + hw + s0, 2 * hw * hf + hw + s0 + S5_STRIP)
            lam = slice(hw * hf + s0, hw * hf + s0 + S5_STRIP)
            ar = jnp.broadcast_to(are_ref[:, lam], (bsz, S5_STRIP))
            ai = jnp.broadcast_to(aim_ref[:, lam], (bsz, S5_STRIP))

            def step(t, carry, re=re, im=im, ar=ar, ai=ai):
                sr, si = carry
                rows = pl.ds(pl.multiple_of(t * bsz, bsz), bsz)
                nr = ar * sr - ai * si + x_ref[rows, re]
                ni = ar * si + ai * sr + x_ref[rows, im]
                x_ref[rows, re] = nr
                x_ref[rows, im] = ni
                return nr, ni

            sr, si = lax.fori_loop(0, tc, step, (s_ref[:, re], s_ref[:, im]), unroll=4)
            s_ref[:, re] = sr
            s_ref[:, im] = si

    y = jnp.concatenate([_mm(x_ref[:, 2 * hw * hf:2 * hw * (hf + 1)], c_ref[hf]) for hf in range(2)], axis=1)
    y = y + d_ref[...] * u
    z = 0.5 * y * (1.0 + lax.erf(y * (2.0 ** -0.5)))
    out = z * jax.nn.sigmoid(_mm(z, gw_ref[...]) + gb_ref[...])
    o_ref[...] = jnp.swapaxes(out.reshape(tc, bsz, ch), 0, 1).astype(o_ref.dtype)


def _s5(p3, lam_re, lam_im, log_step, b_re, b_im, c_re, c_im, d_skip, glu_w, glu_b):
    bsz, seq, _ = p3.shape
    g, st, ch = S5_GROUPS, S5_STATE, S5_GROUP
    gwd = GROUP_WIDTH
    vm = pl.BlockSpec(memory_space=pltpu.VMEM)
    a_re, a_im, bb_re, bb_im = pl.pallas_call(
        _s5_prep_kernel,
        in_specs=[vm] * 5,
        out_specs=[vm] * 4,
        out_shape=[jax.ShapeDtypeStruct((g, st), F32)] * 2 + [jax.ShapeDtypeStruct((g, ch, st), F32)] * 2,
        name="s5_prep",
    )(lam_re, lam_im, log_step.reshape(g, 1), jnp.swapaxes(b_re, 1, 2), jnp.swapaxes(b_im, 1, 2))

    gh = g // 2
    eye = jnp.eye(gh, dtype=F32)
    bd_in = lambda t: (t[:, :, None, :] * eye[:, None, :, None]).reshape(gh * ch, gh * st)
    bd_out = lambda t: (jnp.swapaxes(t, 1, 2)[:, :, None, :] * eye[:, None, :, None]).reshape(gh * st, gh * ch)
    halves = lambda t: (t[:gh], t[gh:])
    b_mat = jnp.stack([jnp.concatenate([bd_in(r), bd_in(i)], axis=1)
                       for r, i in zip(halves(bb_re), halves(bb_im))]).astype(BF16)
    c_mat = jnp.stack([jnp.concatenate([bd_out(r), -bd_out(i)], axis=0)
                       for r, i in zip(halves(c_re.astype(F32)), halves(c_im.astype(F32)))]).astype(BF16)

    tc = min(64, seq)
    full = lambda t: pl.BlockSpec(t.shape, lambda i: (0,) * t.ndim)
    params = [a_re.reshape(1, S5_WIDTH), a_im.reshape(1, S5_WIDTH), b_mat, c_mat,
              d_skip.reshape(1, gwd), glu_w.astype(BF16), glu_b.reshape(1, gwd)]
    return pl.pallas_call(
        _s5_kernel,
        grid=(seq // tc,),
        in_specs=[pl.BlockSpec((bsz, tc, gwd), lambda i: (0, i, S5_OFF_BLOCKS))] + [full(t) for t in params],
        out_specs=pl.BlockSpec((bsz, tc, gwd), lambda i: (0, i, 0)),
        out_shape=jax.ShapeDtypeStruct((bsz, seq, gwd), BF16),
        scratch_shapes=[pltpu.VMEM((tc * bsz, 2 * S5_WIDTH), F32), pltpu.VMEM((bsz, 2 * S5_WIDTH), F32)],
        compiler_params=pltpu.CompilerParams(
            dimension_semantics=("arbitrary",), vmem_limit_bytes=VMEM_LIMIT),
        name="s5_scan",
    )(p3, *params)


def _hgrn_kernel(qf_ref, ig_ref, lbraw_ref, gn_ref, tri_ref, bd_ref, o_ref, s_ref, osc_ref, *, layer):
    nb, tb, _ = qf_ref.shape
    cz = HGRN_CHUNK
    gw = GROUP_WIDTH
    n = HEAD_DIM
    nh = gw // n

    qf = qf_ref[...].astype(F32).reshape(nb * tb, 2 * gw)
    ig = ig_ref[...].astype(F32).reshape(nb * tb, 2 * gw)
    q, f = qf[:, :gw], qf[:, gw:]
    v, g = ig[:, :gw], ig[:, gw:]

    lbr = lbraw_ref[...]
    e = jnp.exp(lbr - jnp.max(lbr, axis=0, keepdims=True))
    sm = e / jnp.sum(e, axis=0, keepdims=True)
    lb = jnp.zeros((1, gw), F32)
    for i in range(1, layer + 1):
        lb = lb + sm[i:i + 1, :]

    q = _silu(q) * (n ** -0.5)
    f_gate = lb + (1.0 - lb) * jax.nn.sigmoid(f)
    log_f = jnp.log(f_gate)
    k = 1.0 - f_gate

    parts = _split2(log_f)

    def per_seq(c_ref):
        cm = c_ref[...]
        seq_rows = lambda t, s: t[s * tb:(s + 1) * tb, :]
        return jnp.concatenate(
            [_dot(cm, seq_rows(parts[0], s), _NN) + _dot(cm, seq_rows(parts[1], s), _NN)
             for s in range(nb)], axis=0)

    bcum = per_seq(tri_ref)
    b3 = bcum.reshape(nb * tb // cz, cz, gw)
    chunk_row = lambda r: jnp.broadcast_to(b3[:, r:r + 1, :], b3.shape).reshape(nb * tb, gw)
    bmid = chunk_row(cz // 2 - 1)
    blast = chunk_row(cz - 1)
    qe = q * jnp.exp(bcum - bmid)
    ke = k * jnp.exp(bmid - bcum)
    kl = k * jnp.exp(blast - bcum)
    qb = q * jnp.exp(bcum)
    dec = jnp.exp(blast)

    ri = lax.broadcasted_iota(jnp.int32, (tb, tb), 0)
    ci = lax.broadcasted_iota(jnp.int32, (tb, tb), 1)
    mask = jnp.logical_and(ri // cz == ci // cz, ri >= ci)

    heads = lambda t: jnp.stack(
        [t[s * tb:(s + 1) * tb, h * n:(h + 1) * n] for s in range(nb) for h in range(nh)], axis=0)
    v_h, qb_h, kl_h, dec_h = heads(v), heads(qb), heads(kl), heads(dec)
    att = jnp.where(mask, _bmm(heads(qe), heads(ke), _BNT), 0.0)
    o_intra = _bmm(att, v_h)
    nchunk = tb // cz
    ti = lax.broadcasted_iota(jnp.int32, (tb, nchunk * n), 0)
    li = lax.broadcasted_iota(jnp.int32, (tb, nchunk * n), 1)
    kl_spread = jnp.where(ti // cz == li // n, jnp.concatenate([kl_h] * nchunk, axis=2), 0.0)
    kv_all = _bmm(v_h, kl_spread, _BTN)
    st = s_ref[...]
    o_inter = []
    for j in range(nchunk):
        rows = slice(j * cz, (j + 1) * cz)
        o_inter.append(_bmm(qb_h[:, rows, :], st, _BNT))
        st = st * dec_h[:, j * cz:j * cz + 1, :] + kv_all[:, :, j * n:(j + 1) * n]
    s_ref[...] = st
    o_heads = o_intra + jnp.concatenate(o_inter, axis=1)
    for s in range(nb):
        for h in range(nh):
            osc_ref[s * tb:(s + 1) * tb, h * n:(h + 1) * n] = o_heads[s * nh + h]

    o = osc_ref[...]
    ms = _head_sum(o * o, bd_ref[...]) * (1.0 / n)
    o_ref[...] = (o * lax.rsqrt(ms + NORM_EPS) * gn_ref[...] * _silu(g)).reshape(nb, tb, gw).astype(o_ref.dtype)


def _hgrn_part(p3, nb, lower_bounds, g_norm, layer):
    tb = HGRN_TILE
    cz = HGRN_CHUNK
    gw = GROUP_WIDTH
    idx = np.arange(tb)
    same = (idx[:, None] // cz) == (idx[None, :] // cz)
    tri = same & (idx[:, None] >= idx[None, :])
    consts = [jnp.asarray(tri.astype(np.float32), BF16)]
    ones_bd = jnp.asarray(np.kron(np.eye(4, dtype=np.float32), np.ones((HEAD_DIM, HEAD_DIM), np.float32)), BF16)
    gn = jnp.tile(g_norm.astype(F32), gw // HEAD_DIM).reshape(1, gw)
    params = [lower_bounds.astype(F32), gn] + consts + [ones_bd]
    col = lambda o: pl.BlockSpec((nb, tb, 2 * gw), lambda i, j: (i, j, HGRN_OFF_BLOCKS + o))
    in_specs = [col(0), col(1)] + [_full_spec(t) for t in params]
    scratch = [pltpu.VMEM((nb * gw // HEAD_DIM, HEAD_DIM, HEAD_DIM), F32), pltpu.VMEM((nb * tb, gw), F32)]
    return functools.partial(_hgrn_kernel, layer=layer), in_specs, [p3, p3] + params, scratch


def _hgrn(p3, *params):
    bsz, seq, _ = p3.shape
    nb = MIXER_SEQS if bsz % MIXER_SEQS == 0 else 1
    return _mixer_call([_hgrn_part(p3, nb, *params)], bsz, seq, nb, HGRN_TILE, "hgrn2")[0]


def kernel(x, positions, ffn1_norm, ffn1_w_gate, ffn1_w_up, ffn1_w_down, mix_norm, w_in, rwkv_mu, rwkv_w0, rwkv_w_up, rwkv_a0, rwkv_a_up, rwkv_g_up, rwkv_k_k, rwkv_k_a, rwkv_r_k, rwkv_ln_w, rwkv_ln_b, attn_q_norm, attn_k_norm, attn_sinks, s5_lambda_re, s5_lambda_im, s5_log_step, s5_b_re, s5_b_im, s5_c_re, s5_c_im, s5_d, s5_glu_w, s5_glu_b, hgrn_lower_bounds, hgrn_g_norm, w_out, ffn2_norm, ffn2_w_gate, ffn2_w_up, ffn2_w_down):
    bsz, seq, d = x.shape
    depth = w_in.shape[0]
    n = bsz * seq
    xf = x.reshape(n, d)
    rope_cos, rope_sin = _rope_tables(positions)
    ffn_w32 = []
    for l in range(depth):
        ffn_w32.append((ffn1_w_gate[l], ffn1_w_up[l], ffn1_w_down[l]))
        ffn_w32.append((ffn2_w_gate[l], ffn2_w_up[l], ffn2_w_down[l]))
    ffn_w32.append(())
    w16 = [w.astype(BF16) for w in ffn_w32[0]]
    for l in range(depth):
        xf, w16_next = _ffn(xf, ffn1_norm[l], *w16, riders=ffn_w32[2 * l + 1])
        p3 = _proj(xf, mix_norm[l], w_in[l].astype(BF16)).reshape(bsz, seq, D_IN)
        y_a = _rwkv(p3, rwkv_mu[l], rwkv_w0[l], rwkv_w_up[l], rwkv_a0[l], rwkv_a_up[l], rwkv_g_up[l],
                    rwkv_k_k[l], rwkv_k_a[l], rwkv_r_k[l].reshape(-1), rwkv_ln_w[l], rwkv_ln_b[l])
        y_b = _attn(p3, rope_cos, rope_sin, attn_q_norm[l], attn_k_norm[l], attn_sinks[l])
        y_c = _s5(p3, s5_lambda_re[l], s5_lambda_im[l], s5_log_step[l], s5_b_re[l], s5_b_im[l],
                  s5_c_re[l], s5_c_im[l], s5_d[l], s5_glu_w[l], s5_glu_b[l])
        y_d = _hgrn(p3, hgrn_lower_bounds, hgrn_g_norm[l], l)
        ys = [t.reshape(n, GROUP_WIDTH) for t in (y_a, y_b, y_c, y_d)]
        xf = _outproj(xf, ys, w_out[l].astype(BF16))
        xf, w16 = _ffn(xf, ffn2_norm[l], *w16_next, riders=ffn_w32[2 * l + 2])
    return xf.reshape(bsz, seq, d)
```

```python
import functools

import jax
import jax.numpy as jnp
import numpy as np
from jax import lax
from jax.experimental import pallas as pl
from jax.experimental.pallas import tpu as pltpu

F32 = jnp.float32
BF16 = jnp.bfloat16

HEAD_DIM = 64
GROUP_WIDTH = 512
NORM_EPS = 1e-6
FFN_RES_WEIGHT = 0.5

RWKV_W_RANK = 64
RWKV_A_RANK = 64
RWKV_G_RANK = 128
RWKV_LN_EPS = 64e-5
RWKV_IN = 3 * GROUP_WIDTH + RWKV_W_RANK + RWKV_A_RANK + RWKV_G_RANK
RWKV_CHUNK = 64
RWKV_BLOCK = 128
MIXER_SEQS = 4

ATTN_HEADS = 8
ATTN_KV_HEADS = 2
ATTN_BLOCK = 128
ROPE_THETA = 500000.0
ROPE_DIM = HEAD_DIM // 4
ATTN_COL_BLOCK = 256
ATTN_OFF_BLOCKS = RWKV_IN // ATTN_COL_BLOCK

S5_GROUP = 16
S5_GROUPS = GROUP_WIDTH // S5_GROUP
S5_STATE = 64
S5_WIDTH = S5_GROUPS * S5_STATE
S5_OFF_BLOCKS = (RWKV_IN + 768) // GROUP_WIDTH
S5_STRIP = 512

HGRN_CHUNK = 16
HGRN_TILE = 128
HGRN_OFF_BLOCKS = 3

D_IN = 5120
VMEM_LIMIT = 56 * 1024 * 1024


def _dot(a, b, dims):
    return lax.dot_general(a, b, (dims, ((), ())), preferred_element_type=F32)


_NN = ((1,), (0,))


def _mm(a, b):
    return _dot(a.astype(BF16), b.astype(BF16), _NN)


_BNN = (((2,), (1,)), ((0,), (0,)))
_BNT = (((2,), (2,)), ((0,), (0,)))
_BTN = (((1,), (1,)), ((0,), (0,)))


def _bmm(a, b, dims=_BNN):
    return lax.dot_general(a.astype(BF16), b.astype(BF16), dims, preferred_element_type=F32)


def _split2(x):
    hi = x.astype(BF16)
    lo = (x - hi.astype(F32)).astype(BF16)
    return hi, lo


def _const_lhs_mm(c, x):
    hi, lo = _split2(x)
    return _dot(c, hi, _NN) + _dot(c, lo, _NN)


def _const_rhs_mm(x, c):
    hi, lo = _split2(x)
    return _dot(hi, c, _NN) + _dot(lo, c, _NN)


def _head_sum(x, ones_bd):
    w = ones_bd.shape[0]
    parts = [_const_rhs_mm(x[:, i:i + w], ones_bd) for i in range(0, x.shape[1], w)]
    return parts[0] if len(parts) == 1 else jnp.concatenate(parts, axis=1)


def _silu(x):
    return x * jax.nn.sigmoid(x)


def _rms_rows(x, gain):
    ms = jnp.mean(x * x, axis=-1, keepdims=True)
    return x * lax.rsqrt(ms + NORM_EPS) * gain


def _ffn_kernel(x_ref, g_ref, wg_ref, wu_ref, wd_ref, *rest):
    n_riders = (len(rest) - 2) // 2
    o_ref, h_ref = rest[n_riders], rest[-1]

    @pl.when(pl.program_id(1) == 0)
    def _():
        x = x_ref[...]
        h_ref[...] = _rms_rows(x, g_ref[...]).astype(BF16)
        o_ref[...] = x

    h = h_ref[...]
    gate = jnp.dot(h, wg_ref[...], preferred_element_type=F32)
    up = jnp.dot(h, wu_ref[...], preferred_element_type=F32)
    act = (_silu(gate) * up).astype(BF16)
    o_ref[...] += FFN_RES_WEIGHT * jnp.dot(act, wd_ref[...], preferred_element_type=F32)
    for src, dst in zip(rest[:n_riders], rest[n_riders + 1:-1]):
        dst[...] = src[...].astype(BF16)


def _rider_specs(shape, layer, gi, gj):
    _, r, c = shape
    steps = gi * gj
    if r % steps == 0 and (r // steps) % 16 == 0:
        blk, pos = (r // steps, c), lambda i, j: (i * gj + j, 0)
    elif r % gi == 0 and (r // gi) % 16 == 0 and c % gj == 0 and (c // gj) % 128 == 0:
        blk, pos = (r // gi, c // gj), lambda i, j: (i, j)
    else:
        return None
    return pl.BlockSpec((None,) + blk, lambda i, j: (layer,) + pos(i, j)), pl.BlockSpec(blk, pos)


def _ffn(x2d, gain, wg, wu, wd, riders=()):
    n, d = x2d.shape
    f = wg.shape[1]
    tm = min(1024, n)
    tf = 512 if f % 512 == 0 else f
    gi, gj = n // tm, f // tf
    specs = [_rider_specs(w.shape, layer, gi, gj) for w, layer in riders]
    ride = [k for k, s in enumerate(specs) if s is not None]
    outs = pl.pallas_call(
        _ffn_kernel,
        grid=(gi, gj),
        in_specs=[
            pl.BlockSpec((tm, d), lambda i, j: (i, 0)),
            pl.BlockSpec((1, d), lambda i, j: (0, 0)),
            pl.BlockSpec((d, tf), lambda i, j: (0, j)),
            pl.BlockSpec((d, tf), lambda i, j: (0, j)),
            pl.BlockSpec((tf, d), lambda i, j: (j, 0)),
        ] + [specs[k][0] for k in ride],
        out_specs=[pl.BlockSpec((tm, d), lambda i, j: (i, 0))] + [specs[k][1] for k in ride],
        out_shape=[jax.ShapeDtypeStruct((n, d), F32)]
        + [jax.ShapeDtypeStruct(riders[k][0].shape[1:], BF16) for k in ride],
        scratch_shapes=[pltpu.VMEM((tm, d), BF16)],
        compiler_params=pltpu.CompilerParams(
            dimension_semantics=("parallel", "arbitrary"), vmem_limit_bytes=VMEM_LIMIT),
        name="ffn",
    )(x2d, gain.reshape(1, d), wg, wu, wd, *[riders[k][0] for k in ride])
    cast = [outs[1 + ride.index(k)] if k in ride else w[layer].astype(BF16)
            for k, (w, layer) in enumerate(riders)]
    return outs[0], cast


def _proj_kernel(x_ref, g_ref, w_ref, o_ref, h_ref):
    @pl.when(pl.program_id(1) == 0)
    def _():
        h_ref[...] = _rms_rows(x_ref[...], g_ref[...]).astype(BF16)

    o_ref[...] = jnp.dot(h_ref[...], w_ref[...], preferred_element_type=F32).astype(o_ref.dtype)


def _proj(x2d, gain, w):
    n, d = x2d.shape
    dout = w.shape[1]
    tm = min(1024, n)
    tn = dout // 4
    return pl.pallas_call(
        _proj_kernel,
        grid=(n // tm, dout // tn),
        in_specs=[
            pl.BlockSpec((tm, d), lambda i, j: (i, 0)),
            pl.BlockSpec((1, d), lambda i, j: (0, 0)),
            pl.BlockSpec((d, tn), lambda i, j: (0, j)),
        ],
        out_specs=pl.BlockSpec((tm, tn), lambda i, j: (i, j)),
        out_shape=jax.ShapeDtypeStruct((n, dout), BF16),
        scratch_shapes=[pltpu.VMEM((tm, d), BF16)],
        compiler_params=pltpu.CompilerParams(
            dimension_semantics=("parallel", "arbitrary"), vmem_limit_bytes=VMEM_LIMIT),
        name="in_proj",
    )(x2d, gain.reshape(1, d), w)


def _outproj_kernel(x_ref, ya_ref, yb_ref, yc_ref, yd_ref, w_ref, o_ref):
    gw = GROUP_WIDTH
    acc = x_ref[...]
    for m, y_ref in enumerate((ya_ref, yb_ref, yc_ref, yd_ref)):
        acc = acc + jnp.dot(y_ref[...].astype(BF16), w_ref[m * gw:(m + 1) * gw, :],
                            preferred_element_type=F32)
    o_ref[...] = acc


def _outproj(x2d, ys, w):
    n, d = x2d.shape
    tm = min(512, n)
    yspec = pl.BlockSpec((tm, GROUP_WIDTH), lambda i: (i, 0))
    return pl.pallas_call(
        _outproj_kernel,
        grid=(n // tm,),
        in_specs=[pl.BlockSpec((tm, d), lambda i: (i, 0)), yspec, yspec, yspec, yspec,
                  pl.BlockSpec(w.shape, lambda i: (0, 0))],
        out_specs=pl.BlockSpec((tm, d), lambda i: (i, 0)),
        out_shape=jax.ShapeDtypeStruct((n, d), F32),
        compiler_params=pltpu.CompilerParams(
            dimension_semantics=("parallel",), vmem_limit_bytes=VMEM_LIMIT),
        name="out_proj",
    )(x2d, *ys, w)


def _advance(fillers):
    for gen in fillers:
        if next(gen, _DONE) is not _DONE:
            return


_DONE = object()


def _rwkv_kernel(p_ref, mu_ref, w0_ref, wup_ref, a0_ref, aup_ref, gup_ref, kk_ref, ka_ref, rk_ref,
                 lnw_ref, lnb_ref, tri_ref, bd_ref, o_ref, s_ref, prev_ref, osc_ref):
    c = RWKV_CHUNK
    nck = p_ref.shape[1] // c
    prm = dict(mu=mu_ref, w0=w0_ref, wup=wup_ref, a0=a0_ref, aup=aup_ref, gup=gup_ref, kk=kk_ref, ka=ka_ref,
               rk=rk_ref, lnw=lnw_ref, lnb=lnb_ref, tri=tri_ref, bd=bd_ref)
    chunk = lambda ref, ck: ref.at[:, pl.ds(ck * c, c), :]
    preps = [dict() for _ in range(nck)]
    for _ in _rwkv_prologue(chunk(p_ref, 0), prm, prev_ref, preps[0]):
        pass
    epilogue = iter(())
    for ck in range(nck):
        fillers = [epilogue]
        if ck + 1 < nck:
            fillers.append(_rwkv_prologue(chunk(p_ref, ck + 1), prm, prev_ref, preps[ck + 1]))
        y = _rwkv_solve(preps[ck], s_ref, functools.partial(_advance, fillers))
        for gen in fillers:
            for _ in gen:
                pass
        epilogue = _rwkv_epilogue(preps[ck], y, prm, chunk(o_ref, ck), osc_ref)
    for _ in epilogue:
        pass


def _rwkv_prologue(p_ref, prm, prev_ref, out):
    nb, c, _ = p_ref.shape
    gw = GROUP_WIDTH
    n = HEAD_DIM
    nh = gw // n

    p = p_ref[...].astype(F32).reshape(nb * c, RWKV_IN)
    row = lax.broadcasted_iota(jnp.int32, p.shape, 0)
    shifted = pltpu.roll(p, 1, axis=0)
    for s in range(nb):
        shifted = jnp.where(row == s * c, prev_ref[s], shifted)
        prev_ref[s] = p[(s + 1) * c - 1:(s + 1) * c, :]
    p = p + (shifted - p) * prm["mu"][...]
    yield

    r = p[:, 0:gw]
    k = p[:, gw:2 * gw]
    v = p[:, 2 * gw:3 * gw]
    o1 = 3 * gw
    w_lo = p[:, o1:o1 + RWKV_W_RANK]
    a_lo = p[:, o1 + RWKV_W_RANK:o1 + RWKV_W_RANK + RWKV_A_RANK]
    g_lo = p[:, o1 + RWKV_W_RANK + RWKV_A_RANK:]

    z = -(prm["w0"][...] + _mm(jnp.tanh(w_lo), prm["wup"][...]))
    softplus = jnp.maximum(z, 0.0) + jnp.log1p(jnp.exp(-jnp.abs(z)))
    lw = -jnp.exp(-softplus - 0.5)
    yield
    a = jax.nn.sigmoid(prm["a0"][...] + _mm(a_lo, prm["aup"][...]))
    g = _mm(jax.nn.sigmoid(g_lo), prm["gup"][...])
    yield

    kk = k * prm["kk"][...]
    kk = kk * lax.rsqrt(jnp.maximum(_head_sum(kk * kk, prm["bd"][...]), 1e-24))
    yield
    k = k * (1.0 + (a - 1.0) * prm["ka"][...])
    b = kk * a
    cum = _const_lhs_mm(prm["tri"][...], lw)
    yield
    cum_last = jnp.concatenate(
        [jnp.broadcast_to(cum[(s + 1) * c - 1:(s + 1) * c, :], (c, gw)) for s in range(nb)], axis=0)
    heads = lambda t, rows=c: jnp.stack(
        [t[s * c:s * c + rows, h * n:(h + 1) * n] for s in range(nb) for h in range(nh)], axis=0)
    out.update(r=r, k=k, v=v, g=g)
    e_neg = jnp.exp(-cum)
    out["bd_h"] = heads(b * e_neg)
    yield
    out["kd_h"] = heads(k * e_neg)
    yield
    out["kq_h"] = heads(kk * jnp.exp(cum - lw))
    yield
    out["rq_h"] = heads(r * jnp.exp(cum))
    yield
    out["v_h"] = heads(v)
    yield
    e_end = jnp.exp(cum_last - cum)
    out["be_h"] = heads(b * e_end)
    yield
    out["ke_h"] = heads(k * e_end)
    out["ge_h"] = heads(jnp.exp(cum_last), 1)
    yield


def _rwkv_solve(pp, s_ref, tick):
    kq_h, rq_h, bd_h, kd_h, v_h = pp["kq_h"], pp["rq_h"], pp["bd_h"], pp["kd_h"], pp["v_h"]
    c = kq_h.shape[1]
    ri = lax.broadcasted_iota(jnp.int32, (c, c), 0)
    ci = lax.broadcasted_iota(jnp.int32, (c, c), 1)
    strict = ri > ci
    incl = ri >= ci
    eye = (ri == ci).astype(F32)

    s0 = s_ref[...]
    gmat = _bmm(jnp.concatenate([kq_h, rq_h], axis=1), jnp.concatenate([bd_h, kd_h], axis=1), _BNT)
    tick()
    a_bb = jnp.where(strict, gmat[:, :c, :c], 0.0)
    a_bk = jnp.where(strict, gmat[:, :c, c:], 0.0)
    a_rb = jnp.where(incl, gmat[:, c:, :c], 0.0)
    a_rk = jnp.where(incl, gmat[:, c:, c:], 0.0)
    pw = -a_bb
    t_inv = eye + pw
    for _ in range(int(np.log2(c)) - 1):
        pw = _bmm(pw, pw)
        tick()
        t_inv = t_inv + _bmm(t_inv, pw)
        tick()
    x = _bmm(kq_h, s0, _BNT)
    tick()
    x = x + _bmm(a_bk, v_h)
    tick()
    u = -_bmm(t_inv, x)
    tick()
    y = _bmm(rq_h, s0, _BNT)
    tick()
    y = y + _bmm(a_rb, u)
    tick()
    y = y + _bmm(a_rk, v_h)
    tick()
    s_ref[...] = s0 * pp["ge_h"] + _bmm(jnp.concatenate([u, v_h], axis=1),
                                        jnp.concatenate([pp["be_h"], pp["ke_h"]], axis=1), _BTN)
    tick()
    return y


def _rwkv_epilogue(pp, y, prm, o_ref, osc_ref):
    nb, c, gw = o_ref.shape
    n = HEAD_DIM
    nh = gw // n
    ones_bd = prm["bd"][...]
    for s in range(nb):
        for h in range(nh):
            osc_ref[s * c:(s + 1) * c, h * n:(h + 1) * n] = y[s * nh + h]
    yield
    o = osc_ref[...]
    inv_n = 1.0 / n
    mean = _head_sum(o, ones_bd) * inv_n
    yield
    dlt = o - mean
    var = _head_sum(dlt * dlt, ones_bd) * inv_n
    yield
    o = dlt * lax.rsqrt(var + RWKV_LN_EPS) * prm["lnw"][...] + prm["lnb"][...]
    yield
    bonus = _head_sum(pp["r"] * pp["k"] * prm["rk"][...], ones_bd) * pp["v"]
    yield
    o_ref[...] = ((o + bonus) * pp["g"]).reshape(nb, c, gw).astype(o_ref.dtype)
    yield


def _full_spec(t):
    return pl.BlockSpec(t.shape, lambda i, j: (0,) * t.ndim)


def _mixer_call(part, bsz, seq, nb, tblk, name, riders=()):
    body, in_specs, operands, scratch = part
    gi, gj = bsz // nb, seq // tblk
    specs = [_rider_specs(w.shape, layer, gi, gj) for w, layer in riders]
    ride = [k for k, s in enumerate(specs) if s is not None]
    n_in, n_ride = len(operands), len(ride)

    def kern(*refs):
        ins, rider_in = refs[:n_in], refs[n_in:n_in + n_ride]
        out, rider_out = refs[n_in + n_ride], refs[n_in + n_ride + 1:n_in + 2 * n_ride + 1]
        scr = refs[n_in + 2 * n_ride + 1:]

        @pl.when(pl.program_id(1) == 0)
        def _():
            for ref in scr:
                ref[...] = jnp.zeros_like(ref)

        body(*ins, out, *scr)
        for src, dst in zip(rider_in, rider_out):
            dst[...] = src[...].astype(BF16)

    out_spec = pl.BlockSpec((nb, tblk, GROUP_WIDTH), lambda i, j: (i, j, 0))
    outs = pl.pallas_call(
        kern,
        grid=(gi, gj),
        in_specs=list(in_specs) + [specs[k][0] for k in ride],
        out_specs=[out_spec] + [specs[k][1] for k in ride],
        out_shape=[jax.ShapeDtypeStruct((bsz, seq, GROUP_WIDTH), BF16)]
        + [jax.ShapeDtypeStruct(riders[k][0].shape[1:], BF16) for k in ride],
        scratch_shapes=scratch,
        compiler_params=pltpu.CompilerParams(
            dimension_semantics=("parallel", "arbitrary"), vmem_limit_bytes=VMEM_LIMIT),
        name=name,
    )(*operands, *[riders[k][0] for k in ride])
    cast = [outs[1 + ride.index(k)] if k in ride else w[layer].astype(BF16)
            for k, (w, layer) in enumerate(riders)]
    return outs[0], cast


def _rwkv_part(p3, nb, tblk, mu, w0, w_up, a0, a_up, g_up, k_k, k_a, r_k, ln_w, ln_b):
    c = RWKV_CHUNK
    gw = GROUP_WIDTH
    tri = jnp.asarray(np.kron(np.eye(nb, dtype=np.float32), np.tril(np.ones((c, c), np.float32))), BF16)
    ones_bd = jnp.asarray(np.kron(np.eye(4, dtype=np.float32), np.ones((HEAD_DIM, HEAD_DIM), np.float32)), BF16)
    row = lambda t: t.reshape(1, -1)
    params = [row(mu), row(w0), w_up, row(a0), a_up, g_up, row(k_k), row(k_a), row(r_k), row(ln_w),
              row(ln_b), tri, ones_bd]
    in_specs = [pl.BlockSpec((nb, tblk, RWKV_IN), lambda i, j: (i, j, 0))] + [_full_spec(t) for t in params]
    scratch = [pltpu.VMEM((nb * gw // HEAD_DIM, HEAD_DIM, HEAD_DIM), F32),
               pltpu.VMEM((nb, 1, RWKV_IN), F32),
               pltpu.VMEM((nb * c, gw), F32)]
    return _rwkv_kernel, in_specs, [p3] + params, scratch


def _rwkv(p3, *params, riders=()):
    bsz, seq, _ = p3.shape
    nb = MIXER_SEQS if bsz % MIXER_SEQS == 0 else 1
    tblk = min(RWKV_BLOCK, seq)
    return _mixer_call(_rwkv_part(p3, nb, tblk, *params), bsz, seq, nb, tblk, "rwkv7", riders)


def _rope_table_kernel(pos_ref, invf_ref, cos_ref, sin_ref):
    ang = pos_ref[...].astype(F32) * invf_ref[...]
    cos_ref[...] = jnp.cos(ang)
    sin_ref[...] = jnp.sin(ang)


def _rope_tables(positions):
    bsz, seq = positions.shape
    d = HEAD_DIM
    w = 2 * d
    inv_freq = ROPE_THETA ** (-jnp.arange(0, ROPE_DIM, 2, dtype=F32) / ROPE_DIM)
    lane_d = np.arange(w) % d
    invf = jnp.where(lane_d < ROPE_DIM, inv_freq[lane_d % (ROPE_DIM // 2)], 0.0).reshape(1, w)
    tr = min(1024, bsz * seq)
    return pl.pallas_call(
        _rope_table_kernel,
        grid=(bsz * seq // tr,),
        in_specs=[pl.BlockSpec((tr, 1), lambda i: (i, 0)), pl.BlockSpec((1, w), lambda i: (0, 0))],
        out_specs=[pl.BlockSpec((tr, w), lambda i: (i, 0))] * 2,
        out_shape=[jax.ShapeDtypeStruct((bsz * seq, w), F32)] * 2,
        compiler_params=pltpu.CompilerParams(dimension_semantics=("parallel",)),
        name="rope_tables",
    )(positions.reshape(bsz * seq, 1), invf)


def _attn_kernel(cos_ref, sin_ref, q0_ref, q1_ref, kv_ref, qn_ref, kn_ref, sink_ref, bd_ref,
                 o_ref, kprev_ref, vprev_ref):
    nb, blk, _ = q0_ref.shape
    rows = nb * blk
    d = HEAD_DIM
    half = ROPE_DIM // 2
    group = ATTN_HEADS // ATTN_KV_HEADS
    kvw = ATTN_KV_HEADS * d
    first = pl.program_id(1) == 0
    ones_bd = bd_ref[...]
    cos = cos_ref[...].reshape(rows, 2 * d)
    sin = sin_ref[...].reshape(rows, 2 * d)

    def norm_rope(x, gain):
        w = x.shape[1]
        x = x * lax.rsqrt(_head_sum(x * x, ones_bd[:w, :w]) * (1.0 / d) + NORM_EPS) * gain
        ld = lax.broadcasted_iota(jnp.int32, (rows, w), 1) % d
        rot = jnp.where(ld < half, -pltpu.roll(x, w - half, axis=1),
                        jnp.where(ld < ROPE_DIM, pltpu.roll(x, half, axis=1), 0.0))
        tile = lambda t: t if w == t.shape[1] else jnp.concatenate([t] * (w // t.shape[1]), axis=1)
        return x * tile(cos) + rot * tile(sin)

    qs = [norm_rope(q_ref[...].astype(F32).reshape(rows, ATTN_COL_BLOCK), qn_ref[...]) * (d ** -0.5)
          for q_ref in (q0_ref, q1_ref)]
    kv = kv_ref[...].astype(F32).reshape(rows, ATTN_COL_BLOCK)
    k_cur = norm_rope(kv[:, :kvw], kn_ref[:, :kvw])
    v_cur = kv[:, kvw:]
    k_prev = kprev_ref[...]
    v_prev = vprev_ref[...]

    heads_per_ref = ATTN_COL_BLOCK // d
    q_head = lambda s, qh: qs[qh // heads_per_ref][s * blk:(s + 1) * blk,
                                                   (qh % heads_per_ref) * d:(qh % heads_per_ref + 1) * d]
    q_g = jnp.stack([jnp.concatenate([q_head(s, kh * group + i) for i in range(group)], axis=0)
                     for s in range(nb) for kh in range(ATTN_KV_HEADS)], axis=0)
    kv_heads = lambda t: jnp.stack([t[s * blk:(s + 1) * blk, kh * d:(kh + 1) * d]
                                    for s in range(nb) for kh in range(ATTN_KV_HEADS)], axis=0)

    ri = lax.broadcasted_iota(jnp.int32, (group * blk, blk), 0) % blk
    ci = lax.broadcasted_iota(jnp.int32, (group * blk, blk), 1)
    neg_inf = -jnp.inf
    no_prev = jnp.where(first, neg_inf, 0.0)
    s_prev = jnp.where(ci > ri, _bmm(q_g, kv_heads(k_prev), _BNT), neg_inf) + no_prev
    s_cur = jnp.where(ci <= ri, _bmm(q_g, kv_heads(k_cur), _BNT), neg_inf)
    sink = sink_ref[...]
    m = jnp.maximum(jnp.maximum(jnp.max(s_prev, axis=2, keepdims=True),
                                jnp.max(s_cur, axis=2, keepdims=True)), sink)
    p_prev = jnp.exp(s_prev - m)
    p_cur = jnp.exp(s_cur - m)
    ones = jnp.ones((nb * ATTN_KV_HEADS, blk, d), BF16)
    den = _bmm(p_prev, ones) + _bmm(p_cur, ones) + jnp.exp(sink - m)
    o = ((_bmm(p_prev, kv_heads(v_prev)) + _bmm(p_cur, kv_heads(v_cur))) / den).astype(o_ref.dtype)
    for s in range(nb):
        for qh in range(ATTN_HEADS):
            o_ref[s, :, qh * d:(qh + 1) * d] = o[s * ATTN_KV_HEADS + qh // group,
                                                 (qh % group) * blk:(qh % group + 1) * blk, :]

    kprev_ref[...] = k_cur
    vprev_ref[...] = v_cur


def _attn_part(p3, nb, rope_cos, rope_sin, q_norm, k_norm, sinks):
    bsz, seq, _ = p3.shape
    blk = ATTN_BLOCK
    cw = ATTN_COL_BLOCK
    d = HEAD_DIM
    qn = jnp.tile(q_norm.astype(F32), cw // d).reshape(1, cw)
    kn = jnp.tile(k_norm.astype(F32), cw // d).reshape(1, cw)
    ones_bd = jnp.asarray(np.kron(np.eye(cw // d, dtype=np.float32), np.ones((d, d), np.float32)), BF16)
    group = ATTN_HEADS // ATTN_KV_HEADS
    sink_col = jnp.repeat(sinks.astype(F32).reshape(ATTN_KV_HEADS, group), blk, axis=1)[:, :, None]
    sink_col = jnp.tile(sink_col, (nb, 1, 1))
    tw = rope_cos.shape[-1]
    cos3 = rope_cos.reshape(bsz, seq, tw)
    sin3 = rope_sin.reshape(bsz, seq, tw)
    col = lambda o: pl.BlockSpec((nb, blk, cw), lambda i, j: (i, j, ATTN_OFF_BLOCKS + o))
    trig = pl.BlockSpec((nb, blk, tw), lambda i, j: (i, j, 0))
    consts = [qn, kn, sink_col, ones_bd]
    in_specs = [trig, trig, col(0), col(1), col(2)] + [_full_spec(t) for t in consts]
    scratch = [pltpu.VMEM((nb * blk, ATTN_KV_HEADS * d), F32), pltpu.VMEM((nb * blk, ATTN_KV_HEADS * d), F32)]
    return _attn_kernel, in_specs, [cos3, sin3, p3, p3, p3] + consts, scratch


def _attn(p3, *params, riders=()):
    bsz, seq, _ = p3.shape
    nb = MIXER_SEQS if bsz % MIXER_SEQS == 0 else 1
    return _mixer_call(_attn_part(p3, nb, *params), bsz, seq, nb, ATTN_BLOCK, "swa_attn", riders)


def _s5_prep_kernel(lre_ref, lim_ref, ls_ref, bre_ref, bim_ref, are_ref, aim_ref, obre_ref, obim_ref):
    lre = lre_ref[...]
    lim = lim_ref[...]
    dt = jnp.exp(ls_ref[...])
    mag = jnp.exp(lre * dt)
    are = mag * jnp.cos(lim * dt)
    aim = mag * jnp.sin(lim * dt)
    are_ref[...] = are
    aim_ref[...] = aim
    inv = 1.0 / (lre * lre + lim * lim)
    cre = ((are - 1.0) * lre + aim * lim) * inv
    cim = (aim * lre - (are - 1.0) * lim) * inv
    bre = bre_ref[...]
    bim = bim_ref[...]
    obre_ref[...] = cre[:, None, :] * bre - cim[:, None, :] * bim
    obim_ref[...] = cre[:, None, :] * bim + cim[:, None, :] * bre


def _s5_kernel(u_ref, are_ref, aim_ref, b_ref, c_ref, d_ref, gw_ref, gb_ref, o_ref, x_ref, s_ref):
    bsz, tc, ch = u_ref.shape
    hw = S5_WIDTH // 2
    hc = ch // 2

    @pl.when(pl.program_id(0) == 0)
    def _():
        s_ref[...] = jnp.zeros_like(s_ref)

    u = jnp.swapaxes(u_ref[...].astype(F32), 0, 1).reshape(tc * bsz, ch)
    for hf in range(2):
        x_ref[:, 2 * hw * hf:2 * hw * (hf + 1)] = _mm(u[:, hc * hf:hc * (hf + 1)], b_ref[hf])

    for hf in range(2):
        for s0 in range(0, hw, S5_STRIP):
            re = slice(2 * hw * hf + s0, 2 * hw * hf + s0 + S5_STRIP)
            im = slice(2 * hw * hf + hw + s0, 2 * hw * hf + hw + s0 + S5_STRIP)
            lam = slice(hw * hf + s0, hw * hf + s0 + S5_STRIP)
            ar = jnp.broadcast_to(are_ref[:, lam], (bsz, S5_STRIP))
            ai = jnp.broadcast_to(aim_ref[:, lam], (bsz, S5_STRIP))

            def step(t, carry, re=re, im=im, ar=ar, ai=ai):
                sr, si = carry
                rows = pl.ds(pl.multiple_of(t * bsz, bsz), bsz)
                nr = ar * sr - ai * si + x_ref[rows, re]
                ni = ar * si + ai * sr + x_ref[rows, im]
                x_ref[rows, re] = nr
                x_ref[rows, im] = ni
                return nr, ni

            sr, si = lax.fori_loop(0, tc, step, (s_ref[:, re], s_ref[:, im]), unroll=4)
            s_ref[:, re] = sr
            s_ref[:, im] = si

    y = jnp.concatenate([_mm(x_ref[:, 2 * hw * hf:2 * hw * (hf + 1)], c_ref[hf]) for hf in range(2)], axis=1)
    y = y + d_ref[...] * u
    z = 0.5 * y * (1.0 + lax.erf(y * (2.0 ** -0.5)))
    out = z * jax.nn.sigmoid(_mm(z, gw_ref[...]) + gb_ref[...])
    o_ref[...] = jnp.swapaxes(out.reshape(tc, bsz, ch), 0, 1).astype(o_ref.dtype)


def _s5(p3, lam_re, lam_im, log_step, b_re, b_im, c_re, c_im, d_skip, glu_w, glu_b):
    bsz, seq, _ = p3.shape
    g, st, ch = S5_GROUPS, S5_STATE, S5_GROUP
    gwd = GROUP_WIDTH
    vm = pl.BlockSpec(memory_space=pltpu.VMEM)
    a_re, a_im, bb_re, bb_im = pl.pallas_call(
        _s5_prep_kernel,
        in_specs=[vm] * 5,
        out_specs=[vm] * 4,
        out_shape=[jax.ShapeDtypeStruct((g, st), F32)] * 2 + [jax.ShapeDtypeStruct((g, ch, st), F32)] * 2,
        name="s5_prep",
    )(lam_re, lam_im, log_step.reshape(g, 1), jnp.swapaxes(b_re, 1, 2), jnp.swapaxes(b_im, 1, 2))

    gh = g // 2
    eye = jnp.eye(gh, dtype=F32)
    bd_in = lambda t: (t[:, :, None, :] * eye[:, None, :, None]).reshape(gh * ch, gh * st)
    bd_out = lambda t: (jnp.swapaxes(t, 1, 2)[:, :, None, :] * eye[:, None, :, None]).reshape(gh * st, gh * ch)
    halves = lambda t: (t[:gh], t[gh:])
    b_mat = jnp.stack([jnp.concatenate([bd_in(r), bd_in(i)], axis=1)
                       for r, i in zip(halves(bb_re), halves(bb_im))]).astype(BF16)
    c_mat = jnp.stack([jnp.concatenate([bd_out(r), -bd_out(i)], axis=0)
                       for r, i in zip(halves(c_re.astype(F32)), halves(c_im.astype(F32)))]).astype(BF16)

    tc = min(64, seq)
    full = lambda t: pl.BlockSpec(t.shape, lambda i: (0,) * t.ndim)
    params = [a_re.reshape(1, S5_WIDTH), a_im.reshape(1, S5_WIDTH), b_mat, c_mat,
              d_skip.reshape(1, gwd), glu_w.astype(BF16), glu_b.reshape(1, gwd)]
    return pl.pallas_call(
        _s5_kernel,
        grid=(seq // tc,),
        in_specs=[pl.BlockSpec((bsz, tc, gwd), lambda i: (0, i, S5_OFF_BLOCKS))] + [full(t) for t in params],
        out_specs=pl.BlockSpec((bsz, tc, gwd), lambda i: (0, i, 0)),
        out_shape=jax.ShapeDtypeStruct((bsz, seq, gwd), BF16),
        scratch_shapes=[pltpu.VMEM((tc * bsz, 2 * S5_WIDTH), F32), pltpu.VMEM((bsz, 2 * S5_WIDTH), F32)],
        compiler_params=pltpu.CompilerParams(
            dimension_semantics=("arbitrary",), vmem_limit_bytes=VMEM_LIMIT),
        name="s5_scan",
    )(p3, *params)


def _hgrn_kernel(qf_ref, ig_ref, lbraw_ref, gn_ref, tri_ref, bd_ref, o_ref, s_ref, osc_ref, *, layer):
    nb, tb, _ = qf_ref.shape
    cz = HGRN_CHUNK
    gw = GROUP_WIDTH
    n = HEAD_DIM
    nh = gw // n

    qf = qf_ref[...].astype(F32).reshape(nb * tb, 2 * gw)
    ig = ig_ref[...].astype(F32).reshape(nb * tb, 2 * gw)
    q, f = qf[:, :gw], qf[:, gw:]
    v, g = ig[:, :gw], ig[:, gw:]

    lbr = lbraw_ref[...]
    e = jnp.exp(lbr - jnp.max(lbr, axis=0, keepdims=True))
    sm = e / jnp.sum(e, axis=0, keepdims=True)
    lb = jnp.zeros((1, gw), F32)
    for i in range(1, layer + 1):
        lb = lb + sm[i:i + 1, :]

    q = _silu(q) * (n ** -0.5)
    f_gate = lb + (1.0 - lb) * jax.nn.sigmoid(f)
    log_f = jnp.log(f_gate)
    k = 1.0 - f_gate

    parts = _split2(log_f)

    def per_seq(c_ref):
        cm = c_ref[...]
        seq_rows = lambda t, s: t[s * tb:(s + 1) * tb, :]
        return jnp.concatenate(
            [_dot(cm, seq_rows(parts[0], s), _NN) + _dot(cm, seq_rows(parts[1], s), _NN)
             for s in range(nb)], axis=0)

    bcum = per_seq(tri_ref)
    b3 = bcum.reshape(nb * tb // cz, cz, gw)
    chunk_row = lambda r: jnp.broadcast_to(b3[:, r:r + 1, :], b3.shape).reshape(nb * tb, gw)
    bmid = chunk_row(cz // 2 - 1)
    blast = chunk_row(cz - 1)
    qe = q * jnp.exp(bcum - bmid)
    ke = k * jnp.exp(bmid - bcum)
    kl = k * jnp.exp(blast - bcum)
    qb = q * jnp.exp(bcum)
    dec = jnp.exp(blast)

    ri = lax.broadcasted_iota(jnp.int32, (tb, tb), 0)
    ci = lax.broadcasted_iota(jnp.int32, (tb, tb), 1)
    mask = jnp.logical_and(ri // cz == ci // cz, ri >= ci)

    heads = lambda t: jnp.stack(
        [t[s * tb:(s + 1) * tb, h * n:(h + 1) * n] for s in range(nb) for h in range(nh)], axis=0)
    v_h, qb_h, kl_h, dec_h = heads(v), heads(qb), heads(kl), heads(dec)
    att = jnp.where(mask, _bmm(heads(qe), heads(ke), _BNT), 0.0)
    o_intra = _bmm(att, v_h)
    nchunk = tb // cz
    ti = lax.broadcasted_iota(jnp.int32, (tb, nchunk * n), 0)
    li = lax.broadcasted_iota(jnp.int32, (tb, nchunk * n), 1)
    kl_spread = jnp.where(ti // cz == li // n, jnp.concatenate([kl_h] * nchunk, axis=2), 0.0)
    kv_all = _bmm(v_h, kl_spread, _BTN)
    st = s_ref[...]
    o_inter = []
    for j in range(nchunk):
        rows = slice(j * cz, (j + 1) * cz)
        o_inter.append(_bmm(qb_h[:, rows, :], st, _BNT))
        st = st * dec_h[:, j * cz:j * cz + 1, :] + kv_all[:, :, j * n:(j + 1) * n]
    s_ref[...] = st
    o_heads = o_intra + jnp.concatenate(o_inter, axis=1)
    for s in range(nb):
        for h in range(nh):
            osc_ref[s * tb:(s + 1) * tb, h * n:(h + 1) * n] = o_heads[s * nh + h]

    o = osc_ref[...]
    ms = _head_sum(o * o, bd_ref[...]) * (1.0 / n)
    o_ref[...] = (o * lax.rsqrt(ms + NORM_EPS) * gn_ref[...] * _silu(g)).reshape(nb, tb, gw).astype(o_ref.dtype)


def _hgrn_part(p3, nb, lower_bounds, g_norm, layer):
    tb = HGRN_TILE
    cz = HGRN_CHUNK
    gw = GROUP_WIDTH
    idx = np.arange(tb)
    same = (idx[:, None] // cz) == (idx[None, :] // cz)
    tri = same & (idx[:, None] >= idx[None, :])
    consts = [jnp.asarray(tri.astype(np.float32), BF16)]
    ones_bd = jnp.asarray(np.kron(np.eye(4, dtype=np.float32), np.ones((HEAD_DIM, HEAD_DIM), np.float32)), BF16)
    gn = jnp.tile(g_norm.astype(F32), gw // HEAD_DIM).reshape(1, gw)
    params = [lower_bounds.astype(F32), gn] + consts + [ones_bd]
    col = lambda o: pl.BlockSpec((nb, tb, 2 * gw), lambda i, j: (i, j, HGRN_OFF_BLOCKS + o))
    in_specs = [col(0), col(1)] + [_full_spec(t) for t in params]
    scratch = [pltpu.VMEM((nb * gw // HEAD_DIM, HEAD_DIM, HEAD_DIM), F32), pltpu.VMEM((nb * tb, gw), F32)]
    return functools.partial(_hgrn_kernel, layer=layer), in_specs, [p3, p3] + params, scratch


def _hgrn(p3, *params, riders=()):
    bsz, seq, _ = p3.shape
    nb = MIXER_SEQS if bsz % MIXER_SEQS == 0 else 1
    return _mixer_call(_hgrn_part(p3, nb, *params), bsz, seq, nb, HGRN_TILE, "hgrn2", riders)


def kernel(x, positions, ffn1_norm, ffn1_w_gate, ffn1_w_up, ffn1_w_down, mix_norm, w_in, rwkv_mu, rwkv_w0, rwkv_w_up, rwkv_a0, rwkv_a_up, rwkv_g_up, rwkv_k_k, rwkv_k_a, rwkv_r_k, rwkv_ln_w, rwkv_ln_b, attn_q_norm, attn_k_norm, attn_sinks, s5_lambda_re, s5_lambda_im, s5_log_step, s5_b_re, s5_b_im, s5_c_re, s5_c_im, s5_d, s5_glu_w, s5_glu_b, hgrn_lower_bounds, hgrn_g_norm, w_out, ffn2_norm, ffn2_w_gate, ffn2_w_up, ffn2_w_down):
    bsz, seq, d = x.shape
    depth = w_in.shape[0]
    n = bsz * seq
    xf = x.reshape(n, d)
    rope_cos, rope_sin = _rope_tables(positions)
    ffn_w32 = []
    for l in range(depth):
        ffn_w32.append([(w, l) for w in (ffn1_w_gate, ffn1_w_up, ffn1_w_down)])
        ffn_w32.append([(w, l) for w in (ffn2_w_gate, ffn2_w_up, ffn2_w_down)])
    ffn_w32.append([])
    w16 = [w[l].astype(BF16) for w, l in ffn_w32[0]]
    w_in16 = w_in[0].astype(BF16)
    for l in range(depth):
        xf, w16_next = _ffn(xf, ffn1_norm[l], *w16, riders=ffn_w32[2 * l + 1])
        p3 = _proj(xf, mix_norm[l], w_in16).reshape(bsz, seq, D_IN)
        y_a, (w_out16,) = _rwkv(p3, rwkv_mu[l], rwkv_w0[l], rwkv_w_up[l], rwkv_a0[l], rwkv_a_up[l], rwkv_g_up[l],
                                rwkv_k_k[l], rwkv_k_a[l], rwkv_r_k[l].reshape(-1), rwkv_ln_w[l], rwkv_ln_b[l],
                                riders=[(w_out, l)])
        y_b, _ = _attn(p3, rope_cos, rope_sin, attn_q_norm[l], attn_k_norm[l], attn_sinks[l])
        y_c = _s5(p3, s5_lambda_re[l], s5_lambda_im[l], s5_log_step[l], s5_b_re[l], s5_b_im[l],
                  s5_c_re[l], s5_c_im[l], s5_d[l], s5_glu_w[l], s5_glu_b[l])
        y_d, next_in = _hgrn(p3, hgrn_lower_bounds, hgrn_g_norm[l], l,
                             riders=[(w_in, l + 1)] if l + 1 < depth else [])
        if next_in:
            w_in16 = next_in[0]
        ys = [t.reshape(n, GROUP_WIDTH) for t in (y_a, y_b, y_c, y_d)]
        xf = _outproj(xf, ys, w_out16)
        xf, w16 = _ffn(xf, ffn2_norm[l], *w16_next, riders=ffn_w32[2 * l + 2])
    return xf.reshape(bsz, seq, d)
```

```python
import functools

import jax
import jax.numpy as jnp
import numpy as np
from jax import lax
from jax.experimental import pallas as pl
from jax.experimental.pallas import tpu as pltpu

F32 = jnp.float32
BF16 = jnp.bfloat16

HEAD_DIM = 64
GROUP_WIDTH = 512
NORM_EPS = 1e-6
FFN_RES_WEIGHT = 0.5

RWKV_W_RANK = 64
RWKV_A_RANK = 64
RWKV_G_RANK = 128
RWKV_LN_EPS = 64e-5
RWKV_IN = 3 * GROUP_WIDTH + RWKV_W_RANK + RWKV_A_RANK + RWKV_G_RANK
RWKV_CHUNK = 64
RWKV_BLOCK = 128
MIXER_SEQS = 4

ATTN_HEADS = 8
ATTN_KV_HEADS = 2
ATTN_BLOCK = 128
ATTN_SEQS = 4
ROPE_THETA = 500000.0
ROPE_DIM = HEAD_DIM // 4
ATTN_COL_BLOCK = 256
ATTN_OFF_BLOCKS = RWKV_IN // ATTN_COL_BLOCK

S5_GROUP = 16
S5_GROUPS = GROUP_WIDTH // S5_GROUP
S5_STATE = 64
S5_WIDTH = S5_GROUPS * S5_STATE
S5_OFF_BLOCKS = (RWKV_IN + 768) // GROUP_WIDTH
S5_STRIP = 512

HGRN_CHUNK = 16
HGRN_TILE = 128
HGRN_SEQS = 4
HGRN_OFF_BLOCKS = 3

D_IN = 5120
VMEM_LIMIT = 56 * 1024 * 1024


def _dot(a, b, dims):
    return lax.dot_general(a, b, (dims, ((), ())), preferred_element_type=F32)


_NN = ((1,), (0,))


def _mm(a, b):
    return _dot(a.astype(BF16), b.astype(BF16), _NN)


_BNN = (((2,), (1,)), ((0,), (0,)))
_BNT = (((2,), (2,)), ((0,), (0,)))
_BTN = (((1,), (1,)), ((0,), (0,)))


def _bmm(a, b, dims=_BNN):
    return lax.dot_general(a.astype(BF16), b.astype(BF16), dims, preferred_element_type=F32)


def _split2(x):
    hi = x.astype(BF16)
    lo = (x - hi.astype(F32)).astype(BF16)
    return hi, lo


def _const_lhs_mm(c, x):
    hi, lo = _split2(x)
    return _dot(c, hi, _NN) + _dot(c, lo, _NN)


def _const_rhs_mm(x, c):
    hi, lo = _split2(x)
    return _dot(hi, c, _NN) + _dot(lo, c, _NN)


def _head_sum(x, ones_bd):
    w = ones_bd.shape[0]
    parts = [_const_rhs_mm(x[:, i:i + w], ones_bd) for i in range(0, x.shape[1], w)]
    return parts[0] if len(parts) == 1 else jnp.concatenate(parts, axis=1)


def _silu(x):
    return x * jax.nn.sigmoid(x)


def _rms_rows(x, gain):
    ms = jnp.mean(x * x, axis=-1, keepdims=True)
    return x * lax.rsqrt(ms + NORM_EPS) * gain


def _ffn_kernel(x_ref, g_ref, wg_ref, wu_ref, wd_ref, *rest):
    n_riders = (len(rest) - 2) // 2
    o_ref, h_ref = rest[n_riders], rest[-1]

    @pl.when(pl.program_id(1) == 0)
    def _():
        x = x_ref[...]
        h_ref[...] = _rms_rows(x, g_ref[...]).astype(BF16)
        o_ref[...] = x

    h = h_ref[...]
    gate = jnp.dot(h, wg_ref[...], preferred_element_type=F32)
    up = jnp.dot(h, wu_ref[...], preferred_element_type=F32)
    act = (_silu(gate) * up).astype(BF16)
    o_ref[...] += FFN_RES_WEIGHT * jnp.dot(act, wd_ref[...], preferred_element_type=F32)
    for src, dst in zip(rest[:n_riders], rest[n_riders + 1:-1]):
        dst[...] = src[...].astype(BF16)


def _rider_specs(shape, layer, gi, gj):
    _, r, c = shape
    steps = gi * gj
    if r % steps == 0 and (r // steps) % 16 == 0:
        blk, pos = (r // steps, c), lambda i, j: (i * gj + j, 0)
    elif r % gi == 0 and (r // gi) % 16 == 0 and c % gj == 0 and (c // gj) % 128 == 0:
        blk, pos = (r // gi, c // gj), lambda i, j: (i, j)
    else:
        return None
    return pl.BlockSpec((None,) + blk, lambda i, j: (layer,) + pos(i, j)), pl.BlockSpec(blk, pos)


def _ffn(x2d, gain, wg, wu, wd, riders=()):
    n, d = x2d.shape
    f = wg.shape[1]
    tm = min(1024, n)
    tf = 512 if f % 512 == 0 else f
    gi, gj = n // tm, f // tf
    specs = [_rider_specs(w.shape, layer, gi, gj) for w, layer in riders]
    ride = [k for k, s in enumerate(specs) if s is not None]
    outs = pl.pallas_call(
        _ffn_kernel,
        grid=(gi, gj),
        in_specs=[
            pl.BlockSpec((tm, d), lambda i, j: (i, 0)),
            pl.BlockSpec((1, d), lambda i, j: (0, 0)),
            pl.BlockSpec((d, tf), lambda i, j: (0, j)),
            pl.BlockSpec((d, tf), lambda i, j: (0, j)),
            pl.BlockSpec((tf, d), lambda i, j: (j, 0)),
        ] + [specs[k][0] for k in ride],
        out_specs=[pl.BlockSpec((tm, d), lambda i, j: (i, 0))] + [specs[k][1] for k in ride],
        out_shape=[jax.ShapeDtypeStruct((n, d), F32)]
        + [jax.ShapeDtypeStruct(riders[k][0].shape[1:], BF16) for k in ride],
        scratch_shapes=[pltpu.VMEM((tm, d), BF16)],
        compiler_params=pltpu.CompilerParams(
            dimension_semantics=("parallel", "arbitrary"), vmem_limit_bytes=VMEM_LIMIT),
        name="ffn",
    )(x2d, gain.reshape(1, d), wg, wu, wd, *[riders[k][0] for k in ride])
    cast = [outs[1 + ride.index(k)] if k in ride else w[layer].astype(BF16)
            for k, (w, layer) in enumerate(riders)]
    return outs[0], cast


def _proj_kernel(x_ref, g_ref, w_ref, o_ref, h_ref):
    @pl.when(pl.program_id(1) == 0)
    def _():
        h_ref[...] = _rms_rows(x_ref[...], g_ref[...]).astype(BF16)

    o_ref[...] = jnp.dot(h_ref[...], w_ref[...], preferred_element_type=F32).astype(o_ref.dtype)


def _proj(x2d, gain, w):
    n, d = x2d.shape
    dout = w.shape[1]
    tm = min(1024, n)
    tn = dout // 2
    return pl.pallas_call(
        _proj_kernel,
        grid=(n // tm, dout // tn),
        in_specs=[
            pl.BlockSpec((tm, d), lambda i, j: (i, 0)),
            pl.BlockSpec((1, d), lambda i, j: (0, 0)),
            pl.BlockSpec((d, tn), lambda i, j: (0, j)),
        ],
        out_specs=pl.BlockSpec((tm, tn), lambda i, j: (i, j)),
        out_shape=jax.ShapeDtypeStruct((n, dout), BF16),
        scratch_shapes=[pltpu.VMEM((tm, d), BF16)],
        compiler_params=pltpu.CompilerParams(
            dimension_semantics=("parallel", "arbitrary"), vmem_limit_bytes=VMEM_LIMIT),
        name="in_proj",
    )(x2d, gain.reshape(1, d), w)


def _outproj_kernel(x_ref, ya_ref, yb_ref, yc_ref, yd_ref, w_ref, o_ref):
    gw = GROUP_WIDTH
    acc = x_ref[...]
    for m, y_ref in enumerate((ya_ref, yb_ref, yc_ref, yd_ref)):
        acc = acc + jnp.dot(y_ref[...].astype(BF16), w_ref[m * gw:(m + 1) * gw, :],
                            preferred_element_type=F32)
    o_ref[...] = acc


def _outproj(x2d, ys, w):
    n, d = x2d.shape
    tm = min(512, n)
    yspec = pl.BlockSpec((tm, GROUP_WIDTH), lambda i: (i, 0))
    return pl.pallas_call(
        _outproj_kernel,
        grid=(n // tm,),
        in_specs=[pl.BlockSpec((tm, d), lambda i: (i, 0)), yspec, yspec, yspec, yspec,
                  pl.BlockSpec(w.shape, lambda i: (0, 0))],
        out_specs=pl.BlockSpec((tm, d), lambda i: (i, 0)),
        out_shape=jax.ShapeDtypeStruct((n, d), F32),
        compiler_params=pltpu.CompilerParams(
            dimension_semantics=("parallel",), vmem_limit_bytes=VMEM_LIMIT),
        name="out_proj",
    )(x2d, *ys, w)


def _advance(fillers):
    for gen in fillers:
        if next(gen, _DONE) is not _DONE:
            return


_DONE = object()


def _rwkv_kernel(p_ref, mu_ref, w0_ref, wup_ref, a0_ref, aup_ref, gup_ref, kk_ref, ka_ref, rk_ref,
                 lnw_ref, lnb_ref, tri_ref, bd_ref, o_ref, s_ref, prev_ref, osc_ref):
    c = RWKV_CHUNK
    nck = p_ref.shape[1] // c
    prm = dict(mu=mu_ref, w0=w0_ref, wup=wup_ref, a0=a0_ref, aup=aup_ref, gup=gup_ref, kk=kk_ref, ka=ka_ref,
               rk=rk_ref, lnw=lnw_ref, lnb=lnb_ref, tri=tri_ref, bd=bd_ref)
    chunk = lambda ref, ck: ref.at[:, pl.ds(ck * c, c), :]
    preps = [dict() for _ in range(nck)]
    for _ in _rwkv_prologue(chunk(p_ref, 0), prm, prev_ref, preps[0]):
        pass
    epilogue = iter(())
    for ck in range(nck):
        fillers = [epilogue]
        if ck + 1 < nck:
            fillers.append(_rwkv_prologue(chunk(p_ref, ck + 1), prm, prev_ref, preps[ck + 1]))
        y = _rwkv_solve(preps[ck], s_ref, functools.partial(_advance, fillers))
        for gen in fillers:
            for _ in gen:
                pass
        epilogue = _rwkv_epilogue(preps[ck], y, prm, chunk(o_ref, ck), osc_ref)
    for _ in epilogue:
        pass


def _rwkv_prologue(p_ref, prm, prev_ref, out):
    nb, c, _ = p_ref.shape
    gw = GROUP_WIDTH
    n = HEAD_DIM
    nh = gw // n

    p = p_ref[...].astype(F32).reshape(nb * c, RWKV_IN)
    row = lax.broadcasted_iota(jnp.int32, p.shape, 0)
    shifted = pltpu.roll(p, 1, axis=0)
    for s in range(nb):
        shifted = jnp.where(row == s * c, prev_ref[s], shifted)
        prev_ref[s] = p[(s + 1) * c - 1:(s + 1) * c, :]
    p = p + (shifted - p) * prm["mu"][...]
    yield

    r = p[:, 0:gw]
    k = p[:, gw:2 * gw]
    v = p[:, 2 * gw:3 * gw]
    o1 = 3 * gw
    w_lo = p[:, o1:o1 + RWKV_W_RANK]
    a_lo = p[:, o1 + RWKV_W_RANK:o1 + RWKV_W_RANK + RWKV_A_RANK]
    g_lo = p[:, o1 + RWKV_W_RANK + RWKV_A_RANK:]

    z = -(prm["w0"][...] + _mm(jnp.tanh(w_lo), prm["wup"][...]))
    softplus = jnp.maximum(z, 0.0) + jnp.log1p(jnp.exp(-jnp.abs(z)))
    lw = -jnp.exp(-softplus - 0.5)
    yield
    a = jax.nn.sigmoid(prm["a0"][...] + _mm(a_lo, prm["aup"][...]))
    g = _mm(jax.nn.sigmoid(g_lo), prm["gup"][...])
    yield

    kk = k * prm["kk"][...]
    kk = kk * lax.rsqrt(jnp.maximum(_head_sum(kk * kk, prm["bd"][...]), 1e-24))
    yield
    k = k * (1.0 + (a - 1.0) * prm["ka"][...])
    b = kk * a
    cum = _const_lhs_mm(prm["tri"][...], lw)
    yield
    cum_last = jnp.concatenate(
        [jnp.broadcast_to(cum[(s + 1) * c - 1:(s + 1) * c, :], (c, gw)) for s in range(nb)], axis=0)
    heads = lambda t, rows=c: jnp.stack(
        [t[s * c:s * c + rows, h * n:(h + 1) * n] for s in range(nb) for h in range(nh)], axis=0)
    out.update(r=r, k=k, v=v, g=g)
    e_neg = jnp.exp(-cum)
    out["bd_h"] = heads(b * e_neg)
    yield
    out["kd_h"] = heads(k * e_neg)
    yield
    out["kq_h"] = heads(kk * jnp.exp(cum - lw))
    yield
    out["rq_h"] = heads(r * jnp.exp(cum))
    yield
    out["v_h"] = heads(v)
    yield
    e_end = jnp.exp(cum_last - cum)
    out["be_h"] = heads(b * e_end)
    yield
    out["ke_h"] = heads(k * e_end)
    out["ge_h"] = heads(jnp.exp(cum_last), 1)
    yield


def _rwkv_solve(pp, s_ref, tick):
    kq_h, rq_h, bd_h, kd_h, v_h = pp["kq_h"], pp["rq_h"], pp["bd_h"], pp["kd_h"], pp["v_h"]
    c = kq_h.shape[1]
    ri = lax.broadcasted_iota(jnp.int32, (c, c), 0)
    ci = lax.broadcasted_iota(jnp.int32, (c, c), 1)
    strict = ri > ci
    incl = ri >= ci
    eye = (ri == ci).astype(F32)

    s0 = s_ref[...]
    gmat = _bmm(jnp.concatenate([kq_h, rq_h], axis=1), jnp.concatenate([bd_h, kd_h], axis=1), _BNT)
    tick()
    a_bb = jnp.where(strict, gmat[:, :c, :c], 0.0)
    a_bk = jnp.where(strict, gmat[:, :c, c:], 0.0)
    a_rb = jnp.where(incl, gmat[:, c:, :c], 0.0)
    a_rk = jnp.where(incl, gmat[:, c:, c:], 0.0)
    pw = -a_bb
    t_inv = eye + pw
    for _ in range(int(np.log2(c)) - 1):
        pw = _bmm(pw, pw)
        tick()
        t_inv = t_inv + _bmm(t_inv, pw)
        tick()
    x = _bmm(kq_h, s0, _BNT)
    tick()
    x = x + _bmm(a_bk, v_h)
    tick()
    u = -_bmm(t_inv, x)
    tick()
    y = _bmm(rq_h, s0, _BNT)
    tick()
    y = y + _bmm(a_rb, u)
    tick()
    y = y + _bmm(a_rk, v_h)
    tick()
    s_ref[...] = s0 * pp["ge_h"] + _bmm(jnp.concatenate([u, v_h], axis=1),
                                        jnp.concatenate([pp["be_h"], pp["ke_h"]], axis=1), _BTN)
    tick()
    return y


def _rwkv_epilogue(pp, y, prm, o_ref, osc_ref):
    nb, c, gw = o_ref.shape
    n = HEAD_DIM
    nh = gw // n
    ones_bd = prm["bd"][...]
    for s in range(nb):
        for h in range(nh):
            osc_ref[s * c:(s + 1) * c, h * n:(h + 1) * n] = y[s * nh + h]
    yield
    o = osc_ref[...]
    inv_n = 1.0 / n
    mean = _head_sum(o, ones_bd) * inv_n
    yield
    dlt = o - mean
    var = _head_sum(dlt * dlt, ones_bd) * inv_n
    yield
    o = dlt * lax.rsqrt(var + RWKV_LN_EPS) * prm["lnw"][...] + prm["lnb"][...]
    yield
    bonus = _head_sum(pp["r"] * pp["k"] * prm["rk"][...], ones_bd) * pp["v"]
    yield
    o_ref[...] = ((o + bonus) * pp["g"]).reshape(nb, c, gw).astype(o_ref.dtype)
    yield


def _full_spec(t):
    return pl.BlockSpec(t.shape, lambda i, j: (0,) * t.ndim)


def _mixer_call(part, bsz, seq, nb, tblk, name, riders=()):
    body, in_specs, operands, scratch = part
    gi, gj = bsz // nb, seq // tblk
    specs = [_rider_specs(w.shape, layer, gi, gj) for w, layer in riders]
    ride = [k for k, s in enumerate(specs) if s is not None]
    n_in, n_ride = len(operands), len(ride)

    def kern(*refs):
        ins, rider_in = refs[:n_in], refs[n_in:n_in + n_ride]
        out, rider_out = refs[n_in + n_ride], refs[n_in + n_ride + 1:n_in + 2 * n_ride + 1]
        scr = refs[n_in + 2 * n_ride + 1:]

        @pl.when(pl.program_id(1) == 0)
        def _():
            for ref in scr:
                ref[...] = jnp.zeros_like(ref)

        body(*ins, out, *scr)
        for src, dst in zip(rider_in, rider_out):
            dst[...] = src[...].astype(BF16)

    out_spec = pl.BlockSpec((nb, tblk, GROUP_WIDTH), lambda i, j: (i, j, 0))
    outs = pl.pallas_call(
        kern,
        grid=(gi, gj),
        in_specs=list(in_specs) + [specs[k][0] for k in ride],
        out_specs=[out_spec] + [specs[k][1] for k in ride],
        out_shape=[jax.ShapeDtypeStruct((bsz, seq, GROUP_WIDTH), BF16)]
        + [jax.ShapeDtypeStruct(riders[k][0].shape[1:], BF16) for k in ride],
        scratch_shapes=scratch,
        compiler_params=pltpu.CompilerParams(
            dimension_semantics=("parallel", "arbitrary"), vmem_limit_bytes=VMEM_LIMIT),
        name=name,
    )(*operands, *[riders[k][0] for k in ride])
    cast = [outs[1 + ride.index(k)] if k in ride else w[layer].astype(BF16)
            for k, (w, layer) in enumerate(riders)]
    return outs[0], cast


def _rwkv_part(p3, nb, tblk, mu, w0, w_up, a0, a_up, g_up, k_k, k_a, r_k, ln_w, ln_b):
    c = RWKV_CHUNK
    gw = GROUP_WIDTH
    tri = jnp.asarray(np.kron(np.eye(nb, dtype=np.float32), np.tril(np.ones((c, c), np.float32))), BF16)
    ones_bd = jnp.asarray(np.kron(np.eye(4, dtype=np.float32), np.ones((HEAD_DIM, HEAD_DIM), np.float32)), BF16)
    row = lambda t: t.reshape(1, -1)
    params = [row(mu), row(w0), w_up, row(a0), a_up, g_up, row(k_k), row(k_a), row(r_k), row(ln_w),
              row(ln_b), tri, ones_bd]
    in_specs = [pl.BlockSpec((nb, tblk, RWKV_IN), lambda i, j: (i, j, 0))] + [_full_spec(t) for t in params]
    scratch = [pltpu.VMEM((nb * gw // HEAD_DIM, HEAD_DIM, HEAD_DIM), F32),
               pltpu.VMEM((nb, 1, RWKV_IN), F32),
               pltpu.VMEM((nb * c, gw), F32)]
    return _rwkv_kernel, in_specs, [p3] + params, scratch


def _rwkv(p3, *params, riders=()):
    bsz, seq, _ = p3.shape
    nb = MIXER_SEQS if bsz % MIXER_SEQS == 0 else 1
    tblk = min(RWKV_BLOCK, seq)
    return _mixer_call(_rwkv_part(p3, nb, tblk, *params), bsz, seq, nb, tblk, "rwkv7", riders)


def _rope_table_kernel(pos_ref, invf_ref, cos_ref, sin_ref):
    ang = pos_ref[...].astype(F32) * invf_ref[...]
    cos_ref[...] = jnp.cos(ang)
    sin_ref[...] = jnp.sin(ang)


def _rope_tables(positions):
    bsz, seq = positions.shape
    d = HEAD_DIM
    w = 2 * d
    inv_freq = ROPE_THETA ** (-jnp.arange(0, ROPE_DIM, 2, dtype=F32) / ROPE_DIM)
    lane_d = np.arange(w) % d
    invf = jnp.where(lane_d < ROPE_DIM, inv_freq[lane_d % (ROPE_DIM // 2)], 0.0).reshape(1, w)
    tr = min(1024, bsz * seq)
    return pl.pallas_call(
        _rope_table_kernel,
        grid=(bsz * seq // tr,),
        in_specs=[pl.BlockSpec((tr, 1), lambda i: (i, 0)), pl.BlockSpec((1, w), lambda i: (0, 0))],
        out_specs=[pl.BlockSpec((tr, w), lambda i: (i, 0))] * 2,
        out_shape=[jax.ShapeDtypeStruct((bsz * seq, w), F32)] * 2,
        compiler_params=pltpu.CompilerParams(dimension_semantics=("parallel",)),
        name="rope_tables",
    )(positions.reshape(bsz * seq, 1), invf)


def _attn_kernel(cos_ref, sin_ref, q0_ref, q1_ref, kv_ref, qn_ref, kn_ref, sink_ref, bd_ref,
                 o_ref, kprev_ref, vprev_ref):
    nb, blk, _ = q0_ref.shape
    rows = nb * blk
    d = HEAD_DIM
    half = ROPE_DIM // 2
    group = ATTN_HEADS // ATTN_KV_HEADS
    kvw = ATTN_KV_HEADS * d
    first = pl.program_id(1) == 0
    ones_bd = bd_ref[...]
    cos = cos_ref[...].reshape(rows, 2 * d)
    sin = sin_ref[...].reshape(rows, 2 * d)

    def norm_rope(x, gain):
        w = x.shape[1]
        x = x * lax.rsqrt(_head_sum(x * x, ones_bd[:w, :w]) * (1.0 / d) + NORM_EPS) * gain
        ld = lax.broadcasted_iota(jnp.int32, (rows, w), 1) % d
        rot = jnp.where(ld < half, -pltpu.roll(x, w - half, axis=1),
                        jnp.where(ld < ROPE_DIM, pltpu.roll(x, half, axis=1), 0.0))
        tile = lambda t: t if w == t.shape[1] else jnp.concatenate([t] * (w // t.shape[1]), axis=1)
        return x * tile(cos) + rot * tile(sin)

    qs = [norm_rope(q_ref[...].astype(F32).reshape(rows, ATTN_COL_BLOCK), qn_ref[...]) * (d ** -0.5)
          for q_ref in (q0_ref, q1_ref)]
    kv = kv_ref[...].astype(F32).reshape(rows, ATTN_COL_BLOCK)
    k_cur = norm_rope(kv[:, :kvw], kn_ref[:, :kvw])
    v_cur = kv[:, kvw:]
    k_prev = kprev_ref[...]
    v_prev = vprev_ref[...]

    heads_per_ref = ATTN_COL_BLOCK // d
    q_head = lambda s, qh: qs[qh // heads_per_ref][s * blk:(s + 1) * blk,
                                                   (qh % heads_per_ref) * d:(qh % heads_per_ref + 1) * d]
    q_g = jnp.stack([jnp.concatenate([q_head(s, kh * group + i) for i in range(group)], axis=0)
                     for s in range(nb) for kh in range(ATTN_KV_HEADS)], axis=0)
    kv_heads = lambda t: jnp.stack([t[s * blk:(s + 1) * blk, kh * d:(kh + 1) * d]
                                    for s in range(nb) for kh in range(ATTN_KV_HEADS)], axis=0)

    ri = lax.broadcasted_iota(jnp.int32, (group * blk, blk), 0) % blk
    ci = lax.broadcasted_iota(jnp.int32, (group * blk, blk), 1)
    neg_inf = -jnp.inf
    no_prev = jnp.where(first, neg_inf, 0.0)
    s_prev = jnp.where(ci > ri, _bmm(q_g, kv_heads(k_prev), _BNT), neg_inf) + no_prev
    s_cur = jnp.where(ci <= ri, _bmm(q_g, kv_heads(k_cur), _BNT), neg_inf)
    sink = sink_ref[...]
    m = jnp.maximum(jnp.maximum(jnp.max(s_prev, axis=2, keepdims=True),
                                jnp.max(s_cur, axis=2, keepdims=True)), sink)
    p_prev = jnp.exp(s_prev - m)
    p_cur = jnp.exp(s_cur - m)
    ones = jnp.ones((nb * ATTN_KV_HEADS, blk, d), BF16)
    den = _bmm(p_prev, ones) + _bmm(p_cur, ones) + jnp.exp(sink - m)
    o = ((_bmm(p_prev, kv_heads(v_prev)) + _bmm(p_cur, kv_heads(v_cur))) / den).astype(o_ref.dtype)
    for s in range(nb):
        for qh in range(ATTN_HEADS):
            o_ref[s, :, qh * d:(qh + 1) * d] = o[s * ATTN_KV_HEADS + qh // group,
                                                 (qh % group) * blk:(qh % group + 1) * blk, :]

    kprev_ref[...] = k_cur
    vprev_ref[...] = v_cur


def _attn_part(p3, nb, rope_cos, rope_sin, q_norm, k_norm, sinks):
    bsz, seq, _ = p3.shape
    blk = ATTN_BLOCK
    cw = ATTN_COL_BLOCK
    d = HEAD_DIM
    qn = jnp.tile(q_norm.astype(F32), cw // d).reshape(1, cw)
    kn = jnp.tile(k_norm.astype(F32), cw // d).reshape(1, cw)
    ones_bd = jnp.asarray(np.kron(np.eye(cw // d, dtype=np.float32), np.ones((d, d), np.float32)), BF16)
    group = ATTN_HEADS // ATTN_KV_HEADS
    sink_col = jnp.repeat(sinks.astype(F32).reshape(ATTN_KV_HEADS, group), blk, axis=1)[:, :, None]
    sink_col = jnp.tile(sink_col, (nb, 1, 1))
    tw = rope_cos.shape[-1]
    cos3 = rope_cos.reshape(bsz, seq, tw)
    sin3 = rope_sin.reshape(bsz, seq, tw)
    col = lambda o: pl.BlockSpec((nb, blk, cw), lambda i, j: (i, j, ATTN_OFF_BLOCKS + o))
    trig = pl.BlockSpec((nb, blk, tw), lambda i, j: (i, j, 0))
    consts = [qn, kn, sink_col, ones_bd]
    in_specs = [trig, trig, col(0), col(1), col(2)] + [_full_spec(t) for t in consts]
    scratch = [pltpu.VMEM((nb * blk, ATTN_KV_HEADS * d), F32), pltpu.VMEM((nb * blk, ATTN_KV_HEADS * d), F32)]
    return _attn_kernel, in_specs, [cos3, sin3, p3, p3, p3] + consts, scratch


def _attn(p3, *params, riders=()):
    bsz, seq, _ = p3.shape
    nb = ATTN_SEQS if bsz % ATTN_SEQS == 0 else 1
    return _mixer_call(_attn_part(p3, nb, *params), bsz, seq, nb, ATTN_BLOCK, "swa_attn", riders)


def _s5_prep_kernel(lre_ref, lim_ref, ls_ref, bre_ref, bim_ref, are_ref, aim_ref, obre_ref, obim_ref):
    lre = lre_ref[...]
    lim = lim_ref[...]
    dt = jnp.exp(ls_ref[...])
    mag = jnp.exp(lre * dt)
    are = mag * jnp.cos(lim * dt)
    aim = mag * jnp.sin(lim * dt)
    are_ref[...] = are
    aim_ref[...] = aim
    inv = 1.0 / (lre * lre + lim * lim)
    cre = ((are - 1.0) * lre + aim * lim) * inv
    cim = (aim * lre - (are - 1.0) * lim) * inv
    bre = bre_ref[...]
    bim = bim_ref[...]
    obre_ref[...] = cre[:, None, :] * bre - cim[:, None, :] * bim
    obim_ref[...] = cre[:, None, :] * bim + cim[:, None, :] * bre


def _s5_kernel(u_ref, are_ref, aim_ref, b_ref, c_ref, d_ref, gw_ref, gb_ref, o_ref, x_ref, s_ref):
    bsz, tc, ch = u_ref.shape
    hw = S5_WIDTH // 2
    hc = ch // 2

    @pl.when(pl.program_id(0) == 0)
    def _():
        s_ref[...] = jnp.zeros_like(s_ref)

    u = jnp.swapaxes(u_ref[...].astype(F32), 0, 1).reshape(tc * bsz, ch)
    for hf in range(2):
        x_ref[:, 2 * hw * hf:2 * hw * (hf + 1)] = _mm(u[:, hc * hf:hc * (hf + 1)], b_ref[hf])

    for hf in range(2):
        for s0 in range(0, hw, S5_STRIP):
            re = slice(2 * hw * hf + s0, 2 * hw * hf + s0 + S5_STRIP)
            im = slice(2 * hw * hf + hw + s0, 2 * hw * hf + hw + s0 + S5_STRIP)
            lam = slice(hw * hf + s0, hw * hf + s0 + S5_STRIP)
            ar = jnp.broadcast_to(are_ref[:, lam], (bsz, S5_STRIP))
            ai = jnp.broadcast_to(aim_ref[:, lam], (bsz, S5_STRIP))

            def step(t, carry, re=re, im=im, ar=ar, ai=ai):
                sr, si = carry
                rows = pl.ds(pl.multiple_of(t * bsz, bsz), bsz)
                nr = ar * sr - ai * si + x_ref[rows, re]
                ni = ar * si + ai * sr + x_ref[rows, im]
                x_ref[rows, re] = nr
                x_ref[rows, im] = ni
                return nr, ni

            sr, si = lax.fori_loop(0, tc, step, (s_ref[:, re], s_ref[:, im]), unroll=4)
            s_ref[:, re] = sr
            s_ref[:, im] = si

    y = jnp.concatenate([_mm(x_ref[:, 2 * hw * hf:2 * hw * (hf + 1)], c_ref[hf]) for hf in range(2)], axis=1)
    y = y + d_ref[...] * u
    z = 0.5 * y * (1.0 + lax.erf(y * (2.0 ** -0.5)))
    out = z * jax.nn.sigmoid(_mm(z, gw_ref[...]) + gb_ref[...])
    o_ref[...] = jnp.swapaxes(out.reshape(tc, bsz, ch), 0, 1).astype(o_ref.dtype)


def _s5(p3, lam_re, lam_im, log_step, b_re, b_im, c_re, c_im, d_skip, glu_w, glu_b):
    bsz, seq, _ = p3.shape
    g, st, ch = S5_GROUPS, S5_STATE, S5_GROUP
    gwd = GROUP_WIDTH
    vm = pl.BlockSpec(memory_space=pltpu.VMEM)
    a_re, a_im, bb_re, bb_im = pl.pallas_call(
        _s5_prep_kernel,
        in_specs=[vm] * 5,
        out_specs=[vm] * 4,
        out_shape=[jax.ShapeDtypeStruct((g, st), F32)] * 2 + [jax.ShapeDtypeStruct((g, ch, st), F32)] * 2,
        name="s5_prep",
    )(lam_re, lam_im, log_step.reshape(g, 1), jnp.swapaxes(b_re, 1, 2), jnp.swapaxes(b_im, 1, 2))

    gh = g // 2
    eye = jnp.eye(gh, dtype=F32)
    bd_in = lambda t: (t[:, :, None, :] * eye[:, None, :, None]).reshape(gh * ch, gh * st)
    bd_out = lambda t: (jnp.swapaxes(t, 1, 2)[:, :, None, :] * eye[:, None, :, None]).reshape(gh * st, gh * ch)
    halves = lambda t: (t[:gh], t[gh:])
    b_mat = jnp.stack([jnp.concatenate([bd_in(r), bd_in(i)], axis=1)
                       for r, i in zip(halves(bb_re), halves(bb_im))]).astype(BF16)
    c_mat = jnp.stack([jnp.concatenate([bd_out(r), -bd_out(i)], axis=0)
                       for r, i in zip(halves(c_re.astype(F32)), halves(c_im.astype(F32)))]).astype(BF16)

    tc = min(64, seq)
    full = lambda t: pl.BlockSpec(t.shape, lambda i: (0,) * t.ndim)
    params = [a_re.reshape(1, S5_WIDTH), a_im.reshape(1, S5_WIDTH), b_mat, c_mat,
              d_skip.reshape(1, gwd), glu_w.astype(BF16), glu_b.reshape(1, gwd)]
    return pl.pallas_call(
        _s5_kernel,
        grid=(seq // tc,),
        in_specs=[pl.BlockSpec((bsz, tc, gwd), lambda i: (0, i, S5_OFF_BLOCKS))] + [full(t) for t in params],
        out_specs=pl.BlockSpec((bsz, tc, gwd), lambda i: (0, i, 0)),
        out_shape=jax.ShapeDtypeStruct((bsz, seq, gwd), BF16),
        scratch_shapes=[pltpu.VMEM((tc * bsz, 2 * S5_WIDTH), F32), pltpu.VMEM((bsz, 2 * S5_WIDTH), F32)],
        compiler_params=pltpu.CompilerParams(
            dimension_semantics=("arbitrary",), vmem_limit_bytes=VMEM_LIMIT),
        name="s5_scan",
    )(p3, *params)


def _hgrn_kernel(qf_ref, ig_ref, lbraw_ref, gn_ref, tri_ref, bd_ref, o_ref, s_ref, osc_ref, *, layer):
    nb, tb, _ = qf_ref.shape
    cz = HGRN_CHUNK
    gw = GROUP_WIDTH
    n = HEAD_DIM
    nh = gw // n

    qf = qf_ref[...].astype(F32).reshape(nb * tb, 2 * gw)
    ig = ig_ref[...].astype(F32).reshape(nb * tb, 2 * gw)
    q, f = qf[:, :gw], qf[:, gw:]
    v, g = ig[:, :gw], ig[:, gw:]

    lbr = lbraw_ref[...]
    e = jnp.exp(lbr - jnp.max(lbr, axis=0, keepdims=True))
    sm = e / jnp.sum(e, axis=0, keepdims=True)
    lb = jnp.zeros((1, gw), F32)
    for i in range(1, layer + 1):
        lb = lb + sm[i:i + 1, :]

    q = _silu(q) * (n ** -0.5)
    f_gate = lb + (1.0 - lb) * jax.nn.sigmoid(f)
    log_f = jnp.log(f_gate)
    k = 1.0 - f_gate

    parts = _split2(log_f)

    def per_seq(c_ref):
        cm = c_ref[...]
        seq_rows = lambda t, s: t[s * tb:(s + 1) * tb, :]
        return jnp.concatenate(
            [_dot(cm, seq_rows(parts[0], s), _NN) + _dot(cm, seq_rows(parts[1], s), _NN)
             for s in range(nb)], axis=0)

    bcum = per_seq(tri_ref)
    b3 = bcum.reshape(nb * tb // cz, cz, gw)
    chunk_row = lambda r: jnp.broadcast_to(b3[:, r:r + 1, :], b3.shape).reshape(nb * tb, gw)
    bmid = chunk_row(cz // 2 - 1)
    blast = chunk_row(cz - 1)
    qe = q * jnp.exp(bcum - bmid)
    ke = k * jnp.exp(bmid - bcum)
    kl = k * jnp.exp(blast - bcum)
    qb = q * jnp.exp(bcum)
    dec = jnp.exp(blast)

    ri = lax.broadcasted_iota(jnp.int32, (tb, tb), 0)
    ci = lax.broadcasted_iota(jnp.int32, (tb, tb), 1)
    mask = jnp.logical_and(ri // cz == ci // cz, ri >= ci)

    heads = lambda t: jnp.stack(
        [t[s * tb:(s + 1) * tb, h * n:(h + 1) * n] for s in range(nb) for h in range(nh)], axis=0)
    v_h, qb_h, kl_h, dec_h = heads(v), heads(qb), heads(kl), heads(dec)
    att = jnp.where(mask, _bmm(heads(qe), heads(ke), _BNT), 0.0)
    o_intra = _bmm(att, v_h)
    nchunk = tb // cz
    ti = lax.broadcasted_iota(jnp.int32, (tb, nchunk * n), 0)
    li = lax.broadcasted_iota(jnp.int32, (tb, nchunk * n), 1)
    kl_spread = jnp.where(ti // cz == li // n, jnp.concatenate([kl_h] * nchunk, axis=2), 0.0)
    kv_all = _bmm(v_h, kl_spread, _BTN)
    st = s_ref[...]
    o_inter = []
    for j in range(nchunk):
        rows = slice(j * cz, (j + 1) * cz)
        o_inter.append(_bmm(qb_h[:, rows, :], st, _BNT))
        st = st * dec_h[:, j * cz:j * cz + 1, :] + kv_all[:, :, j * n:(j + 1) * n]
    s_ref[...] = st
    o_heads = o_intra + jnp.concatenate(o_inter, axis=1)
    for s in range(nb):
        for h in range(nh):
            osc_ref[s * tb:(s + 1) * tb, h * n:(h + 1) * n] = o_heads[s * nh + h]

    o = osc_ref[...]
    ms = _head_sum(o * o, bd_ref[...]) * (1.0 / n)
    o_ref[...] = (o * lax.rsqrt(ms + NORM_EPS) * gn_ref[...] * _silu(g)).reshape(nb, tb, gw).astype(o_ref.dtype)


def _hgrn_part(p3, nb, lower_bounds, g_norm, layer):
    tb = HGRN_TILE
    cz = HGRN_CHUNK
    gw = GROUP_WIDTH
    idx = np.arange(tb)
    same = (idx[:, None] // cz) == (idx[None, :] // cz)
    tri = same & (idx[:, None] >= idx[None, :])
    consts = [jnp.asarray(tri.astype(np.float32), BF16)]
    ones_bd = jnp.asarray(np.kron(np.eye(4, dtype=np.float32), np.ones((HEAD_DIM, HEAD_DIM), np.float32)), BF16)
    gn = jnp.tile(g_norm.astype(F32), gw // HEAD_DIM).reshape(1, gw)
    params = [lower_bounds.astype(F32), gn] + consts + [ones_bd]
    col = lambda o: pl.BlockSpec((nb, tb, 2 * gw), lambda i, j: (i, j, HGRN_OFF_BLOCKS + o))
    in_specs = [col(0), col(1)] + [_full_spec(t) for t in params]
    scratch = [pltpu.VMEM((nb * gw // HEAD_DIM, HEAD_DIM, HEAD_DIM), F32), pltpu.VMEM((nb * tb, gw), F32)]
    return functools.partial(_hgrn_kernel, layer=layer), in_specs, [p3, p3] + params, scratch


def _hgrn(p3, *params, riders=()):
    bsz, seq, _ = p3.shape
    nb = HGRN_SEQS if bsz % HGRN_SEQS == 0 else 1
    return _mixer_call(_hgrn_part(p3, nb, *params), bsz, seq, nb, HGRN_TILE, "hgrn2", riders)


def kernel(x, positions, ffn1_norm, ffn1_w_gate, ffn1_w_up, ffn1_w_down, mix_norm, w_in, rwkv_mu, rwkv_w0, rwkv_w_up, rwkv_a0, rwkv_a_up, rwkv_g_up, rwkv_k_k, rwkv_k_a, rwkv_r_k, rwkv_ln_w, rwkv_ln_b, attn_q_norm, attn_k_norm, attn_sinks, s5_lambda_re, s5_lambda_im, s5_log_step, s5_b_re, s5_b_im, s5_c_re, s5_c_im, s5_d, s5_glu_w, s5_glu_b, hgrn_lower_bounds, hgrn_g_norm, w_out, ffn2_norm, ffn2_w_gate, ffn2_w_up, ffn2_w_down):
    bsz, seq, d = x.shape
    depth = w_in.shape[0]
    n = bsz * seq
    xf = x.reshape(n, d)
    rope_cos, rope_sin = _rope_tables(positions)
    ffn_w32 = []
    for l in range(depth):
        ffn_w32.append([(w, l) for w in (ffn1_w_gate, ffn1_w_up, ffn1_w_down)])
        ffn_w32.append([(w, l) for w in (ffn2_w_gate, ffn2_w_up, ffn2_w_down)])
    ffn_w32.append([])
    w16 = [w[l].astype(BF16) for w, l in ffn_w32[0]]
    w_in16 = w_in[0].astype(BF16)
    for l in range(depth):
        xf, w16_next = _ffn(xf, ffn1_norm[l], *w16, riders=ffn_w32[2 * l + 1])
        p3 = _proj(xf, mix_norm[l], w_in16).reshape(bsz, seq, D_IN)
        y_a, (w_out16,) = _rwkv(p3, rwkv_mu[l], rwkv_w0[l], rwkv_w_up[l], rwkv_a0[l], rwkv_a_up[l], rwkv_g_up[l],
                                rwkv_k_k[l], rwkv_k_a[l], rwkv_r_k[l].reshape(-1), rwkv_ln_w[l], rwkv_ln_b[l],
                                riders=[(w_out, l)])
        y_b, _ = _attn(p3, rope_cos, rope_sin, attn_q_norm[l], attn_k_norm[l], attn_sinks[l])
        y_c = _s5(p3, s5_lambda_re[l], s5_lambda_im[l], s5_log_step[l], s5_b_re[l], s5_b_im[l],
                  s5_c_re[l], s5_c_im[l], s5_d[l], s5_glu_w[l], s5_glu_b[l])
        y_d, next_in = _hgrn(p3, hgrn_lower_bounds, hgrn_g_norm[l], l,
                             riders=[(w_in, l + 1)] if l + 1 < depth else [])
        if next_in:
            w_in16 = next_in[0]
        ys = [t.reshape(n, GROUP_WIDTH) for t in (y_a, y_b, y_c, y_d)]
        xf = _outproj(xf, ys, w_out16)
        xf, w16 = _ffn(xf, ffn2_norm[l], *w16_next, riders=ffn_w32[2 * l + 2])
    return xf.reshape(bsz, seq, d)
```

```python
import functools

import jax
import jax.numpy as jnp
import numpy as np
from jax import lax
from jax.experimental import pallas as pl
from jax.experimental.pallas import tpu as pltpu

F32 = jnp.float32
BF16 = jnp.bfloat16

HEAD_DIM = 64
GROUP_WIDTH = 512
NORM_EPS = 1e-6
FFN_RES_WEIGHT = 0.5

RWKV_W_RANK = 64
RWKV_A_RANK = 64
RWKV_G_RANK = 128
RWKV_LN_EPS = 64e-5
RWKV_IN = 3 * GROUP_WIDTH + RWKV_W_RANK + RWKV_A_RANK + RWKV_G_RANK
RWKV_CHUNK = 64
RWKV_BLOCK = 128
MIXER_SEQS = 4

ATTN_HEADS = 8
ATTN_KV_HEADS = 2
ATTN_BLOCK = 128
ATTN_SEQS = 8
ROPE_THETA = 500000.0
ROPE_DIM = HEAD_DIM // 4
ATTN_COL_BLOCK = 256
ATTN_OFF_BLOCKS = RWKV_IN // ATTN_COL_BLOCK

S5_GROUP = 16
S5_GROUPS = GROUP_WIDTH // S5_GROUP
S5_STATE = 64
S5_WIDTH = S5_GROUPS * S5_STATE
S5_OFF_BLOCKS = (RWKV_IN + 768) // GROUP_WIDTH
S5_STRIP = 512

HGRN_CHUNK = 16
HGRN_TILE = 128
HGRN_SEQS = 8
HGRN_OFF_BLOCKS = 3

D_IN = 5120
VMEM_LIMIT = 56 * 1024 * 1024


def _dot(a, b, dims):
    return lax.dot_general(a, b, (dims, ((), ())), preferred_element_type=F32)


_NN = ((1,), (0,))


def _mm(a, b):
    return _dot(a.astype(BF16), b.astype(BF16), _NN)


_BNN = (((2,), (1,)), ((0,), (0,)))
_BNT = (((2,), (2,)), ((0,), (0,)))
_BTN = (((1,), (1,)), ((0,), (0,)))


def _bmm(a, b, dims=_BNN):
    return lax.dot_general(a.astype(BF16), b.astype(BF16), dims, preferred_element_type=F32)


def _split2(x):
    hi = x.astype(BF16)
    lo = (x - hi.astype(F32)).astype(BF16)
    return hi, lo


def _const_lhs_mm(c, x):
    hi, lo = _split2(x)
    return _dot(c, hi, _NN) + _dot(c, lo, _NN)


def _const_rhs_mm(x, c):
    hi, lo = _split2(x)
    return _dot(hi, c, _NN) + _dot(lo, c, _NN)


def _head_sum(x, ones_bd):
    w = ones_bd.shape[0]
    parts = [_const_rhs_mm(x[:, i:i + w], ones_bd) for i in range(0, x.shape[1], w)]
    return parts[0] if len(parts) == 1 else jnp.concatenate(parts, axis=1)


def _silu(x):
    return x * jax.nn.sigmoid(x)


def _rms_rows(x, gain):
    ms = jnp.mean(x * x, axis=-1, keepdims=True)
    return x * lax.rsqrt(ms + NORM_EPS) * gain


def _ffn_kernel(x_ref, g_ref, wg_ref, wu_ref, wd_ref, *rest):
    n_riders = (len(rest) - 2) // 2
    o_ref, h_ref = rest[n_riders], rest[-1]

    @pl.when(pl.program_id(1) == 0)
    def _():
        x = x_ref[...]
        h_ref[...] = _rms_rows(x, g_ref[...]).astype(BF16)
        o_ref[...] = x

    h = h_ref[...]
    gate = jnp.dot(h, wg_ref[...], preferred_element_type=F32)
    up = jnp.dot(h, wu_ref[...], preferred_element_type=F32)
    act = (_silu(gate) * up).astype(BF16)
    o_ref[...] += FFN_RES_WEIGHT * jnp.dot(act, wd_ref[...], preferred_element_type=F32)
    for src, dst in zip(rest[:n_riders], rest[n_riders + 1:-1]):
        dst[...] = src[...].astype(BF16)


def _rider_specs(shape, layer, gi, gj):
    _, r, c = shape
    steps = gi * gj
    if r % steps == 0 and (r // steps) % 16 == 0:
        blk, pos = (r // steps, c), lambda i, j: (i * gj + j, 0)
    elif r % gi == 0 and (r // gi) % 16 == 0 and c % gj == 0 and (c // gj) % 128 == 0:
        blk, pos = (r // gi, c // gj), lambda i, j: (i, j)
    else:
        return None
    return pl.BlockSpec((None,) + blk, lambda i, j: (layer,) + pos(i, j)), pl.BlockSpec(blk, pos)


def _ffn(x2d, gain, wg, wu, wd, riders=()):
    n, d = x2d.shape
    f = wg.shape[1]
    tm = min(1024, n)
    tf = 512 if f % 512 == 0 else f
    gi, gj = n // tm, f // tf
    specs = [_rider_specs(w.shape, layer, gi, gj) for w, layer in riders]
    ride = [k for k, s in enumerate(specs) if s is not None]
    outs = pl.pallas_call(
        _ffn_kernel,
        grid=(gi, gj),
        in_specs=[
            pl.BlockSpec((tm, d), lambda i, j: (i, 0)),
            pl.BlockSpec((1, d), lambda i, j: (0, 0)),
            pl.BlockSpec((d, tf), lambda i, j: (0, j)),
            pl.BlockSpec((d, tf), lambda i, j: (0, j)),
            pl.BlockSpec((tf, d), lambda i, j: (j, 0)),
        ] + [specs[k][0] for k in ride],
        out_specs=[pl.BlockSpec((tm, d), lambda i, j: (i, 0))] + [specs[k][1] for k in ride],
        out_shape=[jax.ShapeDtypeStruct((n, d), F32)]
        + [jax.ShapeDtypeStruct(riders[k][0].shape[1:], BF16) for k in ride],
        scratch_shapes=[pltpu.VMEM((tm, d), BF16)],
        compiler_params=pltpu.CompilerParams(
            dimension_semantics=("parallel", "arbitrary"), vmem_limit_bytes=VMEM_LIMIT),
        name="ffn",
    )(x2d, gain.reshape(1, d), wg, wu, wd, *[riders[k][0] for k in ride])
    cast = [outs[1 + ride.index(k)] if k in ride else w[layer].astype(BF16)
            for k, (w, layer) in enumerate(riders)]
    return outs[0], cast


def _proj_kernel(x_ref, g_ref, w_ref, o_ref, h_ref):
    @pl.when(pl.program_id(1) == 0)
    def _():
        h_ref[...] = _rms_rows(x_ref[...], g_ref[...]).astype(BF16)

    o_ref[...] = jnp.dot(h_ref[...], w_ref[...], preferred_element_type=F32).astype(o_ref.dtype)


def _proj(x2d, gain, w):
    n, d = x2d.shape
    dout = w.shape[1]
    tm = min(1024, n)
    tn = dout // 2
    return pl.pallas_call(
        _proj_kernel,
        grid=(n // tm, dout // tn),
        in_specs=[
            pl.BlockSpec((tm, d), lambda i, j: (i, 0)),
            pl.BlockSpec((1, d), lambda i, j: (0, 0)),
            pl.BlockSpec((d, tn), lambda i, j: (0, j)),
        ],
        out_specs=pl.BlockSpec((tm, tn), lambda i, j: (i, j)),
        out_shape=jax.ShapeDtypeStruct((n, dout), BF16),
        scratch_shapes=[pltpu.VMEM((tm, d), BF16)],
        compiler_params=pltpu.CompilerParams(
            dimension_semantics=("parallel", "arbitrary"), vmem_limit_bytes=VMEM_LIMIT),
        name="in_proj",
    )(x2d, gain.reshape(1, d), w)


def _outproj_kernel(x_ref, ya_ref, yb_ref, yc_ref, yd_ref, w_ref, o_ref):
    gw = GROUP_WIDTH
    acc = x_ref[...]
    for m, y_ref in enumerate((ya_ref, yb_ref, yc_ref, yd_ref)):
        acc = acc + jnp.dot(y_ref[...].astype(BF16), w_ref[m * gw:(m + 1) * gw, :],
                            preferred_element_type=F32)
    o_ref[...] = acc


def _outproj(x2d, ys, w):
    n, d = x2d.shape
    tm = min(512, n)
    yspec = pl.BlockSpec((tm, GROUP_WIDTH), lambda i: (i, 0))
    return pl.pallas_call(
        _outproj_kernel,
        grid=(n // tm,),
        in_specs=[pl.BlockSpec((tm, d), lambda i: (i, 0)), yspec, yspec, yspec, yspec,
                  pl.BlockSpec(w.shape, lambda i: (0, 0))],
        out_specs=pl.BlockSpec((tm, d), lambda i: (i, 0)),
        out_shape=jax.ShapeDtypeStruct((n, d), F32),
        compiler_params=pltpu.CompilerParams(
            dimension_semantics=("parallel",), vmem_limit_bytes=VMEM_LIMIT),
        name="out_proj",
    )(x2d, *ys, w)


def _advance(fillers):
    for gen in fillers:
        if next(gen, _DONE) is not _DONE:
            return


_DONE = object()


def _rwkv_kernel(p_ref, mu_ref, w0_ref, wup_ref, a0_ref, aup_ref, gup_ref, kk_ref, ka_ref, rk_ref,
                 lnw_ref, lnb_ref, tri_ref, bd_ref, o_ref, s_ref, prev_ref, osc_ref):
    c = RWKV_CHUNK
    nck = p_ref.shape[1] // c
    prm = dict(mu=mu_ref, w0=w0_ref, wup=wup_ref, a0=a0_ref, aup=aup_ref, gup=gup_ref, kk=kk_ref, ka=ka_ref,
               rk=rk_ref, lnw=lnw_ref, lnb=lnb_ref, tri=tri_ref, bd=bd_ref)
    chunk = lambda ref, ck: ref.at[:, pl.ds(ck * c, c), :]
    preps = [dict() for _ in range(nck)]
    for _ in _rwkv_prologue(chunk(p_ref, 0), prm, prev_ref, preps[0]):
        pass
    epilogue = iter(())
    for ck in range(nck):
        fillers = [epilogue]
        if ck + 1 < nck:
            fillers.append(_rwkv_prologue(chunk(p_ref, ck + 1), prm, prev_ref, preps[ck + 1]))
        y = _rwkv_solve(preps[ck], s_ref, functools.partial(_advance, fillers))
        for gen in fillers:
            for _ in gen:
                pass
        epilogue = _rwkv_epilogue(preps[ck], y, prm, chunk(o_ref, ck), osc_ref)
    for _ in epilogue:
        pass


def _rwkv_prologue(p_ref, prm, prev_ref, out):
    nb, c, _ = p_ref.shape
    gw = GROUP_WIDTH
    n = HEAD_DIM
    nh = gw // n

    p = p_ref[...].astype(F32).reshape(nb * c, RWKV_IN)
    row = lax.broadcasted_iota(jnp.int32, p.shape, 0)
    shifted = pltpu.roll(p, 1, axis=0)
    for s in range(nb):
        shifted = jnp.where(row == s * c, prev_ref[s], shifted)
        prev_ref[s] = p[(s + 1) * c - 1:(s + 1) * c, :]
    p = p + (shifted - p) * prm["mu"][...]
    yield

    r = p[:, 0:gw]
    k = p[:, gw:2 * gw]
    v = p[:, 2 * gw:3 * gw]
    o1 = 3 * gw
    w_lo = p[:, o1:o1 + RWKV_W_RANK]
    a_lo = p[:, o1 + RWKV_W_RANK:o1 + RWKV_W_RANK + RWKV_A_RANK]
    g_lo = p[:, o1 + RWKV_W_RANK + RWKV_A_RANK:]

    z = -(prm["w0"][...] + _mm(jnp.tanh(w_lo), prm["wup"][...]))
    softplus = jnp.maximum(z, 0.0) + jnp.log1p(jnp.exp(-jnp.abs(z)))
    lw = -jnp.exp(-softplus - 0.5)
    yield
    a = jax.nn.sigmoid(prm["a0"][...] + _mm(a_lo, prm["aup"][...]))
    g = _mm(jax.nn.sigmoid(g_lo), prm["gup"][...])
    yield

    kk = k * prm["kk"][...]
    kk = kk * lax.rsqrt(jnp.maximum(_head_sum(kk * kk, prm["bd"][...]), 1e-24))
    yield
    k = k * (1.0 + (a - 1.0) * prm["ka"][...])
    b = kk * a
    cum = _const_lhs_mm(prm["tri"][...], lw)
    yield
    cum_last = jnp.concatenate(
        [jnp.broadcast_to(cum[(s + 1) * c - 1:(s + 1) * c, :], (c, gw)) for s in range(nb)], axis=0)
    heads = lambda t, rows=c: jnp.stack(
        [t[s * c:s * c + rows, h * n:(h + 1) * n] for s in range(nb) for h in range(nh)], axis=0)
    mxu_heads = lambda t: heads(t.astype(BF16))
    out.update(r=r, k=k, v=v, g=g)
    e_neg = jnp.exp(-cum)
    out["bd_h"] = mxu_heads(b * e_neg)
    yield
    out["kd_h"] = mxu_heads(k * e_neg)
    yield
    out["kq_h"] = mxu_heads(kk * jnp.exp(cum - lw))
    yield
    out["rq_h"] = mxu_heads(r * jnp.exp(cum))
    yield
    out["v_h"] = mxu_heads(v)
    yield
    e_end = jnp.exp(cum_last - cum)
    out["be_h"] = mxu_heads(b * e_end)
    yield
    out["ke_h"] = mxu_heads(k * e_end)
    out["ge_h"] = heads(jnp.exp(cum_last), 1)
    yield


def _rwkv_solve(pp, s_ref, tick):
    kq_h, rq_h, bd_h, kd_h, v_h = pp["kq_h"], pp["rq_h"], pp["bd_h"], pp["kd_h"], pp["v_h"]
    c = kq_h.shape[1]
    ri = lax.broadcasted_iota(jnp.int32, (c, c), 0)
    ci = lax.broadcasted_iota(jnp.int32, (c, c), 1)
    strict = ri > ci
    incl = ri >= ci
    eye = (ri == ci).astype(F32)

    s0 = s_ref[...]
    gmat = _bmm(jnp.concatenate([kq_h, rq_h], axis=1), jnp.concatenate([bd_h, kd_h], axis=1), _BNT)
    tick()
    a_bb = jnp.where(strict, gmat[:, :c, :c], 0.0)
    a_bk = jnp.where(strict, gmat[:, :c, c:], 0.0)
    a_rb = jnp.where(incl, gmat[:, c:, :c], 0.0)
    a_rk = jnp.where(incl, gmat[:, c:, c:], 0.0)
    pw = -a_bb
    t_inv = eye + pw
    for _ in range(int(np.log2(c)) - 1):
        pw = _bmm(pw, pw)
        tick()
        t_inv = t_inv + _bmm(t_inv, pw)
        tick()
    x = _bmm(kq_h, s0, _BNT)
    tick()
    x = x + _bmm(a_bk, v_h)
    tick()
    u = -_bmm(t_inv, x)
    tick()
    y = _bmm(rq_h, s0, _BNT)
    tick()
    y = y + _bmm(a_rb, u)
    tick()
    y = y + _bmm(a_rk, v_h)
    tick()
    s_ref[...] = s0 * pp["ge_h"] + _bmm(jnp.concatenate([u.astype(BF16), v_h], axis=1),
                                        jnp.concatenate([pp["be_h"], pp["ke_h"]], axis=1), _BTN)
    tick()
    return y


def _rwkv_epilogue(pp, y, prm, o_ref, osc_ref):
    nb, c, gw = o_ref.shape
    n = HEAD_DIM
    nh = gw // n
    ones_bd = prm["bd"][...]
    for s in range(nb):
        for h in range(nh):
            osc_ref[s * c:(s + 1) * c, h * n:(h + 1) * n] = y[s * nh + h]
    yield
    o = osc_ref[...]
    inv_n = 1.0 / n
    mean = _head_sum(o, ones_bd) * inv_n
    yield
    dlt = o - mean
    var = _head_sum(dlt * dlt, ones_bd) * inv_n
    yield
    o = dlt * lax.rsqrt(var + RWKV_LN_EPS) * prm["lnw"][...] + prm["lnb"][...]
    yield
    bonus = _head_sum(pp["r"] * pp["k"] * prm["rk"][...], ones_bd) * pp["v"]
    yield
    o_ref[...] = ((o + bonus) * pp["g"]).reshape(nb, c, gw).astype(o_ref.dtype)
    yield


def _full_spec(t):
    return pl.BlockSpec(t.shape, lambda i, j: (0,) * t.ndim)


def _mixer_call(part, bsz, seq, nb, tblk, name, riders=()):
    body, in_specs, operands, scratch = part
    gi, gj = bsz // nb, seq // tblk
    specs = [_rider_specs(w.shape, layer, gi, gj) for w, layer in riders]
    ride = [k for k, s in enumerate(specs) if s is not None]
    n_in, n_ride = len(operands), len(ride)

    def kern(*refs):
        ins, rider_in = refs[:n_in], refs[n_in:n_in + n_ride]
        out, rider_out = refs[n_in + n_ride], refs[n_in + n_ride + 1:n_in + 2 * n_ride + 1]
        scr = refs[n_in + 2 * n_ride + 1:]

        @pl.when(pl.program_id(1) == 0)
        def _():
            for ref in scr:
                ref[...] = jnp.zeros_like(ref)

        body(*ins, out, *scr)
        for src, dst in zip(rider_in, rider_out):
            dst[...] = src[...].astype(BF16)

    out_spec = pl.BlockSpec((nb, tblk, GROUP_WIDTH), lambda i, j: (i, j, 0))
    outs = pl.pallas_call(
        kern,
        grid=(gi, gj),
        in_specs=list(in_specs) + [specs[k][0] for k in ride],
        out_specs=[out_spec] + [specs[k][1] for k in ride],
        out_shape=[jax.ShapeDtypeStruct((bsz, seq, GROUP_WIDTH), BF16)]
        + [jax.ShapeDtypeStruct(riders[k][0].shape[1:], BF16) for k in ride],
        scratch_shapes=scratch,
        compiler_params=pltpu.CompilerParams(
            dimension_semantics=("parallel", "arbitrary"), vmem_limit_bytes=VMEM_LIMIT),
        name=name,
    )(*operands, *[riders[k][0] for k in ride])
    cast = [outs[1 + ride.index(k)] if k in ride else w[layer].astype(BF16)
            for k, (w, layer) in enumerate(riders)]
    return outs[0], cast


def _rwkv_part(p3, nb, tblk, mu, w0, w_up, a0, a_up, g_up, k_k, k_a, r_k, ln_w, ln_b):
    c = RWKV_CHUNK
    gw = GROUP_WIDTH
    tri = jnp.asarray(np.kron(np.eye(nb, dtype=np.float32), np.tril(np.ones((c, c), np.float32))), BF16)
    ones_bd = jnp.asarray(np.kron(np.eye(4, dtype=np.float32), np.ones((HEAD_DIM, HEAD_DIM), np.float32)), BF16)
    row = lambda t: t.reshape(1, -1)
    params = [row(mu), row(w0), w_up, row(a0), a_up, g_up, row(k_k), row(k_a), row(r_k), row(ln_w),
              row(ln_b), tri, ones_bd]
    in_specs = [pl.BlockSpec((nb, tblk, RWKV_IN), lambda i, j: (i, j, 0))] + [_full_spec(t) for t in params]
    scratch = [pltpu.VMEM((nb * gw // HEAD_DIM, HEAD_DIM, HEAD_DIM), F32),
               pltpu.VMEM((nb, 1, RWKV_IN), F32),
               pltpu.VMEM((nb * c, gw), F32)]
    return _rwkv_kernel, in_specs, [p3] + params, scratch


def _rwkv(p3, *params, riders=()):
    bsz, seq, _ = p3.shape
    nb = MIXER_SEQS if bsz % MIXER_SEQS == 0 else 1
    tblk = min(RWKV_BLOCK, seq)
    return _mixer_call(_rwkv_part(p3, nb, tblk, *params), bsz, seq, nb, tblk, "rwkv7", riders)


def _rope_table_kernel(pos_ref, invf_ref, cos_ref, sin_ref):
    ang = pos_ref[...].astype(F32) * invf_ref[...]
    cos_ref[...] = jnp.cos(ang)
    sin_ref[...] = jnp.sin(ang)


def _rope_tables(positions):
    bsz, seq = positions.shape
    d = HEAD_DIM
    w = 2 * d
    inv_freq = ROPE_THETA ** (-jnp.arange(0, ROPE_DIM, 2, dtype=F32) / ROPE_DIM)
    lane_d = np.arange(w) % d
    invf = jnp.where(lane_d < ROPE_DIM, inv_freq[lane_d % (ROPE_DIM // 2)], 0.0).reshape(1, w)
    tr = min(1024, bsz * seq)
    return pl.pallas_call(
        _rope_table_kernel,
        grid=(bsz * seq // tr,),
        in_specs=[pl.BlockSpec((tr, 1), lambda i: (i, 0)), pl.BlockSpec((1, w), lambda i: (0, 0))],
        out_specs=[pl.BlockSpec((tr, w), lambda i: (i, 0))] * 2,
        out_shape=[jax.ShapeDtypeStruct((bsz * seq, w), F32)] * 2,
        compiler_params=pltpu.CompilerParams(dimension_semantics=("parallel",)),
        name="rope_tables",
    )(positions.reshape(bsz * seq, 1), invf)


def _attn_kernel(cos_ref, sin_ref, q0_ref, q1_ref, kv_ref, qn_ref, kn_ref, sink_ref, bd_ref,
                 o_ref, kprev_ref, vprev_ref):
    nb, blk, _ = q0_ref.shape
    rows = nb * blk
    d = HEAD_DIM
    half = ROPE_DIM // 2
    group = ATTN_HEADS // ATTN_KV_HEADS
    kvw = ATTN_KV_HEADS * d
    first = pl.program_id(1) == 0
    ones_bd = bd_ref[...]
    cos = cos_ref[...].reshape(rows, 2 * d)
    sin = sin_ref[...].reshape(rows, 2 * d)

    def norm_rope(x, gain):
        w = x.shape[1]
        x = x * lax.rsqrt(_head_sum(x * x, ones_bd[:w, :w]) * (1.0 / d) + NORM_EPS) * gain
        ld = lax.broadcasted_iota(jnp.int32, (rows, w), 1) % d
        rot = jnp.where(ld < half, -pltpu.roll(x, w - half, axis=1),
                        jnp.where(ld < ROPE_DIM, pltpu.roll(x, half, axis=1), 0.0))
        tile = lambda t: t if w == t.shape[1] else jnp.concatenate([t] * (w // t.shape[1]), axis=1)
        return x * tile(cos) + rot * tile(sin)

    qs = [norm_rope(q_ref[...].astype(F32).reshape(rows, ATTN_COL_BLOCK), qn_ref[...]) * (d ** -0.5)
          for q_ref in (q0_ref, q1_ref)]
    kv = kv_ref[...].astype(F32).reshape(rows, ATTN_COL_BLOCK)
    k_cur = norm_rope(kv[:, :kvw], kn_ref[:, :kvw])
    v_cur = kv[:, kvw:]
    k_prev = kprev_ref[...]
    v_prev = vprev_ref[...]

    heads_per_ref = ATTN_COL_BLOCK // d
    q_head = lambda s, qh: qs[qh // heads_per_ref][s * blk:(s + 1) * blk,
                                                   (qh % heads_per_ref) * d:(qh % heads_per_ref + 1) * d]
    q_g = jnp.stack([jnp.concatenate([q_head(s, kh * group + i) for i in range(group)], axis=0)
                     for s in range(nb) for kh in range(ATTN_KV_HEADS)], axis=0)
    kv_heads = lambda t: jnp.stack([t[s * blk:(s + 1) * blk, kh * d:(kh + 1) * d]
                                    for s in range(nb) for kh in range(ATTN_KV_HEADS)], axis=0)

    ri = lax.broadcasted_iota(jnp.int32, (group * blk, blk), 0) % blk
    ci = lax.broadcasted_iota(jnp.int32, (group * blk, blk), 1)
    neg_inf = -jnp.inf
    no_prev = jnp.where(first, neg_inf, 0.0)
    s_prev = jnp.where(ci > ri, _bmm(q_g, kv_heads(k_prev), _BNT), neg_inf) + no_prev
    s_cur = jnp.where(ci <= ri, _bmm(q_g, kv_heads(k_cur), _BNT), neg_inf)
    sink = sink_ref[...]
    m = jnp.maximum(jnp.maximum(jnp.max(s_prev, axis=2, keepdims=True),
                                jnp.max(s_cur, axis=2, keepdims=True)), sink)
    p_prev = jnp.exp(s_prev - m)
    p_cur = jnp.exp(s_cur - m)
    ones = jnp.ones((nb * ATTN_KV_HEADS, blk, d), BF16)
    den = _bmm(p_prev, ones) + _bmm(p_cur, ones) + jnp.exp(sink - m)
    o = ((_bmm(p_prev, kv_heads(v_prev)) + _bmm(p_cur, kv_heads(v_cur))) / den).astype(o_ref.dtype)
    for s in range(nb):
        for qh in range(ATTN_HEADS):
            o_ref[s, :, qh * d:(qh + 1) * d] = o[s * ATTN_KV_HEADS + qh // group,
                                                 (qh % group) * blk:(qh % group + 1) * blk, :]

    kprev_ref[...] = k_cur
    vprev_ref[...] = v_cur


def _attn_part(p3, nb, rope_cos, rope_sin, q_norm, k_norm, sinks):
    bsz, seq, _ = p3.shape
    blk = ATTN_BLOCK
    cw = ATTN_COL_BLOCK
    d = HEAD_DIM
    qn = jnp.tile(q_norm.astype(F32), cw // d).reshape(1, cw)
    kn = jnp.tile(k_norm.astype(F32), cw // d).reshape(1, cw)
    ones_bd = jnp.asarray(np.kron(np.eye(cw // d, dtype=np.float32), np.ones((d, d), np.float32)), BF16)
    group = ATTN_HEADS // ATTN_KV_HEADS
    sink_col = jnp.repeat(sinks.astype(F32).reshape(ATTN_KV_HEADS, group), blk, axis=1)[:, :, None]
    sink_col = jnp.tile(sink_col, (nb, 1, 1))
    tw = rope_cos.shape[-1]
    cos3 = rope_cos.reshape(bsz, seq, tw)
    sin3 = rope_sin.reshape(bsz, seq, tw)
    col = lambda o: pl.BlockSpec((nb, blk, cw), lambda i, j: (i, j, ATTN_OFF_BLOCKS + o))
    trig = pl.BlockSpec((nb, blk, tw), lambda i, j: (i, j, 0))
    consts = [qn, kn, sink_col, ones_bd]
    in_specs = [trig, trig, col(0), col(1), col(2)] + [_full_spec(t) for t in consts]
    scratch = [pltpu.VMEM((nb * blk, ATTN_KV_HEADS * d), F32), pltpu.VMEM((nb * blk, ATTN_KV_HEADS * d), F32)]
    return _attn_kernel, in_specs, [cos3, sin3, p3, p3, p3] + consts, scratch


def _attn(p3, *params, riders=()):
    bsz, seq, _ = p3.shape
    nb = ATTN_SEQS if bsz % ATTN_SEQS == 0 else 1
    return _mixer_call(_attn_part(p3, nb, *params), bsz, seq, nb, ATTN_BLOCK, "swa_attn", riders)


def _s5_prep_kernel(lre_ref, lim_ref, ls_ref, bre_ref, bim_ref, are_ref, aim_ref, obre_ref, obim_ref):
    lre = lre_ref[...]
    lim = lim_ref[...]
    dt = jnp.exp(ls_ref[...])
    mag = jnp.exp(lre * dt)
    are = mag * jnp.cos(lim * dt)
    aim = mag * jnp.sin(lim * dt)
    are_ref[...] = are
    aim_ref[...] = aim
    inv = 1.0 / (lre * lre + lim * lim)
    cre = ((are - 1.0) * lre + aim * lim) * inv
    cim = (aim * lre - (are - 1.0) * lim) * inv
    bre = bre_ref[...]
    bim = bim_ref[...]
    obre_ref[...] = cre[:, None, :] * bre - cim[:, None, :] * bim
    obim_ref[...] = cre[:, None, :] * bim + cim[:, None, :] * bre


def _s5_kernel(u_ref, are_ref, aim_ref, b_ref, c_ref, d_ref, gw_ref, gb_ref, o_ref, x_ref, s_ref):
    bsz, tc, ch = u_ref.shape
    hw = S5_WIDTH // 2
    hc = ch // 2

    @pl.when(pl.program_id(0) == 0)
    def _():
        s_ref[...] = jnp.zeros_like(s_ref)

    u = jnp.swapaxes(u_ref[...].astype(F32), 0, 1).reshape(tc * bsz, ch)
    for hf in range(2):
        x_ref[:, 2 * hw * hf:2 * hw * (hf + 1)] = _mm(u[:, hc * hf:hc * (hf + 1)], b_ref[hf])

    for hf in range(2):
        for s0 in range(0, hw, S5_STRIP):
            re = slice(2 * hw * hf + s0, 2 * hw * hf + s0 + S5_STRIP)
            im = slice(2 * hw * hf + hw + s0, 2 * hw * hf + hw + s0 + S5_STRIP)
            lam = slice(hw * hf + s0, hw * hf + s0 + S5_STRIP)
            ar = jnp.broadcast_to(are_ref[:, lam], (bsz, S5_STRIP))
            ai = jnp.broadcast_to(aim_ref[:, lam], (bsz, S5_STRIP))

            def step(t, carry, re=re, im=im, ar=ar, ai=ai):
                sr, si = carry
                rows = pl.ds(pl.multiple_of(t * bsz, bsz), bsz)
                nr = ar * sr - ai * si + x_ref[rows, re]
                ni = ar * si + ai * sr + x_ref[rows, im]
                x_ref[rows, re] = nr
                x_ref[rows, im] = ni
                return nr, ni

            sr, si = lax.fori_loop(0, tc, step, (s_ref[:, re], s_ref[:, im]), unroll=4)
            s_ref[:, re] = sr
            s_ref[:, im] = si

    y = jnp.concatenate([_mm(x_ref[:, 2 * hw * hf:2 * hw * (hf + 1)], c_ref[hf]) for hf in range(2)], axis=1)
    y = y + d_ref[...] * u
    z = 0.5 * y * (1.0 + lax.erf(y * (2.0 ** -0.5)))
    out = z * jax.nn.sigmoid(_mm(z, gw_ref[...]) + gb_ref[...])
    o_ref[...] = jnp.swapaxes(out.reshape(tc, bsz, ch), 0, 1).astype(o_ref.dtype)


def _s5(p3, lam_re, lam_im, log_step, b_re, b_im, c_re, c_im, d_skip, glu_w, glu_b):
    bsz, seq, _ = p3.shape
    g, st, ch = S5_GROUPS, S5_STATE, S5_GROUP
    gwd = GROUP_WIDTH
    vm = pl.BlockSpec(memory_space=pltpu.VMEM)
    a_re, a_im, bb_re, bb_im = pl.pallas_call(
        _s5_prep_kernel,
        in_specs=[vm] * 5,
        out_specs=[vm] * 4,
        out_shape=[jax.ShapeDtypeStruct((g, st), F32)] * 2 + [jax.ShapeDtypeStruct((g, ch, st), F32)] * 2,
        name="s5_prep",
    )(lam_re, lam_im, log_step.reshape(g, 1), jnp.swapaxes(b_re, 1, 2), jnp.swapaxes(b_im, 1, 2))

    gh = g // 2
    eye = jnp.eye(gh, dtype=F32)
    bd_in = lambda t: (t[:, :, None, :] * eye[:, None, :, None]).reshape(gh * ch, gh * st)
    bd_out = lambda t: (jnp.swapaxes(t, 1, 2)[:, :, None, :] * eye[:, None, :, None]).reshape(gh * st, gh * ch)
    halves = lambda t: (t[:gh], t[gh:])
    b_mat = jnp.stack([jnp.concatenate([bd_in(r), bd_in(i)], axis=1)
                       for r, i in zip(halves(bb_re), halves(bb_im))]).astype(BF16)
    c_mat = jnp.stack([jnp.concatenate([bd_out(r), -bd_out(i)], axis=0)
                       for r, i in zip(halves(c_re.astype(F32)), halves(c_im.astype(F32)))]).astype(BF16)

    tc = min(64, seq)
    full = lambda t: pl.BlockSpec(t.shape, lambda i: (0,) * t.ndim)
    params = [a_re.reshape(1, S5_WIDTH), a_im.reshape(1, S5_WIDTH), b_mat, c_mat,
              d_skip.reshape(1, gwd), glu_w.astype(BF16), glu_b.reshape(1, gwd)]
    return pl.pallas_call(
        _s5_kernel,
        grid=(seq // tc,),
        in_specs=[pl.BlockSpec((bsz, tc, gwd), lambda i: (0, i, S5_OFF_BLOCKS))] + [full(t) for t in params],
        out_specs=pl.BlockSpec((bsz, tc, gwd), lambda i: (0, i, 0)),
        out_shape=jax.ShapeDtypeStruct((bsz, seq, gwd), BF16),
        scratch_shapes=[pltpu.VMEM((tc * bsz, 2 * S5_WIDTH), F32), pltpu.VMEM((bsz, 2 * S5_WIDTH), F32)],
        compiler_params=pltpu.CompilerParams(
            dimension_semantics=("arbitrary",), vmem_limit_bytes=VMEM_LIMIT),
        name="s5_scan",
    )(p3, *params)


def _hgrn_kernel(qf_ref, ig_ref, lbraw_ref, gn_ref, tri_ref, bd_ref, o_ref, s_ref, osc_ref, *, layer):
    nb, tb, _ = qf_ref.shape
    cz = HGRN_CHUNK
    gw = GROUP_WIDTH
    n = HEAD_DIM
    nh = gw // n

    qf = qf_ref[...].astype(F32).reshape(nb * tb, 2 * gw)
    ig = ig_ref[...].astype(F32).reshape(nb * tb, 2 * gw)
    q, f = qf[:, :gw], qf[:, gw:]
    v, g = ig[:, :gw], ig[:, gw:]

    lbr = lbraw_ref[...]
    e = jnp.exp(lbr - jnp.max(lbr, axis=0, keepdims=True))
    sm = e / jnp.sum(e, axis=0, keepdims=True)
    lb = jnp.zeros((1, gw), F32)
    for i in range(1, layer + 1):
        lb = lb + sm[i:i + 1, :]

    q = _silu(q) * (n ** -0.5)
    f_gate = lb + (1.0 - lb) * jax.nn.sigmoid(f)
    log_f = jnp.log(f_gate)
    k = 1.0 - f_gate

    parts = _split2(log_f)

    def per_seq(c_ref):
        cm = c_ref[...]
        seq_rows = lambda t, s: t[s * tb:(s + 1) * tb, :]
        return jnp.concatenate(
            [_dot(cm, seq_rows(parts[0], s), _NN) + _dot(cm, seq_rows(parts[1], s), _NN)
             for s in range(nb)], axis=0)

    bcum = per_seq(tri_ref)
    b3 = bcum.reshape(nb * tb // cz, cz, gw)
    chunk_row = lambda r: jnp.broadcast_to(b3[:, r:r + 1, :], b3.shape).reshape(nb * tb, gw)
    bmid = chunk_row(cz // 2 - 1)
    blast = chunk_row(cz - 1)
    qe = q * jnp.exp(bcum - bmid)
    ke = k * jnp.exp(bmid - bcum)
    kl = k * jnp.exp(blast - bcum)
    qb = q * jnp.exp(bcum)
    dec = jnp.exp(blast)

    ri = lax.broadcasted_iota(jnp.int32, (tb, tb), 0)
    ci = lax.broadcasted_iota(jnp.int32, (tb, tb), 1)
    mask = jnp.logical_and(ri // cz == ci // cz, ri >= ci)

    heads = lambda t: jnp.stack(
        [t[s * tb:(s + 1) * tb, h * n:(h + 1) * n] for s in range(nb) for h in range(nh)], axis=0)
    v_h, qb_h, kl_h, dec_h = heads(v), heads(qb), heads(kl), heads(dec)
    att = jnp.where(mask, _bmm(heads(qe), heads(ke), _BNT), 0.0)
    o_intra = _bmm(att, v_h)
    nchunk = tb // cz
    ti = lax.broadcasted_iota(jnp.int32, (tb, nchunk * n), 0)
    li = lax.broadcasted_iota(jnp.int32, (tb, nchunk * n), 1)
    kl_spread = jnp.where(ti // cz == li // n, jnp.concatenate([kl_h] * nchunk, axis=2), 0.0)
    kv_all = _bmm(v_h, kl_spread, _BTN)
    st = s_ref[...]
    o_inter = []
    for j in range(nchunk):
        rows = slice(j * cz, (j + 1) * cz)
        o_inter.append(_bmm(qb_h[:, rows, :], st, _BNT))
        st = st * dec_h[:, j * cz:j * cz + 1, :] + kv_all[:, :, j * n:(j + 1) * n]
    s_ref[...] = st
    o_heads = o_intra + jnp.concatenate(o_inter, axis=1)
    for s in range(nb):
        for h in range(nh):
            osc_ref[s * tb:(s + 1) * tb, h * n:(h + 1) * n] = o_heads[s * nh + h]

    o = osc_ref[...]
    ms = _head_sum(o * o, bd_ref[...]) * (1.0 / n)
    o_ref[...] = (o * lax.rsqrt(ms + NORM_EPS) * gn_ref[...] * _silu(g)).reshape(nb, tb, gw).astype(o_ref.dtype)


def _hgrn_part(p3, nb, lower_bounds, g_norm, layer):
    tb = HGRN_TILE
    cz = HGRN_CHUNK
    gw = GROUP_WIDTH
    idx = np.arange(tb)
    same = (idx[:, None] // cz) == (idx[None, :] // cz)
    tri = same & (idx[:, None] >= idx[None, :])
    consts = [jnp.asarray(tri.astype(np.float32), BF16)]
    ones_bd = jnp.asarray(np.kron(np.eye(4, dtype=np.float32), np.ones((HEAD_DIM, HEAD_DIM), np.float32)), BF16)
    gn = jnp.tile(g_norm.astype(F32), gw // HEAD_DIM).reshape(1, gw)
    params = [lower_bounds.astype(F32), gn] + consts + [ones_bd]
    col = lambda o: pl.BlockSpec((nb, tb, 2 * gw), lambda i, j: (i, j, HGRN_OFF_BLOCKS + o))
    in_specs = [col(0), col(1)] + [_full_spec(t) for t in params]
    scratch = [pltpu.VMEM((nb * gw // HEAD_DIM, HEAD_DIM, HEAD_DIM), F32), pltpu.VMEM((nb * tb, gw), F32)]
    return functools.partial(_hgrn_kernel, layer=layer), in_specs, [p3, p3] + params, scratch


def _hgrn(p3, *params, riders=()):
    bsz, seq, _ = p3.shape
    nb = HGRN_SEQS if bsz % HGRN_SEQS == 0 else 1
    return _mixer_call(_hgrn_part(p3, nb, *params), bsz, seq, nb, HGRN_TILE, "hgrn2", riders)


def kernel(x, positions, ffn1_norm, ffn1_w_gate, ffn1_w_up, ffn1_w_down, mix_norm, w_in, rwkv_mu, rwkv_w0, rwkv_w_up, rwkv_a0, rwkv_a_up, rwkv_g_up, rwkv_k_k, rwkv_k_a, rwkv_r_k, rwkv_ln_w, rwkv_ln_b, attn_q_norm, attn_k_norm, attn_sinks, s5_lambda_re, s5_lambda_im, s5_log_step, s5_b_re, s5_b_im, s5_c_re, s5_c_im, s5_d, s5_glu_w, s5_glu_b, hgrn_lower_bounds, hgrn_g_norm, w_out, ffn2_norm, ffn2_w_gate, ffn2_w_up, ffn2_w_down):
    bsz, seq, d = x.shape
    depth = w_in.shape[0]
    n = bsz * seq
    xf = x.reshape(n, d)
    rope_cos, rope_sin = _rope_tables(positions)
    ffn_w32 = []
    for l in range(depth):
        ffn_w32.append([(w, l) for w in (ffn1_w_gate, ffn1_w_up, ffn1_w_down)])
        ffn_w32.append([(w, l) for w in (ffn2_w_gate, ffn2_w_up, ffn2_w_down)])
    ffn_w32.append([])
    w16 = [w[l].astype(BF16) for w, l in ffn_w32[0]]
    w_in16 = w_in[0].astype(BF16)
    for l in range(depth):
        xf, w16_next = _ffn(xf, ffn1_norm[l], *w16, riders=ffn_w32[2 * l + 1])
        p3 = _proj(xf, mix_norm[l], w_in16).reshape(bsz, seq, D_IN)
        y_a, (w_out16,) = _rwkv(p3, rwkv_mu[l], rwkv_w0[l], rwkv_w_up[l], rwkv_a0[l], rwkv_a_up[l], rwkv_g_up[l],
                                rwkv_k_k[l], rwkv_k_a[l], rwkv_r_k[l].reshape(-1), rwkv_ln_w[l], rwkv_ln_b[l],
                                riders=[(w_out, l)])
        y_b, _ = _attn(p3, rope_cos, rope_sin, attn_q_norm[l], attn_k_norm[l], attn_sinks[l])
        y_c = _s5(p3, s5_lambda_re[l], s5_lambda_im[l], s5_log_step[l], s5_b_re[l], s5_b_im[l],
                  s5_c_re[l], s5_c_im[l], s5_d[l], s5_glu_w[l], s5_glu_b[l])
        y_d, next_in = _hgrn(p3, hgrn_lower_bounds, hgrn_g_norm[l], l,
                             riders=[(w_in, l + 1)] if l + 1 < depth else [])
        if next_in:
            w_in16 = next_in[0]
        ys = [t.reshape(n, GROUP_WIDTH) for t in (y_a, y_b, y_c, y_d)]
        xf = _outproj(xf, ys, w_out16)
        xf, w16 = _ffn(xf, ffn2_norm[l], *w16_next, riders=ffn_w32[2 * l + 2])
    return xf.reshape(bsz, seq, d)
```

```python
import functools

import jax
import jax.numpy as jnp
import numpy as np
from jax import lax
from jax.experimental import pallas as pl
from jax.experimental.pallas import tpu as pltpu

F32 = jnp.float32
BF16 = jnp.bfloat16

HEAD_DIM = 64
GROUP_WIDTH = 512
NORM_EPS = 1e-6
FFN_RES_WEIGHT = 0.5

RWKV_W_RANK = 64
RWKV_A_RANK = 64
RWKV_G_RANK = 128
RWKV_LN_EPS = 64e-5
RWKV_IN = 3 * GROUP_WIDTH + RWKV_W_RANK + RWKV_A_RANK + RWKV_G_RANK
RWKV_CHUNK = 64
RWKV_BLOCK = 128
MIXER_SEQS = 4

ATTN_HEADS = 8
ATTN_KV_HEADS = 2
ATTN_BLOCK = 128
ATTN_SEQS = 8
ROPE_THETA = 500000.0
ROPE_DIM = HEAD_DIM // 4
ATTN_COL_BLOCK = 256
ATTN_OFF_BLOCKS = RWKV_IN // ATTN_COL_BLOCK

S5_GROUP = 16
S5_GROUPS = GROUP_WIDTH // S5_GROUP
S5_STATE = 64
S5_WIDTH = S5_GROUPS * S5_STATE
S5_OFF_BLOCKS = (RWKV_IN + 768) // GROUP_WIDTH
S5_STRIP = 512

HGRN_CHUNK = 16
HGRN_TILE = 128
HGRN_SEQS = 8
HGRN_OFF_BLOCKS = 3

D_IN = 5120
VMEM_LIMIT = 56 * 1024 * 1024


def _dot(a, b, dims):
    return lax.dot_general(a, b, (dims, ((), ())), preferred_element_type=F32)


_NN = ((1,), (0,))


def _mm(a, b):
    return _dot(a.astype(BF16), b.astype(BF16), _NN)


_BNN = (((2,), (1,)), ((0,), (0,)))
_BNT = (((2,), (2,)), ((0,), (0,)))
_BTN = (((1,), (1,)), ((0,), (0,)))


def _bmm(a, b, dims=_BNN):
    return lax.dot_general(a.astype(BF16), b.astype(BF16), dims, preferred_element_type=F32)


def _split2(x):
    hi = x.astype(BF16)
    lo = (x - hi.astype(F32)).astype(BF16)
    return hi, lo


def _const_lhs_mm(c, x):
    hi, lo = _split2(x)
    return _dot(c, hi, _NN) + _dot(c, lo, _NN)


def _const_rhs_mm(x, c):
    hi, lo = _split2(x)
    return _dot(hi, c, _NN) + _dot(lo, c, _NN)


def _head_sum(x, ones_bd):
    w = ones_bd.shape[0]
    parts = [_const_rhs_mm(x[:, i:i + w], ones_bd) for i in range(0, x.shape[1], w)]
    return parts[0] if len(parts) == 1 else jnp.concatenate(parts, axis=1)


def _silu(x):
    return x * jax.nn.sigmoid(x)


def _rms_rows(x, gain):
    ms = jnp.mean(x * x, axis=-1, keepdims=True)
    return x * lax.rsqrt(ms + NORM_EPS) * gain


def _ffn_kernel(x_ref, g_ref, wg_ref, wu_ref, wd_ref, *rest):
    n_riders = (len(rest) - 2) // 2
    o_ref, h_ref = rest[n_riders], rest[-1]

    @pl.when(pl.program_id(1) == 0)
    def _():
        x = x_ref[...]
        h_ref[...] = _rms_rows(x, g_ref[...]).astype(BF16)
        o_ref[...] = x

    h = h_ref[...]
    gate = jnp.dot(h, wg_ref[...], preferred_element_type=F32)
    up = jnp.dot(h, wu_ref[...], preferred_element_type=F32)
    act = (_silu(gate) * up).astype(BF16)
    o_ref[...] += FFN_RES_WEIGHT * jnp.dot(act, wd_ref[...], preferred_element_type=F32)
    for src, dst in zip(rest[:n_riders], rest[n_riders + 1:-1]):
        dst[...] = src[...].astype(BF16)


def _rider_specs(shape, layer, gi, gj):
    _, r, c = shape
    steps = gi * gj
    if r % steps == 0 and (r // steps) % 16 == 0:
        blk, pos = (r // steps, c), lambda i, j: (i * gj + j, 0)
    elif r % gi == 0 and (r // gi) % 16 == 0 and c % gj == 0 and (c // gj) % 128 == 0:
        blk, pos = (r // gi, c // gj), lambda i, j: (i, j)
    else:
        return None
    return pl.BlockSpec((None,) + blk, lambda i, j: (layer,) + pos(i, j)), pl.BlockSpec(blk, pos)


def _ffn(x2d, gain, wg, wu, wd, riders=()):
    n, d = x2d.shape
    f = wg.shape[1]
    tm = min(1024, n)
    tf = 512 if f % 512 == 0 else f
    gi, gj = n // tm, f // tf
    specs = [_rider_specs(w.shape, layer, gi, gj) for w, layer in riders]
    ride = [k for k, s in enumerate(specs) if s is not None]
    outs = pl.pallas_call(
        _ffn_kernel,
        grid=(gi, gj),
        in_specs=[
            pl.BlockSpec((tm, d), lambda i, j: (i, 0)),
            pl.BlockSpec((1, d), lambda i, j: (0, 0)),
            pl.BlockSpec((d, tf), lambda i, j: (0, j)),
            pl.BlockSpec((d, tf), lambda i, j: (0, j)),
            pl.BlockSpec((tf, d), lambda i, j: (j, 0)),
        ] + [specs[k][0] for k in ride],
        out_specs=[pl.BlockSpec((tm, d), lambda i, j: (i, 0))] + [specs[k][1] for k in ride],
        out_shape=[jax.ShapeDtypeStruct((n, d), F32)]
        + [jax.ShapeDtypeStruct(riders[k][0].shape[1:], BF16) for k in ride],
        scratch_shapes=[pltpu.VMEM((tm, d), BF16)],
        compiler_params=pltpu.CompilerParams(
            dimension_semantics=("parallel", "arbitrary"), vmem_limit_bytes=VMEM_LIMIT),
        name="ffn",
    )(x2d, gain.reshape(1, d), wg, wu, wd, *[riders[k][0] for k in ride])
    cast = [outs[1 + ride.index(k)] if k in ride else w[layer].astype(BF16)
            for k, (w, layer) in enumerate(riders)]
    return outs[0], cast


def _proj_kernel(x_ref, g_ref, w_ref, o_ref, h_ref):
    @pl.when(pl.program_id(1) == 0)
    def _():
        h_ref[...] = _rms_rows(x_ref[...], g_ref[...]).astype(BF16)

    o_ref[...] = jnp.dot(h_ref[...], w_ref[...], preferred_element_type=F32).astype(o_ref.dtype)


def _proj(x2d, gain, w):
    n, d = x2d.shape
    dout = w.shape[1]
    tm = min(1024, n)
    tn = dout // 2
    return pl.pallas_call(
        _proj_kernel,
        grid=(n // tm, dout // tn),
        in_specs=[
            pl.BlockSpec((tm, d), lambda i, j: (i, 0)),
            pl.BlockSpec((1, d), lambda i, j: (0, 0)),
            pl.BlockSpec((d, tn), lambda i, j: (0, j)),
        ],
        out_specs=pl.BlockSpec((tm, tn), lambda i, j: (i, j)),
        out_shape=jax.ShapeDtypeStruct((n, dout), BF16),
        scratch_shapes=[pltpu.VMEM((tm, d), BF16)],
        compiler_params=pltpu.CompilerParams(
            dimension_semantics=("parallel", "arbitrary"), vmem_limit_bytes=VMEM_LIMIT),
        name="in_proj",
    )(x2d, gain.reshape(1, d), w)


def _outproj_kernel(x_ref, ya_ref, yb_ref, yc_ref, yd_ref, w_ref, o_ref):
    gw = GROUP_WIDTH
    acc = x_ref[...]
    for m, y_ref in enumerate((ya_ref, yb_ref, yc_ref, yd_ref)):
        acc = acc + jnp.dot(y_ref[...].astype(BF16), w_ref[m * gw:(m + 1) * gw, :],
                            preferred_element_type=F32)
    o_ref[...] = acc


def _outproj(x2d, ys, w):
    n, d = x2d.shape
    tm = min(512, n)
    yspec = pl.BlockSpec((tm, GROUP_WIDTH), lambda i: (i, 0))
    return pl.pallas_call(
        _outproj_kernel,
        grid=(n // tm,),
        in_specs=[pl.BlockSpec((tm, d), lambda i: (i, 0)), yspec, yspec, yspec, yspec,
                  pl.BlockSpec(w.shape, lambda i: (0, 0))],
        out_specs=pl.BlockSpec((tm, d), lambda i: (i, 0)),
        out_shape=jax.ShapeDtypeStruct((n, d), F32),
        compiler_params=pltpu.CompilerParams(
            dimension_semantics=("parallel",), vmem_limit_bytes=VMEM_LIMIT),
        name="out_proj",
    )(x2d, *ys, w)


def _advance(fillers):
    for gen in fillers:
        if next(gen, _DONE) is not _DONE:
            return


_DONE = object()


def _rwkv_kernel(p_ref, mu_ref, w0_ref, wup_ref, a0_ref, aup_ref, gup_ref, kk_ref, ka_ref, rk_ref,
                 lnw_ref, lnb_ref, tri_ref, bd_ref, o_ref, s_ref, prev_ref, osc_ref):
    c = RWKV_CHUNK
    nck = p_ref.shape[1] // c
    prm = dict(mu=mu_ref, w0=w0_ref, wup=wup_ref, a0=a0_ref, aup=aup_ref, gup=gup_ref, kk=kk_ref, ka=ka_ref,
               rk=rk_ref, lnw=lnw_ref, lnb=lnb_ref, tri=tri_ref, bd=bd_ref)
    chunk = lambda ref, ck: ref.at[:, pl.ds(ck * c, c), :]
    preps = [dict() for _ in range(nck)]
    for _ in _rwkv_prologue(chunk(p_ref, 0), prm, prev_ref, preps[0]):
        pass
    epilogue = iter(())
    for ck in range(nck):
        fillers = [epilogue]
        if ck + 1 < nck:
            fillers.append(_rwkv_prologue(chunk(p_ref, ck + 1), prm, prev_ref, preps[ck + 1]))
        y = _rwkv_solve(preps[ck], s_ref, functools.partial(_advance, fillers))
        for gen in fillers:
            for _ in gen:
                pass
        epilogue = _rwkv_epilogue(preps[ck], y, prm, chunk(o_ref, ck), osc_ref)
    for _ in epilogue:
        pass


def _rwkv_prologue(p_ref, prm, prev_ref, out):
    nb, c, _ = p_ref.shape
    gw = GROUP_WIDTH
    n = HEAD_DIM
    nh = gw // n

    p = p_ref[...].astype(F32).reshape(nb * c, RWKV_IN)
    row = lax.broadcasted_iota(jnp.int32, p.shape, 0)
    shifted = pltpu.roll(p, 1, axis=0)
    for s in range(nb):
        shifted = jnp.where(row == s * c, prev_ref[s], shifted)
        prev_ref[s] = p[(s + 1) * c - 1:(s + 1) * c, :]
    p = p + (shifted - p) * prm["mu"][...]
    yield

    r = p[:, 0:gw]
    k = p[:, gw:2 * gw]
    v = p[:, 2 * gw:3 * gw]
    o1 = 3 * gw
    w_lo = p[:, o1:o1 + RWKV_W_RANK]
    a_lo = p[:, o1 + RWKV_W_RANK:o1 + RWKV_W_RANK + RWKV_A_RANK]
    g_lo = p[:, o1 + RWKV_W_RANK + RWKV_A_RANK:]

    z = -(prm["w0"][...] + _mm(jnp.tanh(w_lo), prm["wup"][...]))
    softplus = jnp.maximum(z, 0.0) + jnp.log1p(jnp.exp(-jnp.abs(z)))
    lw = -jnp.exp(-softplus - 0.5)
    yield
    a = jax.nn.sigmoid(prm["a0"][...] + _mm(a_lo, prm["aup"][...]))
    g = _mm(jax.nn.sigmoid(g_lo), prm["gup"][...])
    yield

    kk = k * prm["kk"][...]
    kk = kk * lax.rsqrt(jnp.maximum(_head_sum(kk * kk, prm["bd"][...]), 1e-24))
    yield
    k = k * (1.0 + (a - 1.0) * prm["ka"][...])
    b = kk * a
    cum = _const_lhs_mm(prm["tri"][...], lw)
    yield
    cum_last = jnp.concatenate(
        [jnp.broadcast_to(cum[(s + 1) * c - 1:(s + 1) * c, :], (c, gw)) for s in range(nb)], axis=0)
    heads = lambda t, rows=c: jnp.stack(
        [t[s * c:s * c + rows, h * n:(h + 1) * n] for s in range(nb) for h in range(nh)], axis=0)
    mxu_heads = lambda t: heads(t.astype(BF16))
    out.update(r=r, k=k, v=v, g=g)
    e_neg = jnp.exp(-cum)
    out["bd_h"] = mxu_heads(b * e_neg)
    yield
    out["kd_h"] = mxu_heads(k * e_neg)
    yield
    out["kq_h"] = mxu_heads(kk * jnp.exp(cum - lw))
    yield
    out["rq_h"] = mxu_heads(r * jnp.exp(cum))
    yield
    out["v_h"] = mxu_heads(v)
    yield
    e_end = jnp.exp(cum_last - cum)
    out["be_h"] = mxu_heads(b * e_end)
    yield
    out["ke_h"] = mxu_heads(k * e_end)
    out["ge_h"] = heads(jnp.exp(cum_last), 1)
    yield


def _rwkv_solve(pp, s_ref, tick):
    kq_h, rq_h, bd_h, kd_h, v_h = pp["kq_h"], pp["rq_h"], pp["bd_h"], pp["kd_h"], pp["v_h"]
    c = kq_h.shape[1]
    ri = lax.broadcasted_iota(jnp.int32, (c, c), 0)
    ci = lax.broadcasted_iota(jnp.int32, (c, c), 1)
    strict = ri > ci
    incl = ri >= ci
    eye = (ri == ci).astype(F32)

    s0 = s_ref[...]
    gmat = _bmm(jnp.concatenate([kq_h, rq_h], axis=1), jnp.concatenate([bd_h, kd_h], axis=1), _BNT)
    tick()
    a_bb = jnp.where(strict, gmat[:, :c, :c], 0.0)
    a_bk = jnp.where(strict, gmat[:, :c, c:], 0.0)
    a_rb = jnp.where(incl, gmat[:, c:, :c], 0.0)
    a_rk = jnp.where(incl, gmat[:, c:, c:], 0.0)
    pw = -a_bb
    t_inv = eye + pw
    for _ in range(int(np.log2(c)) - 1):
        pw = _bmm(pw, pw)
        tick()
        t_inv = t_inv + _bmm(t_inv, pw)
        tick()
    x = _bmm(kq_h, s0, _BNT)
    tick()
    x = x + _bmm(a_bk, v_h)
    tick()
    u = -_bmm(t_inv, x)
    tick()
    y = _bmm(rq_h, s0, _BNT)
    tick()
    y = y + _bmm(a_rb, u)
    tick()
    y = y + _bmm(a_rk, v_h)
    tick()
    s_ref[...] = s0 * pp["ge_h"] + _bmm(jnp.concatenate([u.astype(BF16), v_h], axis=1),
                                        jnp.concatenate([pp["be_h"], pp["ke_h"]], axis=1), _BTN)
    tick()
    return y


def _rwkv_epilogue(pp, y, prm, o_ref, osc_ref):
    nb, c, gw = o_ref.shape
    n = HEAD_DIM
    nh = gw // n
    ones_bd = prm["bd"][...]
    for s in range(nb):
        for h in range(nh):
            osc_ref[s * c:(s + 1) * c, h * n:(h + 1) * n] = y[s * nh + h]
    yield
    o = osc_ref[...]
    inv_n = 1.0 / n
    mean = _head_sum(o, ones_bd) * inv_n
    yield
    dlt = o - mean
    var = _head_sum(dlt * dlt, ones_bd) * inv_n
    yield
    o = dlt * lax.rsqrt(var + RWKV_LN_EPS) * prm["lnw"][...] + prm["lnb"][...]
    yield
    bonus = _head_sum(pp["r"] * pp["k"] * prm["rk"][...], ones_bd) * pp["v"]
    yield
    o_ref[...] = ((o + bonus) * pp["g"]).reshape(nb, c, gw).astype(o_ref.dtype)
    yield


def _full_spec(t):
    return pl.BlockSpec(t.shape, lambda i, j: (0,) * t.ndim)


def _mixer_call(part, bsz, seq, nb, tblk, name, riders=()):
    body, in_specs, operands, scratch = part
    gi, gj = bsz // nb, seq // tblk
    specs = [_rider_specs(w.shape, layer, gi, gj) for w, layer in riders]
    ride = [k for k, s in enumerate(specs) if s is not None]
    n_in, n_ride = len(operands), len(ride)

    def kern(*refs):
        ins, rider_in = refs[:n_in], refs[n_in:n_in + n_ride]
        out, rider_out = refs[n_in + n_ride], refs[n_in + n_ride + 1:n_in + 2 * n_ride + 1]
        scr = refs[n_in + 2 * n_ride + 1:]

        @pl.when(pl.program_id(1) == 0)
        def _():
            for ref in scr:
                ref[...] = jnp.zeros_like(ref)

        body(*ins, out, *scr)
        for src, dst in zip(rider_in, rider_out):
            dst[...] = src[...].astype(BF16)

    out_spec = pl.BlockSpec((nb, tblk, GROUP_WIDTH), lambda i, j: (i, j, 0))
    outs = pl.pallas_call(
        kern,
        grid=(gi, gj),
        in_specs=list(in_specs) + [specs[k][0] for k in ride],
        out_specs=[out_spec] + [specs[k][1] for k in ride],
        out_shape=[jax.ShapeDtypeStruct((bsz, seq, GROUP_WIDTH), BF16)]
        + [jax.ShapeDtypeStruct(riders[k][0].shape[1:], BF16) for k in ride],
        scratch_shapes=scratch,
        compiler_params=pltpu.CompilerParams(
            dimension_semantics=("parallel", "arbitrary"), vmem_limit_bytes=VMEM_LIMIT),
        name=name,
    )(*operands, *[riders[k][0] for k in ride])
    cast = [outs[1 + ride.index(k)] if k in ride else w[layer].astype(BF16)
            for k, (w, layer) in enumerate(riders)]
    return outs[0], cast


def _rwkv_part(p3, nb, tblk, mu, w0, w_up, a0, a_up, g_up, k_k, k_a, r_k, ln_w, ln_b):
    c = RWKV_CHUNK
    gw = GROUP_WIDTH
    tri = jnp.asarray(np.kron(np.eye(nb, dtype=np.float32), np.tril(np.ones((c, c), np.float32))), BF16)
    ones_bd = jnp.asarray(np.kron(np.eye(4, dtype=np.float32), np.ones((HEAD_DIM, HEAD_DIM), np.float32)), BF16)
    row = lambda t: t.reshape(1, -1)
    params = [row(mu), row(w0), w_up, row(a0), a_up, g_up, row(k_k), row(k_a), row(r_k), row(ln_w),
              row(ln_b), tri, ones_bd]
    in_specs = [pl.BlockSpec((nb, tblk, RWKV_IN), lambda i, j: (i, j, 0))] + [_full_spec(t) for t in params]
    scratch = [pltpu.VMEM((nb * gw // HEAD_DIM, HEAD_DIM, HEAD_DIM), F32),
               pltpu.VMEM((nb, 1, RWKV_IN), F32),
               pltpu.VMEM((nb * c, gw), F32)]
    return _rwkv_kernel, in_specs, [p3] + params, scratch


def _rwkv(p3, *params, riders=()):
    bsz, seq, _ = p3.shape
    nb = MIXER_SEQS if bsz % MIXER_SEQS == 0 else 1
    tblk = min(RWKV_BLOCK, seq)
    return _mixer_call(_rwkv_part(p3, nb, tblk, *params), bsz, seq, nb, tblk, "rwkv7", riders)


def _rope_table_kernel(pos_ref, invf_ref, cos_ref, sin_ref):
    ang = pos_ref[...].astype(F32) * invf_ref[...]
    cos_ref[...] = jnp.cos(ang)
    sin_ref[...] = jnp.sin(ang)


def _rope_tables(positions):
    bsz, seq = positions.shape
    d = HEAD_DIM
    w = 2 * d
    inv_freq = ROPE_THETA ** (-jnp.arange(0, ROPE_DIM, 2, dtype=F32) / ROPE_DIM)
    lane_d = np.arange(w) % d
    invf = jnp.where(lane_d < ROPE_DIM, inv_freq[lane_d % (ROPE_DIM // 2)], 0.0).reshape(1, w)
    tr = min(1024, bsz * seq)
    return pl.pallas_call(
        _rope_table_kernel,
        grid=(bsz * seq // tr,),
        in_specs=[pl.BlockSpec((tr, 1), lambda i: (i, 0)), pl.BlockSpec((1, w), lambda i: (0, 0))],
        out_specs=[pl.BlockSpec((tr, w), lambda i: (i, 0))] * 2,
        out_shape=[jax.ShapeDtypeStruct((bsz * seq, w), F32)] * 2,
        compiler_params=pltpu.CompilerParams(dimension_semantics=("parallel",)),
        name="rope_tables",
    )(positions.reshape(bsz * seq, 1), invf)


def _attn_kernel(cos_ref, sin_ref, q0_ref, q1_ref, kv_ref, qn_ref, kn_ref, sink_ref, bd_ref,
                 o_ref, kprev_ref, vprev_ref):
    nb, blk, _ = q0_ref.shape
    rows = nb * blk
    d = HEAD_DIM
    half = ROPE_DIM // 2
    group = ATTN_HEADS // ATTN_KV_HEADS
    kvw = ATTN_KV_HEADS * d
    first = pl.program_id(1) == 0
    ones_bd = bd_ref[...]
    cos = cos_ref[...].reshape(rows, 2 * d)
    sin = sin_ref[...].reshape(rows, 2 * d)

    def norm_rope(x, gain):
        w = x.shape[1]
        x = x * lax.rsqrt(_head_sum(x * x, ones_bd[:w, :w]) * (1.0 / d) + NORM_EPS) * gain
        ld = lax.broadcasted_iota(jnp.int32, (rows, w), 1) % d
        rot = jnp.where(ld < half, -pltpu.roll(x, w - half, axis=1),
                        jnp.where(ld < ROPE_DIM, pltpu.roll(x, half, axis=1), 0.0))
        tile = lambda t: t if w == t.shape[1] else jnp.concatenate([t] * (w // t.shape[1]), axis=1)
        return x * tile(cos) + rot * tile(sin)

    qs = [norm_rope(q_ref[...].astype(F32).reshape(rows, ATTN_COL_BLOCK), qn_ref[...]) * (d ** -0.5)
          for q_ref in (q0_ref, q1_ref)]
    kv = kv_ref[...].astype(F32).reshape(rows, ATTN_COL_BLOCK)
    k_cur = norm_rope(kv[:, :kvw], kn_ref[:, :kvw])
    v_cur = kv[:, kvw:]
    k_prev = kprev_ref[...]
    v_prev = vprev_ref[...]

    heads_per_ref = ATTN_COL_BLOCK // d
    q_head = lambda s, qh: qs[qh // heads_per_ref][s * blk:(s + 1) * blk,
                                                   (qh % heads_per_ref) * d:(qh % heads_per_ref + 1) * d]
    q_g = jnp.stack([jnp.concatenate([q_head(s, kh * group + i) for i in range(group)], axis=0)
                     for s in range(nb) for kh in range(ATTN_KV_HEADS)], axis=0)
    kv_heads = lambda t: jnp.stack([t[s * blk:(s + 1) * blk, kh * d:(kh + 1) * d]
                                    for s in range(nb) for kh in range(ATTN_KV_HEADS)], axis=0)

    ri = lax.broadcasted_iota(jnp.int32, (group * blk, blk), 0) % blk
    ci = lax.broadcasted_iota(jnp.int32, (group * blk, blk), 1)
    neg_inf = -jnp.inf
    no_prev = jnp.where(first, neg_inf, 0.0)
    s_prev = jnp.where(ci > ri, _bmm(q_g, kv_heads(k_prev), _BNT), neg_inf) + no_prev
    s_cur = jnp.where(ci <= ri, _bmm(q_g, kv_heads(k_cur), _BNT), neg_inf)
    sink = sink_ref[...]
    m = jnp.maximum(jnp.maximum(jnp.max(s_prev, axis=2, keepdims=True),
                                jnp.max(s_cur, axis=2, keepdims=True)), sink)
    p_prev = jnp.exp(s_prev - m)
    p_cur = jnp.exp(s_cur - m)
    ones = jnp.ones((nb * ATTN_KV_HEADS, blk, d), BF16)
    den = _bmm(p_prev, ones) + _bmm(p_cur, ones) + jnp.exp(sink - m)
    o = ((_bmm(p_prev, kv_heads(v_prev)) + _bmm(p_cur, kv_heads(v_cur))) / den).astype(o_ref.dtype)
    for s in range(nb):
        for qh in range(ATTN_HEADS):
            o_ref[s, :, qh * d:(qh + 1) * d] = o[s * ATTN_KV_HEADS + qh // group,
                                                 (qh % group) * blk:(qh % group + 1) * blk, :]

    kprev_ref[...] = k_cur
    vprev_ref[...] = v_cur


def _attn_part(p3, nb, rope_cos, rope_sin, q_norm, k_norm, sinks):
    bsz, seq, _ = p3.shape
    blk = ATTN_BLOCK
    cw = ATTN_COL_BLOCK
    d = HEAD_DIM
    qn = jnp.tile(q_norm.astype(F32), cw // d).reshape(1, cw)
    kn = jnp.tile(k_norm.astype(F32), cw // d).reshape(1, cw)
    ones_bd = jnp.asarray(np.kron(np.eye(cw // d, dtype=np.float32), np.ones((d, d), np.float32)), BF16)
    group = ATTN_HEADS // ATTN_KV_HEADS
    sink_col = jnp.repeat(sinks.astype(F32).reshape(ATTN_KV_HEADS, group), blk, axis=1)[:, :, None]
    sink_col = jnp.tile(sink_col, (nb, 1, 1))
    tw = rope_cos.shape[-1]
    cos3 = rope_cos.reshape(bsz, seq, tw)
    sin3 = rope_sin.reshape(bsz, seq, tw)
    col = lambda o: pl.BlockSpec((nb, blk, cw), lambda i, j: (i, j, ATTN_OFF_BLOCKS + o))
    trig = pl.BlockSpec((nb, blk, tw), lambda i, j: (i, j, 0))
    consts = [qn, kn, sink_col, ones_bd]
    in_specs = [trig, trig, col(0), col(1), col(2)] + [_full_spec(t) for t in consts]
    scratch = [pltpu.VMEM((nb * blk, ATTN_KV_HEADS * d), F32), pltpu.VMEM((nb * blk, ATTN_KV_HEADS * d), F32)]
    return _attn_kernel, in_specs, [cos3, sin3, p3, p3, p3] + consts, scratch


def _attn(p3, *params, riders=()):
    bsz, seq, _ = p3.shape
    nb = ATTN_SEQS if bsz % ATTN_SEQS == 0 else 1
    return _mixer_call(_attn_part(p3, nb, *params), bsz, seq, nb, ATTN_BLOCK, "swa_attn", riders)


def _s5_prep_kernel(lre_ref, lim_ref, ls_ref, bre_ref, bim_ref, are_ref, aim_ref, obre_ref, obim_ref):
    lre = lre_ref[...]
    lim = lim_ref[...]
    dt = jnp.exp(ls_ref[...])
    mag = jnp.exp(lre * dt)
    are = mag * jnp.cos(lim * dt)
    aim = mag * jnp.sin(lim * dt)
    are_ref[...] = are
    aim_ref[...] = aim
    inv = 1.0 / (lre * lre + lim * lim)
    cre = ((are - 1.0) * lre + aim * lim) * inv
    cim = (aim * lre - (are - 1.0) * lim) * inv
    bre = bre_ref[...]
    bim = bim_ref[...]
    obre_ref[...] = cre[:, None, :] * bre - cim[:, None, :] * bim
    obim_ref[...] = cre[:, None, :] * bim + cim[:, None, :] * bre


def _s5_kernel(u_ref, are_ref, aim_ref, b_ref, c_ref, d_ref, gw_ref, gb_ref, o_ref, x_ref, s_ref):
    bsz, tc, ch = u_ref.shape
    hw = S5_WIDTH // 2
    hc = ch // 2

    @pl.when(pl.program_id(0) == 0)
    def _():
        s_ref[...] = jnp.zeros_like(s_ref)

    u = jnp.swapaxes(u_ref[...].astype(F32), 0, 1).reshape(tc * bsz, ch)

    def scan(hf):
        xh = x_ref.at[hf]
        for s0 in range(0, hw, S5_STRIP):
            re = slice(s0, s0 + S5_STRIP)
            im = slice(hw + s0, hw + s0 + S5_STRIP)
            lam = slice(hw * hf + s0, hw * hf + s0 + S5_STRIP)
            ar = jnp.broadcast_to(are_ref[:, lam], (bsz, S5_STRIP))
            ai = jnp.broadcast_to(aim_ref[:, lam], (bsz, S5_STRIP))
            sre = slice(2 * hw * hf + s0, 2 * hw * hf + s0 + S5_STRIP)
            sim = slice(2 * hw * hf + hw + s0, 2 * hw * hf + hw + s0 + S5_STRIP)
            sr, si = s_ref[:, sre], s_ref[:, sim]
            for t in range(tc):
                rows = slice(t * bsz, (t + 1) * bsz)
                sr, si = ar * sr - ai * si + xh[rows, re], ar * si + ai * sr + xh[rows, im]
                xh[rows, re] = sr
                xh[rows, im] = si
            s_ref[:, sre] = sr
            s_ref[:, sim] = si

    x_ref[0] = _mm(u[:, :hc], b_ref[0])
    x_ref[1] = _mm(u[:, hc:], b_ref[1])
    scan(0)
    y0 = _mm(x_ref[0], c_ref[0])
    scan(1)
    y = jnp.concatenate([y0, _mm(x_ref[1], c_ref[1])], axis=1)
    y = y + d_ref[...] * u
    z = 0.5 * y * (1.0 + lax.erf(y * (2.0 ** -0.5)))
    out = z * jax.nn.sigmoid(_mm(z, gw_ref[...]) + gb_ref[...])
    o_ref[...] = jnp.swapaxes(out.reshape(tc, bsz, ch), 0, 1).astype(o_ref.dtype)


def _s5(p3, lam_re, lam_im, log_step, b_re, b_im, c_re, c_im, d_skip, glu_w, glu_b):
    bsz, seq, _ = p3.shape
    g, st, ch = S5_GROUPS, S5_STATE, S5_GROUP
    gwd = GROUP_WIDTH
    vm = pl.BlockSpec(memory_space=pltpu.VMEM)
    a_re, a_im, bb_re, bb_im = pl.pallas_call(
        _s5_prep_kernel,
        in_specs=[vm] * 5,
        out_specs=[vm] * 4,
        out_shape=[jax.ShapeDtypeStruct((g, st), F32)] * 2 + [jax.ShapeDtypeStruct((g, ch, st), F32)] * 2,
        name="s5_prep",
    )(lam_re, lam_im, log_step.reshape(g, 1), jnp.swapaxes(b_re, 1, 2), jnp.swapaxes(b_im, 1, 2))

    gh = g // 2
    eye = jnp.eye(gh, dtype=F32)
    bd_in = lambda t: (t[:, :, None, :] * eye[:, None, :, None]).reshape(gh * ch, gh * st)
    bd_out = lambda t: (jnp.swapaxes(t, 1, 2)[:, :, None, :] * eye[:, None, :, None]).reshape(gh * st, gh * ch)
    halves = lambda t: (t[:gh], t[gh:])
    b_mat = jnp.stack([jnp.concatenate([bd_in(r), bd_in(i)], axis=1)
                       for r, i in zip(halves(bb_re), halves(bb_im))]).astype(BF16)
    c_mat = jnp.stack([jnp.concatenate([bd_out(r), -bd_out(i)], axis=0)
                       for r, i in zip(halves(c_re.astype(F32)), halves(c_im.astype(F32)))]).astype(BF16)

    tc = min(64, seq)
    full = lambda t: pl.BlockSpec(t.shape, lambda i: (0,) * t.ndim)
    params = [a_re.reshape(1, S5_WIDTH), a_im.reshape(1, S5_WIDTH), b_mat, c_mat,
              d_skip.reshape(1, gwd), glu_w.astype(BF16), glu_b.reshape(1, gwd)]
    return pl.pallas_call(
        _s5_kernel,
        grid=(seq // tc,),
        in_specs=[pl.BlockSpec((bsz, tc, gwd), lambda i: (0, i, S5_OFF_BLOCKS))] + [full(t) for t in params],
        out_specs=pl.BlockSpec((bsz, tc, gwd), lambda i: (0, i, 0)),
        out_shape=jax.ShapeDtypeStruct((bsz, seq, gwd), BF16),
        scratch_shapes=[pltpu.VMEM((2, tc * bsz, S5_WIDTH), F32), pltpu.VMEM((bsz, 2 * S5_WIDTH), F32)],
        compiler_params=pltpu.CompilerParams(
            dimension_semantics=("arbitrary",), vmem_limit_bytes=VMEM_LIMIT),
        name="s5_scan",
    )(p3, *params)


def _hgrn_kernel(qf_ref, ig_ref, lbraw_ref, gn_ref, tri_ref, bd_ref, o_ref, s_ref, osc_ref, *, layer):
    nb, tb, _ = qf_ref.shape
    cz = HGRN_CHUNK
    gw = GROUP_WIDTH
    n = HEAD_DIM
    nh = gw // n

    qf = qf_ref[...].astype(F32).reshape(nb * tb, 2 * gw)
    ig = ig_ref[...].astype(F32).reshape(nb * tb, 2 * gw)
    q, f = qf[:, :gw], qf[:, gw:]
    v, g = ig[:, :gw], ig[:, gw:]

    lbr = lbraw_ref[...]
    e = jnp.exp(lbr - jnp.max(lbr, axis=0, keepdims=True))
    sm = e / jnp.sum(e, axis=0, keepdims=True)
    lb = jnp.zeros((1, gw), F32)
    for i in range(1, layer + 1):
        lb = lb + sm[i:i + 1, :]

    q = _silu(q) * (n ** -0.5)
    f_gate = lb + (1.0 - lb) * jax.nn.sigmoid(f)
    log_f = jnp.log(f_gate)
    k = 1.0 - f_gate

    parts = _split2(log_f)

    def per_seq(c_ref):
        cm = c_ref[...]
        seq_rows = lambda t, s: t[s * tb:(s + 1) * tb, :]
        return jnp.concatenate(
            [_dot(cm, seq_rows(parts[0], s), _NN) + _dot(cm, seq_rows(parts[1], s), _NN)
             for s in range(nb)], axis=0)

    bcum = per_seq(tri_ref)
    b3 = bcum.reshape(nb * tb // cz, cz, gw)
    chunk_row = lambda r: jnp.broadcast_to(b3[:, r:r + 1, :], b3.shape).reshape(nb * tb, gw)
    bmid = chunk_row(cz // 2 - 1)
    blast = chunk_row(cz - 1)
    qe = q * jnp.exp(bcum - bmid)
    ke = k * jnp.exp(bmid - bcum)
    kl = k * jnp.exp(blast - bcum)
    qb = q * jnp.exp(bcum)
    dec = jnp.exp(blast)

    ri = lax.broadcasted_iota(jnp.int32, (tb, tb), 0)
    ci = lax.broadcasted_iota(jnp.int32, (tb, tb), 1)
    mask = jnp.logical_and(ri // cz == ci // cz, ri >= ci)

    heads = lambda t: jnp.stack(
        [t[s * tb:(s + 1) * tb, h * n:(h + 1) * n] for s in range(nb) for h in range(nh)], axis=0)
    v_h, qb_h, kl_h, dec_h = heads(v), heads(qb), heads(kl), heads(dec)
    att = jnp.where(mask, _bmm(heads(qe), heads(ke), _BNT), 0.0)
    o_intra = _bmm(att, v_h)
    nchunk = tb // cz
    ti = lax.broadcasted_iota(jnp.int32, (tb, nchunk * n), 0)
    li = lax.broadcasted_iota(jnp.int32, (tb, nchunk * n), 1)
    kl_spread = jnp.where(ti // cz == li // n, jnp.concatenate([kl_h] * nchunk, axis=2), 0.0)
    kv_all = _bmm(v_h, kl_spread, _BTN)
    st = s_ref[...]
    o_inter = []
    for j in range(nchunk):
        rows = slice(j * cz, (j + 1) * cz)
        o_inter.append(_bmm(qb_h[:, rows, :], st, _BNT))
        st = st * dec_h[:, j * cz:j * cz + 1, :] + kv_all[:, :, j * n:(j + 1) * n]
    s_ref[...] = st
    o_heads = o_intra + jnp.concatenate(o_inter, axis=1)
    for s in range(nb):
        for h in range(nh):
            osc_ref[s * tb:(s + 1) * tb, h * n:(h + 1) * n] = o_heads[s * nh + h]

    o = osc_ref[...]
    ms = _head_sum(o * o, bd_ref[...]) * (1.0 / n)
    o_ref[...] = (o * lax.rsqrt(ms + NORM_EPS) * gn_ref[...] * _silu(g)).reshape(nb, tb, gw).astype(o_ref.dtype)


def _hgrn_part(p3, nb, lower_bounds, g_norm, layer):
    tb = HGRN_TILE
    cz = HGRN_CHUNK
    gw = GROUP_WIDTH
    idx = np.arange(tb)
    same = (idx[:, None] // cz) == (idx[None, :] // cz)
    tri = same & (idx[:, None] >= idx[None, :])
    consts = [jnp.asarray(tri.astype(np.float32), BF16)]
    ones_bd = jnp.asarray(np.kron(np.eye(4, dtype=np.float32), np.ones((HEAD_DIM, HEAD_DIM), np.float32)), BF16)
    gn = jnp.tile(g_norm.astype(F32), gw // HEAD_DIM).reshape(1, gw)
    params = [lower_bounds.astype(F32), gn] + consts + [ones_bd]
    col = lambda o: pl.BlockSpec((nb, tb, 2 * gw), lambda i, j: (i, j, HGRN_OFF_BLOCKS + o))
    in_specs = [col(0), col(1)] + [_full_spec(t) for t in params]
    scratch = [pltpu.VMEM((nb * gw // HEAD_DIM, HEAD_DIM, HEAD_DIM), F32), pltpu.VMEM((nb * tb, gw), F32)]
    return functools.partial(_hgrn_kernel, layer=layer), in_specs, [p3, p3] + params, scratch


def _hgrn(p3, *params, riders=()):
    bsz, seq, _ = p3.shape
    nb = HGRN_SEQS if bsz % HGRN_SEQS == 0 else 1
    return _mixer_call(_hgrn_part(p3, nb, *params), bsz, seq, nb, HGRN_TILE, "hgrn2", riders)


def kernel(x, positions, ffn1_norm, ffn1_w_gate, ffn1_w_up, ffn1_w_down, mix_norm, w_in, rwkv_mu, rwkv_w0, rwkv_w_up, rwkv_a0, rwkv_a_up, rwkv_g_up, rwkv_k_k, rwkv_k_a, rwkv_r_k, rwkv_ln_w, rwkv_ln_b, attn_q_norm, attn_k_norm, attn_sinks, s5_lambda_re, s5_lambda_im, s5_log_step, s5_b_re, s5_b_im, s5_c_re, s5_c_im, s5_d, s5_glu_w, s5_glu_b, hgrn_lower_bounds, hgrn_g_norm, w_out, ffn2_norm, ffn2_w_gate, ffn2_w_up, ffn2_w_down):
    bsz, seq, d = x.shape
    depth = w_in.shape[0]
    n = bsz * seq
    xf = x.reshape(n, d)
    rope_cos, rope_sin = _rope_tables(positions)
    ffn_w32 = []
    for l in range(depth):
        ffn_w32.append([(w, l) for w in (ffn1_w_gate, ffn1_w_up, ffn1_w_down)])
        ffn_w32.append([(w, l) for w in (ffn2_w_gate, ffn2_w_up, ffn2_w_down)])
    ffn_w32.append([])
    w16 = [w[l].astype(BF16) for w, l in ffn_w32[0]]
    w_in16 = w_in[0].astype(BF16)
    for l in range(depth):
        xf, w16_next = _ffn(xf, ffn1_norm[l], *w16, riders=ffn_w32[2 * l + 1])
        p3 = _proj(xf, mix_norm[l], w_in16).reshape(bsz, seq, D_IN)
        y_a, (w_out16,) = _rwkv(p3, rwkv_mu[l], rwkv_w0[l], rwkv_w_up[l], rwkv_a0[l], rwkv_a_up[l], rwkv_g_up[l],
                                rwkv_k_k[l], rwkv_k_a[l], rwkv_r_k[l].reshape(-1), rwkv_ln_w[l], rwkv_ln_b[l],
                                riders=[(w_out, l)])
        y_b, _ = _attn(p3, rope_cos, rope_sin, attn_q_norm[l], attn_k_norm[l], attn_sinks[l])
        y_c = _s5(p3, s5_lambda_re[l], s5_lambda_im[l], s5_log_step[l], s5_b_re[l], s5_b_im[l],
                  s5_c_re[l], s5_c_im[l], s5_d[l], s5_glu_w[l], s5_glu_b[l])
        y_d, next_in = _hgrn(p3, hgrn_lower_bounds, hgrn_g_norm[l], l,
                             riders=[(w_in, l + 1)] if l + 1 < depth else [])
        if next_in:
            w_in16 = next_in[0]
        ys = [t.reshape(n, GROUP_WIDTH) for t in (y_a, y_b, y_c, y_d)]
        xf = _outproj(xf, ys, w_out16)
        xf, w16 = _ffn(xf, ffn2_norm[l], *w16_next, riders=ffn_w32[2 * l + 2])
    return xf.reshape(bsz, seq, d)
```

```python
import functools

import jax
import jax.numpy as jnp
import numpy as np
from jax import lax
from jax.experimental import pallas as pl
from jax.experimental.pallas import tpu as pltpu

F32 = jnp.float32
BF16 = jnp.bfloat16

HEAD_DIM = 64
GROUP_WIDTH = 512
NORM_EPS = 1e-6
FFN_RES_WEIGHT = 0.5
FFN_NORM_SPLIT = 4

RWKV_W_RANK = 64
RWKV_A_RANK = 64
RWKV_G_RANK = 128
RWKV_LN_EPS = 64e-5
RWKV_IN = 3 * GROUP_WIDTH + RWKV_W_RANK + RWKV_A_RANK + RWKV_G_RANK
RWKV_CHUNK = 64
RWKV_BLOCK = 128
MIXER_SEQS = 4

ATTN_HEADS = 8
ATTN_KV_HEADS = 2
ATTN_BLOCK = 128
ATTN_SEQS = 8
ROPE_THETA = 500000.0
ROPE_DIM = HEAD_DIM // 4
ATTN_COL_BLOCK = 256
ATTN_OFF_BLOCKS = RWKV_IN // ATTN_COL_BLOCK

S5_GROUP = 16
S5_GROUPS = GROUP_WIDTH // S5_GROUP
S5_STATE = 64
S5_WIDTH = S5_GROUPS * S5_STATE
S5_OFF_BLOCKS = (RWKV_IN + 768) // GROUP_WIDTH
S5_STRIP = 512

HGRN_CHUNK = 16
HGRN_TILE = 128
HGRN_SEQS = 8
HGRN_OFF_BLOCKS = 3

D_IN = 5120
VMEM_LIMIT = 56 * 1024 * 1024


def _dot(a, b, dims):
    return lax.dot_general(a, b, (dims, ((), ())), preferred_element_type=F32)


_NN = ((1,), (0,))


def _mm(a, b):
    return _dot(a.astype(BF16), b.astype(BF16), _NN)


_BNN = (((2,), (1,)), ((0,), (0,)))
_BNT = (((2,), (2,)), ((0,), (0,)))
_BTN = (((1,), (1,)), ((0,), (0,)))


def _bmm(a, b, dims=_BNN):
    return lax.dot_general(a.astype(BF16), b.astype(BF16), dims, preferred_element_type=F32)


def _split2(x):
    hi = x.astype(BF16)
    lo = (x - hi.astype(F32)).astype(BF16)
    return hi, lo


def _const_lhs_mm(c, x):
    hi, lo = _split2(x)
    return _dot(c, hi, _NN) + _dot(c, lo, _NN)


def _const_rhs_mm(x, c):
    hi, lo = _split2(x)
    return _dot(hi, c, _NN) + _dot(lo, c, _NN)


def _head_sum(x, ones_bd):
    w = ones_bd.shape[0]
    parts = [_const_rhs_mm(x[:, i:i + w], ones_bd) for i in range(0, x.shape[1], w)]
    return parts[0] if len(parts) == 1 else jnp.concatenate(parts, axis=1)


def _silu(x):
    return x * jax.nn.sigmoid(x)


def _rms_rows(x, gain):
    ms = jnp.mean(x * x, axis=-1, keepdims=True)
    return x * lax.rsqrt(ms + NORM_EPS) * gain


def _ffn_kernel(x_ref, g_ref, wg_ref, wu_ref, wd_ref, *rest):
    n_riders = (len(rest) - 2) // 2
    o_ref, h_ref = rest[n_riders], rest[-1]

    def swiglu_step(h, base):
        gate = jnp.dot(h, wg_ref[...], preferred_element_type=F32)
        up = jnp.dot(h, wu_ref[...], preferred_element_type=F32)
        act = (_silu(gate) * up).astype(BF16)
        return base + FFN_RES_WEIGHT * jnp.dot(act, wd_ref[...], preferred_element_type=F32)

    @pl.when(pl.program_id(1) == 0)
    def _():
        sub = x_ref.shape[0] // FFN_NORM_SPLIT
        for s in range(FFN_NORM_SPLIT):
            rows = slice(s * sub, (s + 1) * sub)
            x = x_ref[rows, :]
            h = _rms_rows(x, g_ref[...]).astype(BF16)
            h_ref[rows, :] = h
            o_ref[rows, :] = swiglu_step(h, x)

    @pl.when(pl.program_id(1) > 0)
    def _():
        o_ref[...] = swiglu_step(h_ref[...], o_ref[...])

    for src, dst in zip(rest[:n_riders], rest[n_riders + 1:-1]):
        dst[...] = src[...].astype(BF16)


def _rider_specs(shape, layer, gi, gj):
    _, r, c = shape
    steps = gi * gj
    if r % steps == 0 and (r // steps) % 16 == 0:
        blk, pos = (r // steps, c), lambda i, j: (i * gj + j, 0)
    elif r % gi == 0 and (r // gi) % 16 == 0 and c % gj == 0 and (c // gj) % 128 == 0:
        blk, pos = (r // gi, c // gj), lambda i, j: (i, j)
    else:
        return None
    return pl.BlockSpec((None,) + blk, lambda i, j: (layer,) + pos(i, j)), pl.BlockSpec(blk, pos)


def _ffn(x2d, gain, wg, wu, wd, riders=()):
    n, d = x2d.shape
    f = wg.shape[1]
    tm = min(1024, n)
    tf = 512 if f % 512 == 0 else f
    gi, gj = n // tm, f // tf
    specs = [_rider_specs(w.shape, layer, gi, gj) for w, layer in riders]
    ride = [k for k, s in enumerate(specs) if s is not None]
    outs = pl.pallas_call(
        _ffn_kernel,
        grid=(gi, gj),
        in_specs=[
            pl.BlockSpec((tm, d), lambda i, j: (i, 0)),
            pl.BlockSpec((1, d), lambda i, j: (0, 0)),
            pl.BlockSpec((d, tf), lambda i, j: (0, j)),
            pl.BlockSpec((d, tf), lambda i, j: (0, j)),
            pl.BlockSpec((tf, d), lambda i, j: (j, 0)),
        ] + [specs[k][0] for k in ride],
        out_specs=[pl.BlockSpec((tm, d), lambda i, j: (i, 0))] + [specs[k][1] for k in ride],
        out_shape=[jax.ShapeDtypeStruct((n, d), F32)]
        + [jax.ShapeDtypeStruct(riders[k][0].shape[1:], BF16) for k in ride],
        scratch_shapes=[pltpu.VMEM((tm, d), BF16)],
        compiler_params=pltpu.CompilerParams(
            dimension_semantics=("parallel", "arbitrary"), vmem_limit_bytes=VMEM_LIMIT),
        name="ffn",
    )(x2d, gain.reshape(1, d), wg, wu, wd, *[riders[k][0] for k in ride])
    cast = [outs[1 + ride.index(k)] if k in ride else w[layer].astype(BF16)
            for k, (w, layer) in enumerate(riders)]
    return outs[0], cast


def _proj_kernel(x_ref, g_ref, w_ref, o_ref, h_ref):
    project = lambda h: jnp.dot(h, w_ref[...], preferred_element_type=F32).astype(o_ref.dtype)

    @pl.when(pl.program_id(1) == 0)
    def _():
        sub = x_ref.shape[0] // FFN_NORM_SPLIT
        for s in range(FFN_NORM_SPLIT):
            rows = slice(s * sub, (s + 1) * sub)
            h = _rms_rows(x_ref[rows, :], g_ref[...]).astype(BF16)
            h_ref[rows, :] = h
            o_ref[rows, :] = project(h)

    @pl.when(pl.program_id(1) > 0)
    def _():
        o_ref[...] = project(h_ref[...])


def _proj(x2d, gain, w):
    n, d = x2d.shape
    dout = w.shape[1]
    tm = min(1024, n)
    tn = dout // 2
    return pl.pallas_call(
        _proj_kernel,
        grid=(n // tm, dout // tn),
        in_specs=[
            pl.BlockSpec((tm, d), lambda i, j: (i, 0)),
            pl.BlockSpec((1, d), lambda i, j: (0, 0)),
            pl.BlockSpec((d, tn), lambda i, j: (0, j)),
        ],
        out_specs=pl.BlockSpec((tm, tn), lambda i, j: (i, j)),
        out_shape=jax.ShapeDtypeStruct((n, dout), BF16),
        scratch_shapes=[pltpu.VMEM((tm, d), BF16)],
        compiler_params=pltpu.CompilerParams(
            dimension_semantics=("parallel", "arbitrary"), vmem_limit_bytes=VMEM_LIMIT),
        name="in_proj",
    )(x2d, gain.reshape(1, d), w)


def _outproj_kernel(x_ref, ya_ref, yb_ref, yc_ref, yd_ref, w_ref, o_ref):
    gw = GROUP_WIDTH
    acc = x_ref[...]
    for m, y_ref in enumerate((ya_ref, yb_ref, yc_ref, yd_ref)):
        acc = acc + jnp.dot(y_ref[...].astype(BF16), w_ref[m * gw:(m + 1) * gw, :],
                            preferred_element_type=F32)
    o_ref[...] = acc


def _outproj(x2d, ys, w):
    n, d = x2d.shape
    tm = min(512, n)
    yspec = pl.BlockSpec((tm, GROUP_WIDTH), lambda i: (i, 0))
    return pl.pallas_call(
        _outproj_kernel,
        grid=(n // tm,),
        in_specs=[pl.BlockSpec((tm, d), lambda i: (i, 0)), yspec, yspec, yspec, yspec,
                  pl.BlockSpec(w.shape, lambda i: (0, 0))],
        out_specs=pl.BlockSpec((tm, d), lambda i: (i, 0)),
        out_shape=jax.ShapeDtypeStruct((n, d), F32),
        compiler_params=pltpu.CompilerParams(
            dimension_semantics=("parallel",), vmem_limit_bytes=VMEM_LIMIT),
        name="out_proj",
    )(x2d, *ys, w)


def _advance(fillers):
    for gen in fillers:
        if next(gen, _DONE) is not _DONE:
            return


_DONE = object()


def _rwkv_kernel(p_ref, mu_ref, w0_ref, wup_ref, a0_ref, aup_ref, gup_ref, kk_ref, ka_ref, rk_ref,
                 lnw_ref, lnb_ref, tri_ref, bd_ref, o_ref, s_ref, prev_ref, osc_ref):
    c = RWKV_CHUNK
    nck = p_ref.shape[1] // c
    prm = dict(mu=mu_ref, w0=w0_ref, wup=wup_ref, a0=a0_ref, aup=aup_ref, gup=gup_ref, kk=kk_ref, ka=ka_ref,
               rk=rk_ref, lnw=lnw_ref, lnb=lnb_ref, tri=tri_ref, bd=bd_ref)
    chunk = lambda ref, ck: ref.at[:, pl.ds(ck * c, c), :]
    preps = [dict() for _ in range(nck)]
    for _ in _rwkv_prologue(chunk(p_ref, 0), prm, prev_ref, preps[0]):
        pass
    epilogue = iter(())
    for ck in range(nck):
        fillers = [epilogue]
        if ck + 1 < nck:
            fillers.append(_rwkv_prologue(chunk(p_ref, ck + 1), prm, prev_ref, preps[ck + 1]))
        y = _rwkv_solve(preps[ck], s_ref, functools.partial(_advance, fillers))
        for gen in fillers:
            for _ in gen:
                pass
        epilogue = _rwkv_epilogue(preps[ck], y, prm, chunk(o_ref, ck), osc_ref)
    for _ in epilogue:
        pass


def _rwkv_prologue(p_ref, prm, prev_ref, out):
    nb, c, _ = p_ref.shape
    gw = GROUP_WIDTH
    n = HEAD_DIM
    nh = gw // n

    p = p_ref[...].astype(F32).reshape(nb * c, RWKV_IN)
    row = lax.broadcasted_iota(jnp.int32, p.shape, 0)
    shifted = pltpu.roll(p, 1, axis=0)
    for s in range(nb):
        shifted = jnp.where(row == s * c, prev_ref[s], shifted)
        prev_ref[s] = p[(s + 1) * c - 1:(s + 1) * c, :]
    p = p + (shifted - p) * prm["mu"][...]
    yield

    r = p[:, 0:gw]
    k = p[:, gw:2 * gw]
    v = p[:, 2 * gw:3 * gw]
    o1 = 3 * gw
    w_lo = p[:, o1:o1 + RWKV_W_RANK]
    a_lo = p[:, o1 + RWKV_W_RANK:o1 + RWKV_W_RANK + RWKV_A_RANK]
    g_lo = p[:, o1 + RWKV_W_RANK + RWKV_A_RANK:]

    z = -(prm["w0"][...] + _mm(jnp.tanh(w_lo), prm["wup"][...]))
    softplus = jnp.maximum(z, 0.0) + jnp.log1p(jnp.exp(-jnp.abs(z)))
    lw = -jnp.exp(-softplus - 0.5)
    yield
    a = jax.nn.sigmoid(prm["a0"][...] + _mm(a_lo, prm["aup"][...]))
    g = _mm(jax.nn.sigmoid(g_lo), prm["gup"][...])
    yield

    kk = k * prm["kk"][...]
    kk = kk * lax.rsqrt(jnp.maximum(_head_sum(kk * kk, prm["bd"][...]), 1e-24))
    yield
    k = k * (1.0 + (a - 1.0) * prm["ka"][...])
    b = kk * a
    cum = _const_lhs_mm(prm["tri"][...], lw)
    yield
    cum_last = jnp.concatenate(
        [jnp.broadcast_to(cum[(s + 1) * c - 1:(s + 1) * c, :], (c, gw)) for s in range(nb)], axis=0)
    heads = lambda t, rows=c: jnp.stack(
        [t[s * c:s * c + rows, h * n:(h + 1) * n] for s in range(nb) for h in range(nh)], axis=0)
    mxu_heads = lambda t: heads(t.astype(BF16))
    out.update(r=r, k=k, v=v, g=g)
    e_neg = jnp.exp(-cum)
    out["bd_h"] = mxu_heads(b * e_neg)
    yield
    out["kd_h"] = mxu_heads(k * e_neg)
    yield
    out["kq_h"] = mxu_heads(kk * jnp.exp(cum - lw))
    yield
    out["rq_h"] = mxu_heads(r * jnp.exp(cum))
    yield
    out["v_h"] = mxu_heads(v)
    yield
    e_end = jnp.exp(cum_last - cum)
    out["be_h"] = mxu_heads(b * e_end)
    yield
    out["ke_h"] = mxu_heads(k * e_end)
    out["ge_h"] = heads(jnp.exp(cum_last), 1)
    yield


def _rwkv_solve(pp, s_ref, tick):
    kq_h, rq_h, bd_h, kd_h, v_h = pp["kq_h"], pp["rq_h"], pp["bd_h"], pp["kd_h"], pp["v_h"]
    c = kq_h.shape[1]
    ri = lax.broadcasted_iota(jnp.int32, (c, c), 0)
    ci = lax.broadcasted_iota(jnp.int32, (c, c), 1)
    strict = ri > ci
    incl = ri >= ci
    eye = (ri == ci).astype(F32)

    s0 = s_ref[...]
    gmat = _bmm(jnp.concatenate([kq_h, rq_h], axis=1), jnp.concatenate([bd_h, kd_h], axis=1), _BNT)
    tick()
    a_bb = jnp.where(strict, gmat[:, :c, :c], 0.0)
    a_bk = jnp.where(strict, gmat[:, :c, c:], 0.0)
    a_rb = jnp.where(incl, gmat[:, c:, :c], 0.0)
    a_rk = jnp.where(incl, gmat[:, c:, c:], 0.0)
    pw = -a_bb
    t_inv = eye + pw
    for _ in range(int(np.log2(c)) - 1):
        pw = _bmm(pw, pw)
        tick()
        t_inv = t_inv + _bmm(t_inv, pw)
        tick()
    x = _bmm(kq_h, s0, _BNT)
    tick()
    x = x + _bmm(a_bk, v_h)
    tick()
    u = -_bmm(t_inv, x)
    tick()
    y = _bmm(rq_h, s0, _BNT)
    tick()
    y = y + _bmm(a_rb, u)
    tick()
    y = y + _bmm(a_rk, v_h)
    tick()
    s_ref[...] = s0 * pp["ge_h"] + _bmm(jnp.concatenate([u.astype(BF16), v_h], axis=1),
                                        jnp.concatenate([pp["be_h"], pp["ke_h"]], axis=1), _BTN)
    tick()
    return y


def _rwkv_epilogue(pp, y, prm, o_ref, osc_ref):
    nb, c, gw = o_ref.shape
    n = HEAD_DIM
    nh = gw // n
    ones_bd = prm["bd"][...]
    for s in range(nb):
        for h in range(nh):
            osc_ref[s * c:(s + 1) * c, h * n:(h + 1) * n] = y[s * nh + h]
    yield
    o = osc_ref[...]
    inv_n = 1.0 / n
    mean = _head_sum(o, ones_bd) * inv_n
    yield
    dlt = o - mean
    var = _head_sum(dlt * dlt, ones_bd) * inv_n
    yield
    o = dlt * lax.rsqrt(var + RWKV_LN_EPS) * prm["lnw"][...] + prm["lnb"][...]
    yield
    bonus = _head_sum(pp["r"] * pp["k"] * prm["rk"][...], ones_bd) * pp["v"]
    yield
    o_ref[...] = ((o + bonus) * pp["g"]).reshape(nb, c, gw).astype(o_ref.dtype)
    yield


def _full_spec(t):
    return pl.BlockSpec(t.shape, lambda i, j: (0,) * t.ndim)


def _mixer_call(part, bsz, seq, nb, tblk, name, riders=()):
    body, in_specs, operands, scratch = part
    gi, gj = bsz // nb, seq // tblk
    specs = [_rider_specs(w.shape, layer, gi, gj) for w, layer in riders]
    ride = [k for k, s in enumerate(specs) if s is not None]
    n_in, n_ride = len(operands), len(ride)

    def kern(*refs):
        ins, rider_in = refs[:n_in], refs[n_in:n_in + n_ride]
        out, rider_out = refs[n_in + n_ride], refs[n_in + n_ride + 1:n_in + 2 * n_ride + 1]
        scr = refs[n_in + 2 * n_ride + 1:]

        @pl.when(pl.program_id(1) == 0)
        def _():
            for ref in scr:
                ref[...] = jnp.zeros_like(ref)

        body(*ins, out, *scr)
        for src, dst in zip(rider_in, rider_out):
            dst[...] = src[...].astype(BF16)

    out_spec = pl.BlockSpec((nb, tblk, GROUP_WIDTH), lambda i, j: (i, j, 0))
    outs = pl.pallas_call(
        kern,
        grid=(gi, gj),
        in_specs=list(in_specs) + [specs[k][0] for k in ride],
        out_specs=[out_spec] + [specs[k][1] for k in ride],
        out_shape=[jax.ShapeDtypeStruct((bsz, seq, GROUP_WIDTH), BF16)]
        + [jax.ShapeDtypeStruct(riders[k][0].shape[1:], BF16) for k in ride],
        scratch_shapes=scratch,
        compiler_params=pltpu.CompilerParams(
            dimension_semantics=("parallel", "arbitrary"), vmem_limit_bytes=VMEM_LIMIT),
        name=name,
    )(*operands, *[riders[k][0] for k in ride])
    cast = [outs[1 + ride.index(k)] if k in ride else w[layer].astype(BF16)
            for k, (w, layer) in enumerate(riders)]
    return outs[0], cast


def _rwkv_part(p3, nb, tblk, mu, w0, w_up, a0, a_up, g_up, k_k, k_a, r_k, ln_w, ln_b):
    c = RWKV_CHUNK
    gw = GROUP_WIDTH
    tri = jnp.asarray(np.kron(np.eye(nb, dtype=np.float32), np.tril(np.ones((c, c), np.float32))), BF16)
    ones_bd = jnp.asarray(np.kron(np.eye(4, dtype=np.float32), np.ones((HEAD_DIM, HEAD_DIM), np.float32)), BF16)
    row = lambda t: t.reshape(1, -1)
    params = [row(mu), row(w0), w_up, row(a0), a_up, g_up, row(k_k), row(k_a), row(r_k), row(ln_w),
              row(ln_b), tri, ones_bd]
    in_specs = [pl.BlockSpec((nb, tblk, RWKV_IN), lambda i, j: (i, j, 0))] + [_full_spec(t) for t in params]
    scratch = [pltpu.VMEM((nb * gw // HEAD_DIM, HEAD_DIM, HEAD_DIM), F32),
               pltpu.VMEM((nb, 1, RWKV_IN), F32),
               pltpu.VMEM((nb * c, gw), F32)]
    return _rwkv_kernel, in_specs, [p3] + params, scratch


def _rwkv(p3, *params, riders=()):
    bsz, seq, _ = p3.shape
    nb = MIXER_SEQS if bsz % MIXER_SEQS == 0 else 1
    tblk = min(RWKV_BLOCK, seq)
    return _mixer_call(_rwkv_part(p3, nb, tblk, *params), bsz, seq, nb, tblk, "rwkv7", riders)


def _rope_table_kernel(pos_ref, invf_ref, cos_ref, sin_ref):
    ang = pos_ref[...].astype(F32) * invf_ref[...]
    cos_ref[...] = jnp.cos(ang)
    sin_ref[...] = jnp.sin(ang)


def _rope_tables(positions):
    bsz, seq = positions.shape
    d = HEAD_DIM
    w = 2 * d
    inv_freq = ROPE_THETA ** (-jnp.arange(0, ROPE_DIM, 2, dtype=F32) / ROPE_DIM)
    lane_d = np.arange(w) % d
    invf = jnp.where(lane_d < ROPE_DIM, inv_freq[lane_d % (ROPE_DIM // 2)], 0.0).reshape(1, w)
    tr = min(1024, bsz * seq)
    return pl.pallas_call(
        _rope_table_kernel,
        grid=(bsz * seq // tr,),
        in_specs=[pl.BlockSpec((tr, 1), lambda i: (i, 0)), pl.BlockSpec((1, w), lambda i: (0, 0))],
        out_specs=[pl.BlockSpec((tr, w), lambda i: (i, 0))] * 2,
        out_shape=[jax.ShapeDtypeStruct((bsz * seq, w), F32)] * 2,
        compiler_params=pltpu.CompilerParams(dimension_semantics=("parallel",)),
        name="rope_tables",
    )(positions.reshape(bsz * seq, 1), invf)


def _attn_kernel(cos_ref, sin_ref, q0_ref, q1_ref, kv_ref, qn_ref, kn_ref, sink_ref, bd_ref,
                 o_ref, kprev_ref, vprev_ref):
    nb, blk, _ = q0_ref.shape
    rows = nb * blk
    d = HEAD_DIM
    half = ROPE_DIM // 2
    group = ATTN_HEADS // ATTN_KV_HEADS
    kvw = ATTN_KV_HEADS * d
    first = pl.program_id(1) == 0
    ones_bd = bd_ref[...]
    cos = cos_ref[...].reshape(rows, 2 * d)
    sin = sin_ref[...].reshape(rows, 2 * d)

    def norm_rope(x, gain):
        w = x.shape[1]
        x = x * lax.rsqrt(_head_sum(x * x, ones_bd[:w, :w]) * (1.0 / d) + NORM_EPS) * gain
        ld = lax.broadcasted_iota(jnp.int32, (rows, w), 1) % d
        rot = jnp.where(ld < half, -pltpu.roll(x, w - half, axis=1),
                        jnp.where(ld < ROPE_DIM, pltpu.roll(x, half, axis=1), 0.0))
        tile = lambda t: t if w == t.shape[1] else jnp.concatenate([t] * (w // t.shape[1]), axis=1)
        return x * tile(cos) + rot * tile(sin)

    qs = [norm_rope(q_ref[...].astype(F32).reshape(rows, ATTN_COL_BLOCK), qn_ref[...]) * (d ** -0.5)
          for q_ref in (q0_ref, q1_ref)]
    kv = kv_ref[...].astype(F32).reshape(rows, ATTN_COL_BLOCK)
    k_cur = norm_rope(kv[:, :kvw], kn_ref[:, :kvw])
    v_cur = kv[:, kvw:]
    k_prev = kprev_ref[...]
    v_prev = vprev_ref[...]

    heads_per_ref = ATTN_COL_BLOCK // d
    q_head = lambda s, qh: qs[qh // heads_per_ref][s * blk:(s + 1) * blk,
                                                   (qh % heads_per_ref) * d:(qh % heads_per_ref + 1) * d]
    q_g = jnp.stack([jnp.concatenate([q_head(s, kh * group + i) for i in range(group)], axis=0)
                     for s in range(nb) for kh in range(ATTN_KV_HEADS)], axis=0)
    kv_heads = lambda t: jnp.stack([t[s * blk:(s + 1) * blk, kh * d:(kh + 1) * d]
                                    for s in range(nb) for kh in range(ATTN_KV_HEADS)], axis=0)

    ri = lax.broadcasted_iota(jnp.int32, (group * blk, blk), 0) % blk
    ci = lax.broadcasted_iota(jnp.int32, (group * blk, blk), 1)
    neg_inf = -jnp.inf
    no_prev = jnp.where(first, neg_inf, 0.0)
    s_prev = jnp.where(ci > ri, _bmm(q_g, kv_heads(k_prev), _BNT), neg_inf) + no_prev
    s_cur = jnp.where(ci <= ri, _bmm(q_g, kv_heads(k_cur), _BNT), neg_inf)
    sink = sink_ref[...]
    m = jnp.maximum(jnp.maximum(jnp.max(s_prev, axis=2, keepdims=True),
                                jnp.max(s_cur, axis=2, keepdims=True)), sink)
    p_prev = jnp.exp(s_prev - m)
    p_cur = jnp.exp(s_cur - m)
    ones = jnp.ones((nb * ATTN_KV_HEADS, blk, d), BF16)
    den = _bmm(p_prev, ones) + _bmm(p_cur, ones) + jnp.exp(sink - m)
    o = ((_bmm(p_prev, kv_heads(v_prev)) + _bmm(p_cur, kv_heads(v_cur))) / den).astype(o_ref.dtype)
    for s in range(nb):
        for qh in range(ATTN_HEADS):
            o_ref[s, :, qh * d:(qh + 1) * d] = o[s * ATTN_KV_HEADS + qh // group,
                                                 (qh % group) * blk:(qh % group + 1) * blk, :]

    kprev_ref[...] = k_cur
    vprev_ref[...] = v_cur


def _attn_part(p3, nb, rope_cos, rope_sin, q_norm, k_norm, sinks):
    bsz, seq, _ = p3.shape
    blk = ATTN_BLOCK
    cw = ATTN_COL_BLOCK
    d = HEAD_DIM
    qn = jnp.tile(q_norm.astype(F32), cw // d).reshape(1, cw)
    kn = jnp.tile(k_norm.astype(F32), cw // d).reshape(1, cw)
    ones_bd = jnp.asarray(np.kron(np.eye(cw // d, dtype=np.float32), np.ones((d, d), np.float32)), BF16)
    group = ATTN_HEADS // ATTN_KV_HEADS
    sink_col = jnp.repeat(sinks.astype(F32).reshape(ATTN_KV_HEADS, group), blk, axis=1)[:, :, None]
    sink_col = jnp.tile(sink_col, (nb, 1, 1))
    tw = rope_cos.shape[-1]
    cos3 = rope_cos.reshape(bsz, seq, tw)
    sin3 = rope_sin.reshape(bsz, seq, tw)
    col = lambda o: pl.BlockSpec((nb, blk, cw), lambda i, j: (i, j, ATTN_OFF_BLOCKS + o))
    trig = pl.BlockSpec((nb, blk, tw), lambda i, j: (i, j, 0))
    consts = [qn, kn, sink_col, ones_bd]
    in_specs = [trig, trig, col(0), col(1), col(2)] + [_full_spec(t) for t in consts]
    scratch = [pltpu.VMEM((nb * blk, ATTN_KV_HEADS * d), F32), pltpu.VMEM((nb * blk, ATTN_KV_HEADS * d), F32)]
    return _attn_kernel, in_specs, [cos3, sin3, p3, p3, p3] + consts, scratch


def _attn(p3, *params, riders=()):
    bsz, seq, _ = p3.shape
    nb = ATTN_SEQS if bsz % ATTN_SEQS == 0 else 1
    return _mixer_call(_attn_part(p3, nb, *params), bsz, seq, nb, ATTN_BLOCK, "swa_attn", riders)


def _s5_prep_kernel(lre_ref, lim_ref, ls_ref, bre_ref, bim_ref, are_ref, aim_ref, obre_ref, obim_ref):
    lre = lre_ref[...]
    lim = lim_ref[...]
    dt = jnp.exp(ls_ref[...])
    mag = jnp.exp(lre * dt)
    are = mag * jnp.cos(lim * dt)
    aim = mag * jnp.sin(lim * dt)
    are_ref[...] = are
    aim_ref[...] = aim
    inv = 1.0 / (lre * lre + lim * lim)
    cre = ((are - 1.0) * lre + aim * lim) * inv
    cim = (aim * lre - (are - 1.0) * lim) * inv
    bre = bre_ref[...]
    bim = bim_ref[...]
    obre_ref[...] = cre[:, None, :] * bre - cim[:, None, :] * bim
    obim_ref[...] = cre[:, None, :] * bim + cim[:, None, :] * bre


def _s5_kernel(u_ref, are_ref, aim_ref, b_ref, c_ref, d_ref, gw_ref, gb_ref, o_ref, x_ref, s_ref):
    bsz, tc, ch = u_ref.shape
    hw = S5_WIDTH // 2
    hc = ch // 2

    @pl.when(pl.program_id(0) == 0)
    def _():
        s_ref[...] = jnp.zeros_like(s_ref)

    u = jnp.swapaxes(u_ref[...].astype(F32), 0, 1).reshape(tc * bsz, ch)

    def scan(hf):
        xh = x_ref.at[hf]
        for s0 in range(0, hw, S5_STRIP):
            re = slice(s0, s0 + S5_STRIP)
            im = slice(hw + s0, hw + s0 + S5_STRIP)
            lam = slice(hw * hf + s0, hw * hf + s0 + S5_STRIP)
            ar = jnp.broadcast_to(are_ref[:, lam], (bsz, S5_STRIP))
            ai = jnp.broadcast_to(aim_ref[:, lam], (bsz, S5_STRIP))
            sre = slice(2 * hw * hf + s0, 2 * hw * hf + s0 + S5_STRIP)
            sim = slice(2 * hw * hf + hw + s0, 2 * hw * hf + hw + s0 + S5_STRIP)
            sr, si = s_ref[:, sre], s_ref[:, sim]
            for t in range(tc):
                rows = slice(t * bsz, (t + 1) * bsz)
                sr, si = ar * sr - ai * si + xh[rows, re], ar * si + ai * sr + xh[rows, im]
                xh[rows, re] = sr
                xh[rows, im] = si
            s_ref[:, sre] = sr
            s_ref[:, sim] = si

    x_ref[0] = _mm(u[:, :hc], b_ref[0])
    x_ref[1] = _mm(u[:, hc:], b_ref[1])
    scan(0)
    y0 = _mm(x_ref[0], c_ref[0])
    scan(1)
    y = jnp.concatenate([y0, _mm(x_ref[1], c_ref[1])], axis=1)
    y = y + d_ref[...] * u
    z = 0.5 * y * (1.0 + lax.erf(y * (2.0 ** -0.5)))
    out = z * jax.nn.sigmoid(_mm(z, gw_ref[...]) + gb_ref[...])
    o_ref[...] = jnp.swapaxes(out.reshape(tc, bsz, ch), 0, 1).astype(o_ref.dtype)


def _s5(p3, lam_re, lam_im, log_step, b_re, b_im, c_re, c_im, d_skip, glu_w, glu_b):
    bsz, seq, _ = p3.shape
    g, st, ch = S5_GROUPS, S5_STATE, S5_GROUP
    gwd = GROUP_WIDTH
    vm = pl.BlockSpec(memory_space=pltpu.VMEM)
    a_re, a_im, bb_re, bb_im = pl.pallas_call(
        _s5_prep_kernel,
        in_specs=[vm] * 5,
        out_specs=[vm] * 4,
        out_shape=[jax.ShapeDtypeStruct((g, st), F32)] * 2 + [jax.ShapeDtypeStruct((g, ch, st), F32)] * 2,
        name="s5_prep",
    )(lam_re, lam_im, log_step.reshape(g, 1), jnp.swapaxes(b_re, 1, 2), jnp.swapaxes(b_im, 1, 2))

    gh = g // 2
    eye = jnp.eye(gh, dtype=F32)
    bd_in = lambda t: (t[:, :, None, :] * eye[:, None, :, None]).reshape(gh * ch, gh * st)
    bd_out = lambda t: (jnp.swapaxes(t, 1, 2)[:, :, None, :] * eye[:, None, :, None]).reshape(gh * st, gh * ch)
    halves = lambda t: (t[:gh], t[gh:])
    b_mat = jnp.stack([jnp.concatenate([bd_in(r), bd_in(i)], axis=1)
                       for r, i in zip(halves(bb_re), halves(bb_im))]).astype(BF16)
    c_mat = jnp.stack([jnp.concatenate([bd_out(r), -bd_out(i)], axis=0)
                       for r, i in zip(halves(c_re.astype(F32)), halves(c_im.astype(F32)))]).astype(BF16)

    tc = min(64, seq)
    full = lambda t: pl.BlockSpec(t.shape, lambda i: (0,) * t.ndim)
    params = [a_re.reshape(1, S5_WIDTH), a_im.reshape(1, S5_WIDTH), b_mat, c_mat,
              d_skip.reshape(1, gwd), glu_w.astype(BF16), glu_b.reshape(1, gwd)]
    return pl.pallas_call(
        _s5_kernel,
        grid=(seq // tc,),
        in_specs=[pl.BlockSpec((bsz, tc, gwd), lambda i: (0, i, S5_OFF_BLOCKS))] + [full(t) for t in params],
        out_specs=pl.BlockSpec((bsz, tc, gwd), lambda i: (0, i, 0)),
        out_shape=jax.ShapeDtypeStruct((bsz, seq, gwd), BF16),
        scratch_shapes=[pltpu.VMEM((2, tc * bsz, S5_WIDTH), F32), pltpu.VMEM((bsz, 2 * S5_WIDTH), F32)],
        compiler_params=pltpu.CompilerParams(
            dimension_semantics=("arbitrary",), vmem_limit_bytes=VMEM_LIMIT),
        name="s5_scan",
    )(p3, *params)


def _hgrn_kernel(qf_ref, ig_ref, lbraw_ref, gn_ref, tri_ref, bd_ref, o_ref, s_ref, osc_ref, *, layer):
    nb, tb, _ = qf_ref.shape
    cz = HGRN_CHUNK
    gw = GROUP_WIDTH
    n = HEAD_DIM
    nh = gw // n

    qf = qf_ref[...].astype(F32).reshape(nb * tb, 2 * gw)
    ig = ig_ref[...].astype(F32).reshape(nb * tb, 2 * gw)
    q, f = qf[:, :gw], qf[:, gw:]
    v, g = ig[:, :gw], ig[:, gw:]

    lbr = lbraw_ref[...]
    e = jnp.exp(lbr - jnp.max(lbr, axis=0, keepdims=True))
    sm = e / jnp.sum(e, axis=0, keepdims=True)
    lb = jnp.zeros((1, gw), F32)
    for i in range(1, layer + 1):
        lb = lb + sm[i:i + 1, :]

    q = _silu(q) * (n ** -0.5)
    f_gate = lb + (1.0 - lb) * jax.nn.sigmoid(f)
    log_f = jnp.log(f_gate)
    k = 1.0 - f_gate

    parts = _split2(log_f)

    def per_seq(c_ref):
        cm = c_ref[...]
        seq_rows = lambda t, s: t[s * tb:(s + 1) * tb, :]
        return jnp.concatenate(
            [_dot(cm, seq_rows(parts[0], s), _NN) + _dot(cm, seq_rows(parts[1], s), _NN)
             for s in range(nb)], axis=0)

    bcum = per_seq(tri_ref)
    b3 = bcum.reshape(nb * tb // cz, cz, gw)
    chunk_row = lambda r: jnp.broadcast_to(b3[:, r:r + 1, :], b3.shape).reshape(nb * tb, gw)
    bmid = chunk_row(cz // 2 - 1)
    blast = chunk_row(cz - 1)
    qe = q * jnp.exp(bcum - bmid)
    ke = k * jnp.exp(bmid - bcum)
    kl = k * jnp.exp(blast - bcum)
    qb = q * jnp.exp(bcum)
    dec = jnp.exp(blast)

    ri = lax.broadcasted_iota(jnp.int32, (tb, tb), 0)
    ci = lax.broadcasted_iota(jnp.int32, (tb, tb), 1)
    mask = jnp.logical_and(ri // cz == ci // cz, ri >= ci)

    heads = lambda t: jnp.stack(
        [t[s * tb:(s + 1) * tb, h * n:(h + 1) * n] for s in range(nb) for h in range(nh)], axis=0)
    v_h, qb_h, kl_h, dec_h = heads(v), heads(qb), heads(kl), heads(dec)
    att = jnp.where(mask, _bmm(heads(qe), heads(ke), _BNT), 0.0)
    o_intra = _bmm(att, v_h)
    nchunk = tb // cz
    ti = lax.broadcasted_iota(jnp.int32, (tb, nchunk * n), 0)
    li = lax.broadcasted_iota(jnp.int32, (tb, nchunk * n), 1)
    kl_spread = jnp.where(ti // cz == li // n, jnp.concatenate([kl_h] * nchunk, axis=2), 0.0)
    kv_all = _bmm(v_h, kl_spread, _BTN)
    st = s_ref[...]
    o_inter = []
    for j in range(nchunk):
        rows = slice(j * cz, (j + 1) * cz)
        o_inter.append(_bmm(qb_h[:, rows, :], st, _BNT))
        st = st * dec_h[:, j * cz:j * cz + 1, :] + kv_all[:, :, j * n:(j + 1) * n]
    s_ref[...] = st
    o_heads = o_intra + jnp.concatenate(o_inter, axis=1)
    for s in range(nb):
        for h in range(nh):
            osc_ref[s * tb:(s + 1) * tb, h * n:(h + 1) * n] = o_heads[s * nh + h]

    o = osc_ref[...]
    ms = _head_sum(o * o, bd_ref[...]) * (1.0 / n)
    o_ref[...] = (o * lax.rsqrt(ms + NORM_EPS) * gn_ref[...] * _silu(g)).reshape(nb, tb, gw).astype(o_ref.dtype)


def _hgrn_part(p3, nb, lower_bounds, g_norm, layer):
    tb = HGRN_TILE
    cz = HGRN_CHUNK
    gw = GROUP_WIDTH
    idx = np.arange(tb)
    same = (idx[:, None] // cz) == (idx[None, :] // cz)
    tri = same & (idx[:, None] >= idx[None, :])
    consts = [jnp.asarray(tri.astype(np.float32), BF16)]
    ones_bd = jnp.asarray(np.kron(np.eye(4, dtype=np.float32), np.ones((HEAD_DIM, HEAD_DIM), np.float32)), BF16)
    gn = jnp.tile(g_norm.astype(F32), gw // HEAD_DIM).reshape(1, gw)
    params = [lower_bounds.astype(F32), gn] + consts + [ones_bd]
    col = lambda o: pl.BlockSpec((nb, tb, 2 * gw), lambda i, j: (i, j, HGRN_OFF_BLOCKS + o))
    in_specs = [col(0), col(1)] + [_full_spec(t) for t in params]
    scratch = [pltpu.VMEM((nb * gw // HEAD_DIM, HEAD_DIM, HEAD_DIM), F32), pltpu.VMEM((nb * tb, gw), F32)]
    return functools.partial(_hgrn_kernel, layer=layer), in_specs, [p3, p3] + params, scratch


def _hgrn(p3, *params, riders=()):
    bsz, seq, _ = p3.shape
    nb = HGRN_SEQS if bsz % HGRN_SEQS == 0 else 1
    return _mixer_call(_hgrn_part(p3, nb, *params), bsz, seq, nb, HGRN_TILE, "hgrn2", riders)


def kernel(x, positions, ffn1_norm, ffn1_w_gate, ffn1_w_up, ffn1_w_down, mix_norm, w_in, rwkv_mu, rwkv_w0, rwkv_w_up, rwkv_a0, rwkv_a_up, rwkv_g_up, rwkv_k_k, rwkv_k_a, rwkv_r_k, rwkv_ln_w, rwkv_ln_b, attn_q_norm, attn_k_norm, attn_sinks, s5_lambda_re, s5_lambda_im, s5_log_step, s5_b_re, s5_b_im, s5_c_re, s5_c_im, s5_d, s5_glu_w, s5_glu_b, hgrn_lower_bounds, hgrn_g_norm, w_out, ffn2_norm, ffn2_w_gate, ffn2_w_up, ffn2_w_down):
    bsz, seq, d = x.shape
    depth = w_in.shape[0]
    n = bsz * seq
    xf = x.reshape(n, d)
    rope_cos, rope_sin = _rope_tables(positions)
    ffn_w32 = []
    for l in range(depth):
        ffn_w32.append([(w, l) for w in (ffn1_w_gate, ffn1_w_up, ffn1_w_down)])
        ffn_w32.append([(w, l) for w in (ffn2_w_gate, ffn2_w_up, ffn2_w_down)])
    ffn_w32.append([])
    w16 = [w[l].astype(BF16) for w, l in ffn_w32[0]]
    w_in16 = w_in[0].astype(BF16)
    for l in range(depth):
        xf, w16_next = _ffn(xf, ffn1_norm[l], *w16, riders=ffn_w32[2 * l + 1])
        p3 = _proj(xf, mix_norm[l], w_in16).reshape(bsz, seq, D_IN)
        y_a, (w_out16,) = _rwkv(p3, rwkv_mu[l], rwkv_w0[l], rwkv_w_up[l], rwkv_a0[l], rwkv_a_up[l], rwkv_g_up[l],
                                rwkv_k_k[l], rwkv_k_a[l], rwkv_r_k[l].reshape(-1), rwkv_ln_w[l], rwkv_ln_b[l],
                                riders=[(w_out, l)])
        y_b, _ = _attn(p3, rope_cos, rope_sin, attn_q_norm[l], attn_k_norm[l], attn_sinks[l])
        y_c = _s5(p3, s5_lambda_re[l], s5_lambda_im[l], s5_log_step[l], s5_b_re[l], s5_b_im[l],
                  s5_c_re[l], s5_c_im[l], s5_d[l], s5_glu_w[l], s5_glu_b[l])
        y_d, next_in = _hgrn(p3, hgrn_lower_bounds, hgrn_g_norm[l], l,
                             riders=[(w_in, l + 1)] if l + 1 < depth else [])
        if next_in:
            w_in16 = next_in[0]
        ys = [t.reshape(n, GROUP_WIDTH) for t in (y_a, y_b, y_c, y_d)]
        xf = _outproj(xf, ys, w_out16)
        xf, w16 = _ffn(xf, ffn2_norm[l], *w16_next, riders=ffn_w32[2 * l + 2])
    return xf.reshape(bsz, seq, d)
```

```python
import functools

import jax
import jax.numpy as jnp
import numpy as np
from jax import lax
from jax.experimental import pallas as pl
from jax.experimental.pallas import tpu as pltpu

F32 = jnp.float32
BF16 = jnp.bfloat16

HEAD_DIM = 64
GROUP_WIDTH = 512
NORM_EPS = 1e-6
FFN_RES_WEIGHT = 0.5
FFN_NORM_SPLIT = 4

RWKV_W_RANK = 64
RWKV_A_RANK = 64
RWKV_G_RANK = 128
RWKV_LN_EPS = 64e-5
RWKV_IN = 3 * GROUP_WIDTH + RWKV_W_RANK + RWKV_A_RANK + RWKV_G_RANK
RWKV_CHUNK = 64
RWKV_BLOCK = 128
MIXER_SEQS = 4

ATTN_HEADS = 8
ATTN_KV_HEADS = 2
ATTN_BLOCK = 128
ATTN_SEQS = 8
ROPE_THETA = 500000.0
ROPE_DIM = HEAD_DIM // 4
ATTN_COL_BLOCK = 256
ATTN_OFF_BLOCKS = RWKV_IN // ATTN_COL_BLOCK

S5_GROUP = 16
S5_GROUPS = GROUP_WIDTH // S5_GROUP
S5_STATE = 64
S5_WIDTH = S5_GROUPS * S5_STATE
S5_OFF_BLOCKS = (RWKV_IN + 768) // GROUP_WIDTH
S5_STRIP = 512

HGRN_CHUNK = 16
HGRN_TILE = 128
HGRN_SEQS = 8
HGRN_OFF_BLOCKS = 3

D_IN = 5120
VMEM_LIMIT = 56 * 1024 * 1024


def _dot(a, b, dims):
    return lax.dot_general(a, b, (dims, ((), ())), preferred_element_type=F32)


_NN = ((1,), (0,))


def _mm(a, b):
    return _dot(a.astype(BF16), b.astype(BF16), _NN)


_BNN = (((2,), (1,)), ((0,), (0,)))
_BNT = (((2,), (2,)), ((0,), (0,)))
_BTN = (((1,), (1,)), ((0,), (0,)))


def _bmm(a, b, dims=_BNN):
    return lax.dot_general(a.astype(BF16), b.astype(BF16), dims, preferred_element_type=F32)


def _split2(x):
    hi = x.astype(BF16)
    lo = (x - hi.astype(F32)).astype(BF16)
    return hi, lo


def _const_lhs_mm(c, x):
    hi, lo = _split2(x)
    return _dot(c, hi, _NN) + _dot(c, lo, _NN)


def _const_rhs_mm(x, c):
    hi, lo = _split2(x)
    return _dot(hi, c, _NN) + _dot(lo, c, _NN)


def _head_sum(x, ones_bd):
    w = ones_bd.shape[0]
    parts = [_const_rhs_mm(x[:, i:i + w], ones_bd) for i in range(0, x.shape[1], w)]
    return parts[0] if len(parts) == 1 else jnp.concatenate(parts, axis=1)


def _silu(x):
    return x * jax.nn.sigmoid(x)


def _rms_rows(x, gain):
    ms = jnp.mean(x * x, axis=-1, keepdims=True)
    return x * lax.rsqrt(ms + NORM_EPS) * gain


def _ffn_kernel(x_ref, g_ref, wg_ref, wu_ref, wd_ref, o_ref, h_ref):
    def swiglu_step(h, base):
        gate = jnp.dot(h, wg_ref[...], preferred_element_type=F32)
        up = jnp.dot(h, wu_ref[...], preferred_element_type=F32)
        act = (_silu(gate) * up).astype(BF16)
        return base + FFN_RES_WEIGHT * jnp.dot(act, wd_ref[...], preferred_element_type=F32)

    @pl.when(pl.program_id(1) == 0)
    def _():
        sub = x_ref.shape[0] // FFN_NORM_SPLIT
        for s in range(FFN_NORM_SPLIT):
            rows = slice(s * sub, (s + 1) * sub)
            x = x_ref[rows, :]
            h = _rms_rows(x, g_ref[...]).astype(BF16)
            h_ref[rows, :] = h
            o_ref[rows, :] = swiglu_step(h, x)

    @pl.when(pl.program_id(1) > 0)
    def _():
        o_ref[...] = swiglu_step(h_ref[...], o_ref[...])


def _rider_specs(shape, layer, gi, gj):
    _, r, c = shape
    steps = gi * gj
    if r % steps == 0 and (r // steps) % 16 == 0:
        blk, pos = (r // steps, c), lambda i, j: (i * gj + j, 0)
    elif r % gi == 0 and (r // gi) % 16 == 0 and c % gj == 0 and (c // gj) % 128 == 0:
        blk, pos = (r // gi, c // gj), lambda i, j: (i, j)
    else:
        return None
    return pl.BlockSpec((None,) + blk, lambda i, j: (layer,) + pos(i, j)), pl.BlockSpec(blk, pos)


def _ffn(x2d, gain, wg, wu, wd):
    n, d = x2d.shape
    f = wg.shape[1]
    tm = min(1024, n)
    tf = 512 if f % 512 == 0 else f
    return pl.pallas_call(
        _ffn_kernel,
        grid=(n // tm, f // tf),
        in_specs=[
            pl.BlockSpec((tm, d), lambda i, j: (i, 0)),
            pl.BlockSpec((1, d), lambda i, j: (0, 0)),
            pl.BlockSpec((d, tf), lambda i, j: (0, j)),
            pl.BlockSpec((d, tf), lambda i, j: (0, j)),
            pl.BlockSpec((tf, d), lambda i, j: (j, 0)),
        ],
        out_specs=pl.BlockSpec((tm, d), lambda i, j: (i, 0)),
        out_shape=jax.ShapeDtypeStruct((n, d), F32),
        scratch_shapes=[pltpu.VMEM((tm, d), BF16)],
        compiler_params=pltpu.CompilerParams(
            dimension_semantics=("parallel", "arbitrary"), vmem_limit_bytes=VMEM_LIMIT),
        name="ffn",
    )(x2d, gain.reshape(1, d), wg, wu, wd)


def _proj_kernel(x_ref, g_ref, w_ref, o_ref, h_ref):
    project = lambda h: jnp.dot(h, w_ref[...], preferred_element_type=F32).astype(o_ref.dtype)

    @pl.when(pl.program_id(1) == 0)
    def _():
        sub = x_ref.shape[0] // FFN_NORM_SPLIT
        for s in range(FFN_NORM_SPLIT):
            rows = slice(s * sub, (s + 1) * sub)
            h = _rms_rows(x_ref[rows, :], g_ref[...]).astype(BF16)
            h_ref[rows, :] = h
            o_ref[rows, :] = project(h)

    @pl.when(pl.program_id(1) > 0)
    def _():
        o_ref[...] = project(h_ref[...])


def _proj(x2d, gain, w):
    n, d = x2d.shape
    dout = w.shape[1]
    tm = min(1024, n)
    tn = dout // 2
    return pl.pallas_call(
        _proj_kernel,
        grid=(n // tm, dout // tn),
        in_specs=[
            pl.BlockSpec((tm, d), lambda i, j: (i, 0)),
            pl.BlockSpec((1, d), lambda i, j: (0, 0)),
            pl.BlockSpec((d, tn), lambda i, j: (0, j)),
        ],
        out_specs=pl.BlockSpec((tm, tn), lambda i, j: (i, j)),
        out_shape=jax.ShapeDtypeStruct((n, dout), BF16),
        scratch_shapes=[pltpu.VMEM((tm, d), BF16)],
        compiler_params=pltpu.CompilerParams(
            dimension_semantics=("parallel", "arbitrary"), vmem_limit_bytes=VMEM_LIMIT),
        name="in_proj",
    )(x2d, gain.reshape(1, d), w)


def _outproj_kernel(x_ref, ya_ref, yb_ref, yc_ref, yd_ref, w_ref, o_ref):
    gw = GROUP_WIDTH
    acc = x_ref[...]
    for m, y_ref in enumerate((ya_ref, yb_ref, yc_ref, yd_ref)):
        acc = acc + jnp.dot(y_ref[...].astype(BF16), w_ref[m * gw:(m + 1) * gw, :],
                            preferred_element_type=F32)
    o_ref[...] = acc


def _outproj(x2d, ys, w):
    n, d = x2d.shape
    tm = min(512, n)
    yspec = pl.BlockSpec((tm, GROUP_WIDTH), lambda i: (i, 0))
    return pl.pallas_call(
        _outproj_kernel,
        grid=(n // tm,),
        in_specs=[pl.BlockSpec((tm, d), lambda i: (i, 0)), yspec, yspec, yspec, yspec,
                  pl.BlockSpec(w.shape, lambda i: (0, 0))],
        out_specs=pl.BlockSpec((tm, d), lambda i: (i, 0)),
        out_shape=jax.ShapeDtypeStruct((n, d), F32),
        compiler_params=pltpu.CompilerParams(
            dimension_semantics=("parallel",), vmem_limit_bytes=VMEM_LIMIT),
        name="out_proj",
    )(x2d, *ys, w)


def _advance(fillers):
    for gen in fillers:
        if next(gen, _DONE) is not _DONE:
            return


_DONE = object()


def _rwkv_kernel(p_ref, mu_ref, w0_ref, wup_ref, a0_ref, aup_ref, gup_ref, kk_ref, ka_ref, rk_ref,
                 lnw_ref, lnb_ref, tri_ref, bd_ref, o_ref, s_ref, prev_ref, osc_ref):
    c = RWKV_CHUNK
    nck = p_ref.shape[1] // c
    prm = dict(mu=mu_ref, w0=w0_ref, wup=wup_ref, a0=a0_ref, aup=aup_ref, gup=gup_ref, kk=kk_ref, ka=ka_ref,
               rk=rk_ref, lnw=lnw_ref, lnb=lnb_ref, tri=tri_ref, bd=bd_ref)
    chunk = lambda ref, ck: ref.at[:, pl.ds(ck * c, c), :]
    preps = [dict() for _ in range(nck)]
    for _ in _rwkv_prologue(chunk(p_ref, 0), prm, prev_ref, preps[0]):
        pass
    epilogue = iter(())
    for ck in range(nck):
        fillers = [epilogue]
        if ck + 1 < nck:
            fillers.append(_rwkv_prologue(chunk(p_ref, ck + 1), prm, prev_ref, preps[ck + 1]))
        y = _rwkv_solve(preps[ck], s_ref, functools.partial(_advance, fillers))
        for gen in fillers:
            for _ in gen:
                pass
        epilogue = _rwkv_epilogue(preps[ck], y, prm, chunk(o_ref, ck), osc_ref)
    for _ in epilogue:
        pass


def _rwkv_prologue(p_ref, prm, prev_ref, out):
    nb, c, _ = p_ref.shape
    gw = GROUP_WIDTH
    n = HEAD_DIM
    nh = gw // n

    p = p_ref[...].astype(F32).reshape(nb * c, RWKV_IN)
    row = lax.broadcasted_iota(jnp.int32, p.shape, 0)
    shifted = pltpu.roll(p, 1, axis=0)
    for s in range(nb):
        shifted = jnp.where(row == s * c, prev_ref[s], shifted)
        prev_ref[s] = p[(s + 1) * c - 1:(s + 1) * c, :]
    p = p + (shifted - p) * prm["mu"][...]
    yield

    r = p[:, 0:gw]
    k = p[:, gw:2 * gw]
    v = p[:, 2 * gw:3 * gw]
    o1 = 3 * gw
    w_lo = p[:, o1:o1 + RWKV_W_RANK]
    a_lo = p[:, o1 + RWKV_W_RANK:o1 + RWKV_W_RANK + RWKV_A_RANK]
    g_lo = p[:, o1 + RWKV_W_RANK + RWKV_A_RANK:]

    z = -(prm["w0"][...] + _mm(jnp.tanh(w_lo), prm["wup"][...]))
    softplus = jnp.maximum(z, 0.0) + jnp.log1p(jnp.exp(-jnp.abs(z)))
    lw = -jnp.exp(-softplus - 0.5)
    yield
    a = jax.nn.sigmoid(prm["a0"][...] + _mm(a_lo, prm["aup"][...]))
    g = _mm(jax.nn.sigmoid(g_lo), prm["gup"][...])
    yield

    kk = k * prm["kk"][...]
    kk = kk * lax.rsqrt(jnp.maximum(_head_sum(kk * kk, prm["bd"][...]), 1e-24))
    yield
    k = k * (1.0 + (a - 1.0) * prm["ka"][...])
    b = kk * a
    cum = _const_lhs_mm(prm["tri"][...], lw)
    yield
    cum_last = jnp.concatenate(
        [jnp.broadcast_to(cum[(s + 1) * c - 1:(s + 1) * c, :], (c, gw)) for s in range(nb)], axis=0)
    heads = lambda t, rows=c: jnp.stack(
        [t[s * c:s * c + rows, h * n:(h + 1) * n] for s in range(nb) for h in range(nh)], axis=0)
    mxu_heads = lambda t: heads(t.astype(BF16))
    out.update(r=r, k=k, v=v, g=g)
    e_neg = jnp.exp(-cum)
    out["bd_h"] = mxu_heads(b * e_neg)
    yield
    out["kd_h"] = mxu_heads(k * e_neg)
    yield
    out["kq_h"] = mxu_heads(kk * jnp.exp(cum - lw))
    yield
    out["rq_h"] = mxu_heads(r * jnp.exp(cum))
    yield
    out["v_h"] = mxu_heads(v)
    yield
    e_end = jnp.exp(cum_last - cum)
    out["be_h"] = mxu_heads(b * e_end)
    yield
    out["ke_h"] = mxu_heads(k * e_end)
    out["ge_h"] = heads(jnp.exp(cum_last), 1)
    yield


def _rwkv_solve(pp, s_ref, tick):
    kq_h, rq_h, bd_h, kd_h, v_h = pp["kq_h"], pp["rq_h"], pp["bd_h"], pp["kd_h"], pp["v_h"]
    c = kq_h.shape[1]
    ri = lax.broadcasted_iota(jnp.int32, (c, c), 0)
    ci = lax.broadcasted_iota(jnp.int32, (c, c), 1)
    strict = ri > ci
    incl = ri >= ci
    eye = (ri == ci).astype(F32)

    s0 = s_ref[...]
    gmat = _bmm(jnp.concatenate([kq_h, rq_h], axis=1), jnp.concatenate([bd_h, kd_h], axis=1), _BNT)
    tick()
    a_bb = jnp.where(strict, gmat[:, :c, :c], 0.0)
    a_bk = jnp.where(strict, gmat[:, :c, c:], 0.0)
    a_rb = jnp.where(incl, gmat[:, c:, :c], 0.0)
    a_rk = jnp.where(incl, gmat[:, c:, c:], 0.0)
    pw = -a_bb
    t_inv = eye + pw
    for _ in range(int(np.log2(c)) - 1):
        pw = _bmm(pw, pw)
        tick()
        t_inv = t_inv + _bmm(t_inv, pw)
        tick()
    x = _bmm(kq_h, s0, _BNT)
    tick()
    x = x + _bmm(a_bk, v_h)
    tick()
    u = -_bmm(t_inv, x)
    tick()
    y = _bmm(rq_h, s0, _BNT)
    tick()
    y = y + _bmm(a_rb, u)
    tick()
    y = y + _bmm(a_rk, v_h)
    tick()
    s_ref[...] = s0 * pp["ge_h"] + _bmm(jnp.concatenate([u.astype(BF16), v_h], axis=1),
                                        jnp.concatenate([pp["be_h"], pp["ke_h"]], axis=1), _BTN)
    tick()
    return y


def _rwkv_epilogue(pp, y, prm, o_ref, osc_ref):
    nb, c, gw = o_ref.shape
    n = HEAD_DIM
    nh = gw // n
    ones_bd = prm["bd"][...]
    for s in range(nb):
        for h in range(nh):
            osc_ref[s * c:(s + 1) * c, h * n:(h + 1) * n] = y[s * nh + h]
    yield
    o = osc_ref[...]
    inv_n = 1.0 / n
    mean = _head_sum(o, ones_bd) * inv_n
    yield
    dlt = o - mean
    var = _head_sum(dlt * dlt, ones_bd) * inv_n
    yield
    o = dlt * lax.rsqrt(var + RWKV_LN_EPS) * prm["lnw"][...] + prm["lnb"][...]
    yield
    bonus = _head_sum(pp["r"] * pp["k"] * prm["rk"][...], ones_bd) * pp["v"]
    yield
    o_ref[...] = ((o + bonus) * pp["g"]).reshape(nb, c, gw).astype(o_ref.dtype)
    yield


def _full_spec(t):
    return pl.BlockSpec(t.shape, lambda i, j: (0,) * t.ndim)


def _mixer_call(part, bsz, seq, nb, tblk, name, riders=()):
    body, in_specs, operands, scratch = part
    gi, gj = bsz // nb, seq // tblk
    specs = [_rider_specs(w.shape, layer, gi, gj) for w, layer in riders]
    ride = [k for k, s in enumerate(specs) if s is not None]
    n_in, n_ride = len(operands), len(ride)

    def kern(*refs):
        ins, rider_in = refs[:n_in], refs[n_in:n_in + n_ride]
        out, rider_out = refs[n_in + n_ride], refs[n_in + n_ride + 1:n_in + 2 * n_ride + 1]
        scr = refs[n_in + 2 * n_ride + 1:]

        @pl.when(pl.program_id(1) == 0)
        def _():
            for ref in scr:
                ref[...] = jnp.zeros_like(ref)

        body(*ins, out, *scr)
        for src, dst in zip(rider_in, rider_out):
            dst[...] = src[...].astype(BF16)

    out_spec = pl.BlockSpec((nb, tblk, GROUP_WIDTH), lambda i, j: (i, j, 0))
    outs = pl.pallas_call(
        kern,
        grid=(gi, gj),
        in_specs=list(in_specs) + [specs[k][0] for k in ride],
        out_specs=[out_spec] + [specs[k][1] for k in ride],
        out_shape=[jax.ShapeDtypeStruct((bsz, seq, GROUP_WIDTH), BF16)]
        + [jax.ShapeDtypeStruct(riders[k][0].shape[1:], BF16) for k in ride],
        scratch_shapes=scratch,
        compiler_params=pltpu.CompilerParams(
            dimension_semantics=("parallel", "arbitrary"), vmem_limit_bytes=VMEM_LIMIT),
        name=name,
    )(*operands, *[riders[k][0] for k in ride])
    cast = [outs[1 + ride.index(k)] if k in ride else w[layer].astype(BF16)
            for k, (w, layer) in enumerate(riders)]
    return outs[0], cast


def _rwkv_part(p3, nb, tblk, mu, w0, w_up, a0, a_up, g_up, k_k, k_a, r_k, ln_w, ln_b):
    c = RWKV_CHUNK
    gw = GROUP_WIDTH
    tri = jnp.asarray(np.kron(np.eye(nb, dtype=np.float32), np.tril(np.ones((c, c), np.float32))), BF16)
    ones_bd = jnp.asarray(np.kron(np.eye(4, dtype=np.float32), np.ones((HEAD_DIM, HEAD_DIM), np.float32)), BF16)
    row = lambda t: t.reshape(1, -1)
    params = [row(mu), row(w0), w_up, row(a0), a_up, g_up, row(k_k), row(k_a), row(r_k), row(ln_w),
              row(ln_b), tri, ones_bd]
    in_specs = [pl.BlockSpec((nb, tblk, RWKV_IN), lambda i, j: (i, j, 0))] + [_full_spec(t) for t in params]
    scratch = [pltpu.VMEM((nb * gw // HEAD_DIM, HEAD_DIM, HEAD_DIM), F32),
               pltpu.VMEM((nb, 1, RWKV_IN), F32),
               pltpu.VMEM((nb * c, gw), F32)]
    return _rwkv_kernel, in_specs, [p3] + params, scratch


def _rwkv(p3, *params, riders=()):
    bsz, seq, _ = p3.shape
    nb = MIXER_SEQS if bsz % MIXER_SEQS == 0 else 1
    tblk = min(RWKV_BLOCK, seq)
    return _mixer_call(_rwkv_part(p3, nb, tblk, *params), bsz, seq, nb, tblk, "rwkv7", riders)


def _rope_table_kernel(pos_ref, invf_ref, *rest):
    n_riders = (len(rest) - 2) // 2
    cos_ref, sin_ref = rest[n_riders], rest[n_riders + 1]
    ang = pos_ref[...].astype(F32) * invf_ref[...]
    cos_ref[...] = jnp.cos(ang)
    sin_ref[...] = jnp.sin(ang)
    for src, dst in zip(rest[:n_riders], rest[n_riders + 2:]):
        dst[...] = src[...].astype(BF16)


def _rope_tables(positions, riders=()):
    bsz, seq = positions.shape
    d = HEAD_DIM
    w = 2 * d
    inv_freq = ROPE_THETA ** (-jnp.arange(0, ROPE_DIM, 2, dtype=F32) / ROPE_DIM)
    lane_d = np.arange(w) % d
    invf = jnp.where(lane_d < ROPE_DIM, inv_freq[lane_d % (ROPE_DIM // 2)], 0.0).reshape(1, w)
    tr = min(1024, bsz * seq)
    steps = bsz * seq // tr
    specs = [_rider_specs(wt.shape, layer, steps, 1) for wt, layer in riders]
    ride = [k for k, s in enumerate(specs) if s is not None]
    table = pl.BlockSpec((tr, w), lambda i, j: (i, 0))
    outs = pl.pallas_call(
        _rope_table_kernel,
        grid=(steps, 1),
        in_specs=[pl.BlockSpec((tr, 1), lambda i, j: (i, 0)), pl.BlockSpec((1, w), lambda i, j: (0, 0))]
        + [specs[k][0] for k in ride],
        out_specs=[table, table] + [specs[k][1] for k in ride],
        out_shape=[jax.ShapeDtypeStruct((bsz * seq, w), F32)] * 2
        + [jax.ShapeDtypeStruct(riders[k][0].shape[1:], BF16) for k in ride],
        compiler_params=pltpu.CompilerParams(
            dimension_semantics=("parallel", "arbitrary"), vmem_limit_bytes=VMEM_LIMIT),
        name="rope_tables",
    )(positions.reshape(bsz * seq, 1), invf, *[riders[k][0] for k in ride])
    cast = [outs[2 + ride.index(k)] if k in ride else wt[layer].astype(BF16)
            for k, (wt, layer) in enumerate(riders)]
    return outs[0], outs[1], cast


def _attn_kernel(cos_ref, sin_ref, q0_ref, q1_ref, kv_ref, qn_ref, kn_ref, sink_ref, bd_ref,
                 o_ref, kprev_ref, vprev_ref):
    nb, blk, _ = q0_ref.shape
    rows = nb * blk
    d = HEAD_DIM
    half = ROPE_DIM // 2
    group = ATTN_HEADS // ATTN_KV_HEADS
    kvw = ATTN_KV_HEADS * d
    first = pl.program_id(1) == 0
    ones_bd = bd_ref[...]
    cos = cos_ref[...].reshape(rows, 2 * d)
    sin = sin_ref[...].reshape(rows, 2 * d)

    def norm_rope(x, gain):
        w = x.shape[1]
        x = x * lax.rsqrt(_head_sum(x * x, ones_bd[:w, :w]) * (1.0 / d) + NORM_EPS) * gain
        ld = lax.broadcasted_iota(jnp.int32, (rows, w), 1) % d
        rot = jnp.where(ld < half, -pltpu.roll(x, w - half, axis=1),
                        jnp.where(ld < ROPE_DIM, pltpu.roll(x, half, axis=1), 0.0))
        tile = lambda t: t if w == t.shape[1] else jnp.concatenate([t] * (w // t.shape[1]), axis=1)
        return x * tile(cos) + rot * tile(sin)

    qs = [norm_rope(q_ref[...].astype(F32).reshape(rows, ATTN_COL_BLOCK), qn_ref[...]) * (d ** -0.5)
          for q_ref in (q0_ref, q1_ref)]
    kv = kv_ref[...].astype(F32).reshape(rows, ATTN_COL_BLOCK)
    k_cur = norm_rope(kv[:, :kvw], kn_ref[:, :kvw])
    v_cur = kv[:, kvw:]
    k_prev = kprev_ref[...]
    v_prev = vprev_ref[...]

    heads_per_ref = ATTN_COL_BLOCK // d
    q_head = lambda s, qh: qs[qh // heads_per_ref][s * blk:(s + 1) * blk,
                                                   (qh % heads_per_ref) * d:(qh % heads_per_ref + 1) * d]
    q_g = jnp.stack([jnp.concatenate([q_head(s, kh * group + i) for i in range(group)], axis=0)
                     for s in range(nb) for kh in range(ATTN_KV_HEADS)], axis=0)
    kv_heads = lambda t: jnp.stack([t[s * blk:(s + 1) * blk, kh * d:(kh + 1) * d]
                                    for s in range(nb) for kh in range(ATTN_KV_HEADS)], axis=0)

    ri = lax.broadcasted_iota(jnp.int32, (group * blk, blk), 0) % blk
    ci = lax.broadcasted_iota(jnp.int32, (group * blk, blk), 1)
    neg_inf = -jnp.inf
    no_prev = jnp.where(first, neg_inf, 0.0)
    s_prev = jnp.where(ci > ri, _bmm(q_g, kv_heads(k_prev), _BNT), neg_inf) + no_prev
    s_cur = jnp.where(ci <= ri, _bmm(q_g, kv_heads(k_cur), _BNT), neg_inf)
    sink = sink_ref[...]
    m = jnp.maximum(jnp.maximum(jnp.max(s_prev, axis=2, keepdims=True),
                                jnp.max(s_cur, axis=2, keepdims=True)), sink)
    p_prev = jnp.exp(s_prev - m)
    p_cur = jnp.exp(s_cur - m)
    ones = jnp.ones((nb * ATTN_KV_HEADS, blk, d), BF16)
    den = _bmm(p_prev, ones) + _bmm(p_cur, ones) + jnp.exp(sink - m)
    o = ((_bmm(p_prev, kv_heads(v_prev)) + _bmm(p_cur, kv_heads(v_cur))) / den).astype(o_ref.dtype)
    for s in range(nb):
        for qh in range(ATTN_HEADS):
            o_ref[s, :, qh * d:(qh + 1) * d] = o[s * ATTN_KV_HEADS + qh // group,
                                                 (qh % group) * blk:(qh % group + 1) * blk, :]

    kprev_ref[...] = k_cur
    vprev_ref[...] = v_cur


def _attn_part(p3, nb, rope_cos, rope_sin, q_norm, k_norm, sinks):
    bsz, seq, _ = p3.shape
    blk = ATTN_BLOCK
    cw = ATTN_COL_BLOCK
    d = HEAD_DIM
    qn = jnp.tile(q_norm.astype(F32), cw // d).reshape(1, cw)
    kn = jnp.tile(k_norm.astype(F32), cw // d).reshape(1, cw)
    ones_bd = jnp.asarray(np.kron(np.eye(cw // d, dtype=np.float32), np.ones((d, d), np.float32)), BF16)
    group = ATTN_HEADS // ATTN_KV_HEADS
    sink_col = jnp.repeat(sinks.astype(F32).reshape(ATTN_KV_HEADS, group), blk, axis=1)[:, :, None]
    sink_col = jnp.tile(sink_col, (nb, 1, 1))
    tw = rope_cos.shape[-1]
    cos3 = rope_cos.reshape(bsz, seq, tw)
    sin3 = rope_sin.reshape(bsz, seq, tw)
    col = lambda o: pl.BlockSpec((nb, blk, cw), lambda i, j: (i, j, ATTN_OFF_BLOCKS + o))
    trig = pl.BlockSpec((nb, blk, tw), lambda i, j: (i, j, 0))
    consts = [qn, kn, sink_col, ones_bd]
    in_specs = [trig, trig, col(0), col(1), col(2)] + [_full_spec(t) for t in consts]
    scratch = [pltpu.VMEM((nb * blk, ATTN_KV_HEADS * d), F32), pltpu.VMEM((nb * blk, ATTN_KV_HEADS * d), F32)]
    return _attn_kernel, in_specs, [cos3, sin3, p3, p3, p3] + consts, scratch


def _attn(p3, *params, riders=()):
    bsz, seq, _ = p3.shape
    nb = ATTN_SEQS if bsz % ATTN_SEQS == 0 else 1
    return _mixer_call(_attn_part(p3, nb, *params), bsz, seq, nb, ATTN_BLOCK, "swa_attn", riders)


def _s5_prep_kernel(lre_ref, lim_ref, ls_ref, bre_ref, bim_ref, are_ref, aim_ref, obre_ref, obim_ref):
    lre = lre_ref[...]
    lim = lim_ref[...]
    dt = jnp.exp(ls_ref[...])
    mag = jnp.exp(lre * dt)
    are = mag * jnp.cos(lim * dt)
    aim = mag * jnp.sin(lim * dt)
    are_ref[...] = are
    aim_ref[...] = aim
    inv = 1.0 / (lre * lre + lim * lim)
    cre = ((are - 1.0) * lre + aim * lim) * inv
    cim = (aim * lre - (are - 1.0) * lim) * inv
    bre = bre_ref[...]
    bim = bim_ref[...]
    obre_ref[...] = cre[:, None, :] * bre - cim[:, None, :] * bim
    obim_ref[...] = cre[:, None, :] * bim + cim[:, None, :] * bre


def _s5_kernel(u_ref, are_ref, aim_ref, b_ref, c_ref, d_ref, gw_ref, gb_ref, o_ref, x_ref, s_ref):
    bsz, tc, ch = u_ref.shape
    hw = S5_WIDTH // 2
    hc = ch // 2

    @pl.when(pl.program_id(0) == 0)
    def _():
        s_ref[...] = jnp.zeros_like(s_ref)

    u = jnp.swapaxes(u_ref[...].astype(F32), 0, 1).reshape(tc * bsz, ch)

    def scan(hf):
        xh = x_ref.at[hf]
        for s0 in range(0, hw, S5_STRIP):
            re = slice(s0, s0 + S5_STRIP)
            im = slice(hw + s0, hw + s0 + S5_STRIP)
            lam = slice(hw * hf + s0, hw * hf + s0 + S5_STRIP)
            ar = jnp.broadcast_to(are_ref[:, lam], (bsz, S5_STRIP))
            ai = jnp.broadcast_to(aim_ref[:, lam], (bsz, S5_STRIP))
            sre = slice(2 * hw * hf + s0, 2 * hw * hf + s0 + S5_STRIP)
            sim = slice(2 * hw * hf + hw + s0, 2 * hw * hf + hw + s0 + S5_STRIP)
            sr, si = s_ref[:, sre], s_ref[:, sim]
            for t in range(tc):
                rows = slice(t * bsz, (t + 1) * bsz)
                sr, si = ar * sr - ai * si + xh[rows, re], ar * si + ai * sr + xh[rows, im]
                xh[rows, re] = sr
                xh[rows, im] = si
            s_ref[:, sre] = sr
            s_ref[:, sim] = si

    x_ref[0] = _mm(u[:, :hc], b_ref[0])
    x_ref[1] = _mm(u[:, hc:], b_ref[1])
    scan(0)
    y0 = _mm(x_ref[0], c_ref[0])
    scan(1)
    y = jnp.concatenate([y0, _mm(x_ref[1], c_ref[1])], axis=1)
    y = y + d_ref[...] * u
    z = 0.5 * y * (1.0 + lax.erf(y * (2.0 ** -0.5)))
    out = z * jax.nn.sigmoid(_mm(z, gw_ref[...]) + gb_ref[...])
    o_ref[...] = jnp.swapaxes(out.reshape(tc, bsz, ch), 0, 1).astype(o_ref.dtype)


def _s5(p3, lam_re, lam_im, log_step, b_re, b_im, c_re, c_im, d_skip, glu_w, glu_b):
    bsz, seq, _ = p3.shape
    g, st, ch = S5_GROUPS, S5_STATE, S5_GROUP
    gwd = GROUP_WIDTH
    vm = pl.BlockSpec(memory_space=pltpu.VMEM)
    a_re, a_im, bb_re, bb_im = pl.pallas_call(
        _s5_prep_kernel,
        in_specs=[vm] * 5,
        out_specs=[vm] * 4,
        out_shape=[jax.ShapeDtypeStruct((g, st), F32)] * 2 + [jax.ShapeDtypeStruct((g, ch, st), F32)] * 2,
        name="s5_prep",
    )(lam_re, lam_im, log_step.reshape(g, 1), jnp.swapaxes(b_re, 1, 2), jnp.swapaxes(b_im, 1, 2))

    gh = g // 2
    eye = jnp.eye(gh, dtype=F32)
    bd_in = lambda t: (t[:, :, None, :] * eye[:, None, :, None]).reshape(gh * ch, gh * st)
    bd_out = lambda t: (jnp.swapaxes(t, 1, 2)[:, :, None, :] * eye[:, None, :, None]).reshape(gh * st, gh * ch)
    halves = lambda t: (t[:gh], t[gh:])
    b_mat = jnp.stack([jnp.concatenate([bd_in(r), bd_in(i)], axis=1)
                       for r, i in zip(halves(bb_re), halves(bb_im))]).astype(BF16)
    c_mat = jnp.stack([jnp.concatenate([bd_out(r), -bd_out(i)], axis=0)
                       for r, i in zip(halves(c_re.astype(F32)), halves(c_im.astype(F32)))]).astype(BF16)

    tc = min(64, seq)
    full = lambda t: pl.BlockSpec(t.shape, lambda i: (0,) * t.ndim)
    params = [a_re.reshape(1, S5_WIDTH), a_im.reshape(1, S5_WIDTH), b_mat, c_mat,
              d_skip.reshape(1, gwd), glu_w.astype(BF16), glu_b.reshape(1, gwd)]
    return pl.pallas_call(
        _s5_kernel,
        grid=(seq // tc,),
        in_specs=[pl.BlockSpec((bsz, tc, gwd), lambda i: (0, i, S5_OFF_BLOCKS))] + [full(t) for t in params],
        out_specs=pl.BlockSpec((bsz, tc, gwd), lambda i: (0, i, 0)),
        out_shape=jax.ShapeDtypeStruct((bsz, seq, gwd), BF16),
        scratch_shapes=[pltpu.VMEM((2, tc * bsz, S5_WIDTH), F32), pltpu.VMEM((bsz, 2 * S5_WIDTH), F32)],
        compiler_params=pltpu.CompilerParams(
            dimension_semantics=("arbitrary",), vmem_limit_bytes=VMEM_LIMIT),
        name="s5_scan",
    )(p3, *params)


def _hgrn_kernel(qf_ref, ig_ref, lbraw_ref, gn_ref, tri_ref, bd_ref, o_ref, s_ref, osc_ref, *, layer):
    nb, tb, _ = qf_ref.shape
    cz = HGRN_CHUNK
    gw = GROUP_WIDTH
    n = HEAD_DIM
    nh = gw // n

    qf = qf_ref[...].astype(F32).reshape(nb * tb, 2 * gw)
    ig = ig_ref[...].astype(F32).reshape(nb * tb, 2 * gw)
    q, f = qf[:, :gw], qf[:, gw:]
    v, g = ig[:, :gw], ig[:, gw:]

    lbr = lbraw_ref[...]
    e = jnp.exp(lbr - jnp.max(lbr, axis=0, keepdims=True))
    sm = e / jnp.sum(e, axis=0, keepdims=True)
    lb = jnp.zeros((1, gw), F32)
    for i in range(1, layer + 1):
        lb = lb + sm[i:i + 1, :]

    q = _silu(q) * (n ** -0.5)
    f_gate = lb + (1.0 - lb) * jax.nn.sigmoid(f)
    log_f = jnp.log(f_gate)
    k = 1.0 - f_gate

    parts = _split2(log_f)

    def per_seq(c_ref):
        cm = c_ref[...]
        seq_rows = lambda t, s: t[s * tb:(s + 1) * tb, :]
        return jnp.concatenate(
            [_dot(cm, seq_rows(parts[0], s), _NN) + _dot(cm, seq_rows(parts[1], s), _NN)
             for s in range(nb)], axis=0)

    bcum = per_seq(tri_ref)
    b3 = bcum.reshape(nb * tb // cz, cz, gw)
    chunk_row = lambda r: jnp.broadcast_to(b3[:, r:r + 1, :], b3.shape).reshape(nb * tb, gw)
    bmid = chunk_row(cz // 2 - 1)
    blast = chunk_row(cz - 1)
    qe = q * jnp.exp(bcum - bmid)
    ke = k * jnp.exp(bmid - bcum)
    kl = k * jnp.exp(blast - bcum)
    qb = q * jnp.exp(bcum)
    dec = jnp.exp(blast)

    ri = lax.broadcasted_iota(jnp.int32, (tb, tb), 0)
    ci = lax.broadcasted_iota(jnp.int32, (tb, tb), 1)
    mask = jnp.logical_and(ri // cz == ci // cz, ri >= ci)

    heads = lambda t: jnp.stack(
        [t[s * tb:(s + 1) * tb, h * n:(h + 1) * n] for s in range(nb) for h in range(nh)], axis=0)
    v_h, qb_h, kl_h, dec_h = heads(v), heads(qb), heads(kl), heads(dec)
    att = jnp.where(mask, _bmm(heads(qe), heads(ke), _BNT), 0.0)
    o_intra = _bmm(att, v_h)
    nchunk = tb // cz
    ti = lax.broadcasted_iota(jnp.int32, (tb, nchunk * n), 0)
    li = lax.broadcasted_iota(jnp.int32, (tb, nchunk * n), 1)
    kl_spread = jnp.where(ti // cz == li // n, jnp.concatenate([kl_h] * nchunk, axis=2), 0.0)
    kv_all = _bmm(v_h, kl_spread, _BTN)
    st = s_ref[...]
    o_inter = []
    for j in range(nchunk):
        rows = slice(j * cz, (j + 1) * cz)
        o_inter.append(_bmm(qb_h[:, rows, :], st, _BNT))
        st = st * dec_h[:, j * cz:j * cz + 1, :] + kv_all[:, :, j * n:(j + 1) * n]
    s_ref[...] = st
    o_heads = o_intra + jnp.concatenate(o_inter, axis=1)
    for s in range(nb):
        for h in range(nh):
            osc_ref[s * tb:(s + 1) * tb, h * n:(h + 1) * n] = o_heads[s * nh + h]

    o = osc_ref[...]
    ms = _head_sum(o * o, bd_ref[...]) * (1.0 / n)
    o_ref[...] = (o * lax.rsqrt(ms + NORM_EPS) * gn_ref[...] * _silu(g)).reshape(nb, tb, gw).astype(o_ref.dtype)


def _hgrn_part(p3, nb, lower_bounds, g_norm, layer):
    tb = HGRN_TILE
    cz = HGRN_CHUNK
    gw = GROUP_WIDTH
    idx = np.arange(tb)
    same = (idx[:, None] // cz) == (idx[None, :] // cz)
    tri = same & (idx[:, None] >= idx[None, :])
    consts = [jnp.asarray(tri.astype(np.float32), BF16)]
    ones_bd = jnp.asarray(np.kron(np.eye(4, dtype=np.float32), np.ones((HEAD_DIM, HEAD_DIM), np.float32)), BF16)
    gn = jnp.tile(g_norm.astype(F32), gw // HEAD_DIM).reshape(1, gw)
    params = [lower_bounds.astype(F32), gn] + consts + [ones_bd]
    col = lambda o: pl.BlockSpec((nb, tb, 2 * gw), lambda i, j: (i, j, HGRN_OFF_BLOCKS + o))
    in_specs = [col(0), col(1)] + [_full_spec(t) for t in params]
    scratch = [pltpu.VMEM((nb * gw // HEAD_DIM, HEAD_DIM, HEAD_DIM), F32), pltpu.VMEM((nb * tb, gw), F32)]
    return functools.partial(_hgrn_kernel, layer=layer), in_specs, [p3, p3] + params, scratch


def _hgrn(p3, *params, riders=()):
    bsz, seq, _ = p3.shape
    nb = HGRN_SEQS if bsz % HGRN_SEQS == 0 else 1
    return _mixer_call(_hgrn_part(p3, nb, *params), bsz, seq, nb, HGRN_TILE, "hgrn2", riders)


def kernel(x, positions, ffn1_norm, ffn1_w_gate, ffn1_w_up, ffn1_w_down, mix_norm, w_in, rwkv_mu, rwkv_w0, rwkv_w_up, rwkv_a0, rwkv_a_up, rwkv_g_up, rwkv_k_k, rwkv_k_a, rwkv_r_k, rwkv_ln_w, rwkv_ln_b, attn_q_norm, attn_k_norm, attn_sinks, s5_lambda_re, s5_lambda_im, s5_log_step, s5_b_re, s5_b_im, s5_c_re, s5_c_im, s5_d, s5_glu_w, s5_glu_b, hgrn_lower_bounds, hgrn_g_norm, w_out, ffn2_norm, ffn2_w_gate, ffn2_w_up, ffn2_w_down):
    bsz, seq, d = x.shape
    depth = w_in.shape[0]
    n = bsz * seq
    xf = x.reshape(n, d)
    rope_cos, rope_sin, (g1, u1, d1, w_in16) = _rope_tables(
        positions, riders=[(ffn1_w_gate, 0), (ffn1_w_up, 0), (ffn1_w_down, 0), (w_in, 0)])
    for l in range(depth):
        more = l + 1 < depth
        xf = _ffn(xf, ffn1_norm[l], g1, u1, d1)
        p3 = _proj(xf, mix_norm[l], w_in16).reshape(bsz, seq, D_IN)
        y_a, (w_out16, g2, u2) = _rwkv(
            p3, rwkv_mu[l], rwkv_w0[l], rwkv_w_up[l], rwkv_a0[l], rwkv_a_up[l], rwkv_g_up[l],
            rwkv_k_k[l], rwkv_k_a[l], rwkv_r_k[l].reshape(-1), rwkv_ln_w[l], rwkv_ln_b[l],
            riders=[(w_out, l), (ffn2_w_gate, l), (ffn2_w_up, l)])
        y_b, cast_b = _attn(p3, rope_cos, rope_sin, attn_q_norm[l], attn_k_norm[l], attn_sinks[l],
                            riders=[(ffn2_w_down, l)] + ([(ffn1_w_down, l + 1)] if more else []))
        y_c = _s5(p3, s5_lambda_re[l], s5_lambda_im[l], s5_log_step[l], s5_b_re[l], s5_b_im[l],
                  s5_c_re[l], s5_c_im[l], s5_d[l], s5_glu_w[l], s5_glu_b[l])
        y_d, cast_d = _hgrn(p3, hgrn_lower_bounds, hgrn_g_norm[l], l,
                            riders=[(w_in, l + 1), (ffn1_w_gate, l + 1), (ffn1_w_up, l + 1)] if more else [])
        ys = [t.reshape(n, GROUP_WIDTH) for t in (y_a, y_b, y_c, y_d)]
        xf = _outproj(xf, ys, w_out16)
        xf = _ffn(xf, ffn2_norm[l], g2, u2, cast_b[0])
        if more:
            d1 = cast_b[1]
            w_in16, g1, u1 = cast_d
    return xf.reshape(bsz, seq, d)
```

```python
import functools

import jax
import jax.numpy as jnp
import numpy as np
from jax import lax
from jax.experimental import pallas as pl
from jax.experimental.pallas import tpu as pltpu

F32 = jnp.float32
BF16 = jnp.bfloat16

HEAD_DIM = 64
GROUP_WIDTH = 512
NORM_EPS = 1e-6
FFN_RES_WEIGHT = 0.5
FFN_NORM_SPLIT = 4

RWKV_W_RANK = 64
RWKV_A_RANK = 64
RWKV_G_RANK = 128
RWKV_LN_EPS = 64e-5
RWKV_IN = 3 * GROUP_WIDTH + RWKV_W_RANK + RWKV_A_RANK + RWKV_G_RANK
RWKV_CHUNK = 64
RWKV_BLOCK = 128
MIXER_SEQS = 4

ATTN_HEADS = 8
ATTN_KV_HEADS = 2
ATTN_BLOCK = 128
ATTN_SEQS = 8
ROPE_THETA = 500000.0
ROPE_DIM = HEAD_DIM // 4
ATTN_COL_BLOCK = 256
ATTN_OFF_BLOCKS = RWKV_IN // ATTN_COL_BLOCK

S5_GROUP = 16
S5_GROUPS = GROUP_WIDTH // S5_GROUP
S5_STATE = 64
S5_WIDTH = S5_GROUPS * S5_STATE
S5_OFF_BLOCKS = (RWKV_IN + 768) // GROUP_WIDTH
S5_STRIP = 512

HGRN_CHUNK = 16
HGRN_TILE = 128
HGRN_SEQS = 8
HGRN_OFF_BLOCKS = 3

D_IN = 5120
VMEM_LIMIT = 56 * 1024 * 1024


def _dot(a, b, dims):
    return lax.dot_general(a, b, (dims, ((), ())), preferred_element_type=F32)


_NN = ((1,), (0,))


def _mm(a, b):
    return _dot(a.astype(BF16), b.astype(BF16), _NN)


_BNN = (((2,), (1,)), ((0,), (0,)))
_BNT = (((2,), (2,)), ((0,), (0,)))
_BTN = (((1,), (1,)), ((0,), (0,)))


def _bmm(a, b, dims=_BNN):
    return lax.dot_general(a.astype(BF16), b.astype(BF16), dims, preferred_element_type=F32)


def _split2(x):
    hi = x.astype(BF16)
    lo = (x - hi.astype(F32)).astype(BF16)
    return hi, lo


def _const_lhs_mm(c, x):
    hi, lo = _split2(x)
    return _dot(c, hi, _NN) + _dot(c, lo, _NN)


def _const_rhs_mm(x, c):
    hi, lo = _split2(x)
    return _dot(hi, c, _NN) + _dot(lo, c, _NN)


def _head_sum(x, ones_bd):
    w = ones_bd.shape[0]
    parts = [_const_rhs_mm(x[:, i:i + w], ones_bd) for i in range(0, x.shape[1], w)]
    return parts[0] if len(parts) == 1 else jnp.concatenate(parts, axis=1)


def _silu(x):
    return x * jax.nn.sigmoid(x)


def _rms_rows(x, gain):
    ms = jnp.mean(x * x, axis=-1, keepdims=True)
    return x * lax.rsqrt(ms + NORM_EPS) * gain


def _ffn_kernel(x_ref, g_ref, wg_hbm, wu_hbm, wd_hbm, o_ref, h_ref, *, tf):
    d, f = wg_hbm.shape
    x = x_ref[...]
    h_ref[...] = _rms_rows(x, g_ref[...]).astype(BF16)
    o_ref[...] = x

    def f_tile(wg_ref, wu_ref, wd_ref):
        h = h_ref[...]
        gate = jnp.dot(h, wg_ref[...], preferred_element_type=F32)
        up = jnp.dot(h, wu_ref[...], preferred_element_type=F32)
        act = (_silu(gate) * up).astype(BF16)
        o_ref[...] += FFN_RES_WEIGHT * jnp.dot(act, wd_ref[...], preferred_element_type=F32)

    pltpu.emit_pipeline(
        f_tile,
        grid=(f // tf,),
        in_specs=[pl.BlockSpec((d, tf), lambda j: (0, j)),
                  pl.BlockSpec((d, tf), lambda j: (0, j)),
                  pl.BlockSpec((tf, d), lambda j: (j, 0))],
    )(wg_hbm, wu_hbm, wd_hbm)


def _rider_specs(shape, layer, gi, gj):
    _, r, c = shape
    steps = gi * gj
    if r % steps == 0 and (r // steps) % 16 == 0:
        blk, pos = (r // steps, c), lambda i, j: (i * gj + j, 0)
    elif r % gi == 0 and (r // gi) % 16 == 0 and c % gj == 0 and (c // gj) % 128 == 0:
        blk, pos = (r // gi, c // gj), lambda i, j: (i, j)
    else:
        return None
    return pl.BlockSpec((None,) + blk, lambda i, j: (layer,) + pos(i, j)), pl.BlockSpec(blk, pos)


def _ffn(x2d, gain, wg, wu, wd):
    n, d = x2d.shape
    f = wg.shape[1]
    tm = min(1024, n)
    tf = 512 if f % 512 == 0 else f
    hbm = pl.BlockSpec(memory_space=pl.ANY)
    return pl.pallas_call(
        functools.partial(_ffn_kernel, tf=tf),
        grid=(n // tm,),
        in_specs=[
            pl.BlockSpec((tm, d), lambda i: (i, 0)),
            pl.BlockSpec((1, d), lambda i: (0, 0)),
            hbm, hbm, hbm,
        ],
        out_specs=pl.BlockSpec((tm, d), lambda i: (i, 0)),
        out_shape=jax.ShapeDtypeStruct((n, d), F32),
        scratch_shapes=[pltpu.VMEM((tm, d), BF16)],
        compiler_params=pltpu.CompilerParams(
            dimension_semantics=("parallel",), vmem_limit_bytes=VMEM_LIMIT),
        name="ffn",
    )(x2d, gain.reshape(1, d), wg, wu, wd)


def _proj_kernel(x_ref, g_ref, w_ref, o_ref, h_ref):
    project = lambda h: jnp.dot(h, w_ref[...], preferred_element_type=F32).astype(o_ref.dtype)

    @pl.when(pl.program_id(1) == 0)
    def _():
        sub = x_ref.shape[0] // FFN_NORM_SPLIT
        for s in range(FFN_NORM_SPLIT):
            rows = slice(s * sub, (s + 1) * sub)
            h = _rms_rows(x_ref[rows, :], g_ref[...]).astype(BF16)
            h_ref[rows, :] = h
            o_ref[rows, :] = project(h)

    @pl.when(pl.program_id(1) > 0)
    def _():
        o_ref[...] = project(h_ref[...])


def _proj(x2d, gain, w):
    n, d = x2d.shape
    dout = w.shape[1]
    tm = min(1024, n)
    tn = dout // 2
    return pl.pallas_call(
        _proj_kernel,
        grid=(n // tm, dout // tn),
        in_specs=[
            pl.BlockSpec((tm, d), lambda i, j: (i, 0)),
            pl.BlockSpec((1, d), lambda i, j: (0, 0)),
            pl.BlockSpec((d, tn), lambda i, j: (0, j)),
        ],
        out_specs=pl.BlockSpec((tm, tn), lambda i, j: (i, j)),
        out_shape=jax.ShapeDtypeStruct((n, dout), BF16),
        scratch_shapes=[pltpu.VMEM((tm, d), BF16)],
        compiler_params=pltpu.CompilerParams(
            dimension_semantics=("parallel", "arbitrary"), vmem_limit_bytes=VMEM_LIMIT),
        name="in_proj",
    )(x2d, gain.reshape(1, d), w)


def _outproj_kernel(x_ref, ya_ref, yb_ref, yc_ref, yd_ref, w_ref, o_ref):
    gw = GROUP_WIDTH
    acc = x_ref[...]
    for m, y_ref in enumerate((ya_ref, yb_ref, yc_ref, yd_ref)):
        acc = acc + jnp.dot(y_ref[...].astype(BF16), w_ref[m * gw:(m + 1) * gw, :],
                            preferred_element_type=F32)
    o_ref[...] = acc


def _outproj(x2d, ys, w):
    n, d = x2d.shape
    tm = min(512, n)
    yspec = pl.BlockSpec((tm, GROUP_WIDTH), lambda i: (i, 0))
    return pl.pallas_call(
        _outproj_kernel,
        grid=(n // tm,),
        in_specs=[pl.BlockSpec((tm, d), lambda i: (i, 0)), yspec, yspec, yspec, yspec,
                  pl.BlockSpec(w.shape, lambda i: (0, 0))],
        out_specs=pl.BlockSpec((tm, d), lambda i: (i, 0)),
        out_shape=jax.ShapeDtypeStruct((n, d), F32),
        compiler_params=pltpu.CompilerParams(
            dimension_semantics=("parallel",), vmem_limit_bytes=VMEM_LIMIT),
        name="out_proj",
    )(x2d, *ys, w)


def _advance(fillers):
    for gen in fillers:
        if next(gen, _DONE) is not _DONE:
            return


_DONE = object()


def _rwkv_kernel(p_ref, mu_ref, w0_ref, wup_ref, a0_ref, aup_ref, gup_ref, kk_ref, ka_ref, rk_ref,
                 lnw_ref, lnb_ref, tri_ref, bd_ref, o_ref, s_ref, prev_ref, osc_ref):
    c = RWKV_CHUNK
    nck = p_ref.shape[1] // c
    prm = dict(mu=mu_ref, w0=w0_ref, wup=wup_ref, a0=a0_ref, aup=aup_ref, gup=gup_ref, kk=kk_ref, ka=ka_ref,
               rk=rk_ref, lnw=lnw_ref, lnb=lnb_ref, tri=tri_ref, bd=bd_ref)
    chunk = lambda ref, ck: ref.at[:, pl.ds(ck * c, c), :]
    preps = [dict() for _ in range(nck)]
    for _ in _rwkv_prologue(chunk(p_ref, 0), prm, prev_ref, preps[0]):
        pass
    epilogue = iter(())
    for ck in range(nck):
        fillers = [epilogue]
        if ck + 1 < nck:
            fillers.append(_rwkv_prologue(chunk(p_ref, ck + 1), prm, prev_ref, preps[ck + 1]))
        y = _rwkv_solve(preps[ck], s_ref, functools.partial(_advance, fillers))
        for gen in fillers:
            for _ in gen:
                pass
        epilogue = _rwkv_epilogue(preps[ck], y, prm, chunk(o_ref, ck), osc_ref)
    for _ in epilogue:
        pass


def _rwkv_prologue(p_ref, prm, prev_ref, out):
    nb, c, _ = p_ref.shape
    gw = GROUP_WIDTH
    n = HEAD_DIM
    nh = gw // n

    p = p_ref[...].astype(F32).reshape(nb * c, RWKV_IN)
    row = lax.broadcasted_iota(jnp.int32, p.shape, 0)
    shifted = pltpu.roll(p, 1, axis=0)
    for s in range(nb):
        shifted = jnp.where(row == s * c, prev_ref[s], shifted)
        prev_ref[s] = p[(s + 1) * c - 1:(s + 1) * c, :]
    p = p + (shifted - p) * prm["mu"][...]
    yield

    r = p[:, 0:gw]
    k = p[:, gw:2 * gw]
    v = p[:, 2 * gw:3 * gw]
    o1 = 3 * gw
    w_lo = p[:, o1:o1 + RWKV_W_RANK]
    a_lo = p[:, o1 + RWKV_W_RANK:o1 + RWKV_W_RANK + RWKV_A_RANK]
    g_lo = p[:, o1 + RWKV_W_RANK + RWKV_A_RANK:]

    z = -(prm["w0"][...] + _mm(jnp.tanh(w_lo), prm["wup"][...]))
    softplus = jnp.maximum(z, 0.0) + jnp.log1p(jnp.exp(-jnp.abs(z)))
    lw = -jnp.exp(-softplus - 0.5)
    yield
    a = jax.nn.sigmoid(prm["a0"][...] + _mm(a_lo, prm["aup"][...]))
    g = _mm(jax.nn.sigmoid(g_lo), prm["gup"][...])
    yield

    kk = k * prm["kk"][...]
    kk = kk * lax.rsqrt(jnp.maximum(_head_sum(kk * kk, prm["bd"][...]), 1e-24))
    yield
    k = k * (1.0 + (a - 1.0) * prm["ka"][...])
    b = kk * a
    cum = _const_lhs_mm(prm["tri"][...], lw)
    yield
    cum_last = jnp.concatenate(
        [jnp.broadcast_to(cum[(s + 1) * c - 1:(s + 1) * c, :], (c, gw)) for s in range(nb)], axis=0)
    heads = lambda t, rows=c: jnp.stack(
        [t[s * c:s * c + rows, h * n:(h + 1) * n] for s in range(nb) for h in range(nh)], axis=0)
    mxu_heads = lambda t: heads(t.astype(BF16))
    out.update(r=r, k=k, v=v, g=g)
    e_neg = jnp.exp(-cum)
    out["bd_h"] = mxu_heads(b * e_neg)
    yield
    out["kd_h"] = mxu_heads(k * e_neg)
    yield
    out["kq_h"] = mxu_heads(kk * jnp.exp(cum - lw))
    yield
    out["rq_h"] = mxu_heads(r * jnp.exp(cum))
    yield
    out["v_h"] = mxu_heads(v)
    yield
    e_end = jnp.exp(cum_last - cum)
    out["be_h"] = mxu_heads(b * e_end)
    yield
    out["ke_h"] = mxu_heads(k * e_end)
    out["ge_h"] = heads(jnp.exp(cum_last), 1)
    yield


def _rwkv_solve(pp, s_ref, tick):
    kq_h, rq_h, bd_h, kd_h, v_h = pp["kq_h"], pp["rq_h"], pp["bd_h"], pp["kd_h"], pp["v_h"]
    c = kq_h.shape[1]
    ri = lax.broadcasted_iota(jnp.int32, (c, c), 0)
    ci = lax.broadcasted_iota(jnp.int32, (c, c), 1)
    strict = ri > ci
    incl = ri >= ci
    eye = (ri == ci).astype(F32)

    s0 = s_ref[...]
    gmat = _bmm(jnp.concatenate([kq_h, rq_h], axis=1), jnp.concatenate([bd_h, kd_h], axis=1), _BNT)
    tick()
    a_bb = jnp.where(strict, gmat[:, :c, :c], 0.0)
    a_bk = jnp.where(strict, gmat[:, :c, c:], 0.0)
    a_rb = jnp.where(incl, gmat[:, c:, :c], 0.0)
    a_rk = jnp.where(incl, gmat[:, c:, c:], 0.0)
    pw = -a_bb
    t_inv = eye + pw
    for _ in range(int(np.log2(c)) - 1):
        pw = _bmm(pw, pw)
        tick()
        t_inv = t_inv + _bmm(t_inv, pw)
        tick()
    x = _bmm(kq_h, s0, _BNT)
    tick()
    x = x + _bmm(a_bk, v_h)
    tick()
    u = -_bmm(t_inv, x)
    tick()
    y = _bmm(rq_h, s0, _BNT)
    tick()
    y = y + _bmm(a_rb, u)
    tick()
    y = y + _bmm(a_rk, v_h)
    tick()
    s_ref[...] = s0 * pp["ge_h"] + _bmm(jnp.concatenate([u.astype(BF16), v_h], axis=1),
                                        jnp.concatenate([pp["be_h"], pp["ke_h"]], axis=1), _BTN)
    tick()
    return y


def _rwkv_epilogue(pp, y, prm, o_ref, osc_ref):
    nb, c, gw = o_ref.shape
    n = HEAD_DIM
    nh = gw // n
    ones_bd = prm["bd"][...]
    for s in range(nb):
        for h in range(nh):
            osc_ref[s * c:(s + 1) * c, h * n:(h + 1) * n] = y[s * nh + h]
    yield
    o = osc_ref[...]
    inv_n = 1.0 / n
    mean = _head_sum(o, ones_bd) * inv_n
    yield
    dlt = o - mean
    var = _head_sum(dlt * dlt, ones_bd) * inv_n
    yield
    o = dlt * lax.rsqrt(var + RWKV_LN_EPS) * prm["lnw"][...] + prm["lnb"][...]
    yield
    bonus = _head_sum(pp["r"] * pp["k"] * prm["rk"][...], ones_bd) * pp["v"]
    yield
    o_ref[...] = ((o + bonus) * pp["g"]).reshape(nb, c, gw).astype(o_ref.dtype)
    yield


def _full_spec(t):
    return pl.BlockSpec(t.shape, lambda i, j: (0,) * t.ndim)


def _mixer_call(part, bsz, seq, nb, tblk, name, riders=()):
    body, in_specs, operands, scratch = part
    gi, gj = bsz // nb, seq // tblk
    specs = [_rider_specs(w.shape, layer, gi, gj) for w, layer in riders]
    ride = [k for k, s in enumerate(specs) if s is not None]
    n_in, n_ride = len(operands), len(ride)

    def kern(*refs):
        ins, rider_in = refs[:n_in], refs[n_in:n_in + n_ride]
        out, rider_out = refs[n_in + n_ride], refs[n_in + n_ride + 1:n_in + 2 * n_ride + 1]
        scr = refs[n_in + 2 * n_ride + 1:]

        @pl.when(pl.program_id(1) == 0)
        def _():
            for ref in scr:
                ref[...] = jnp.zeros_like(ref)

        body(*ins, out, *scr)
        for src, dst in zip(rider_in, rider_out):
            dst[...] = src[...].astype(BF16)

    out_spec = pl.BlockSpec((nb, tblk, GROUP_WIDTH), lambda i, j: (i, j, 0))
    outs = pl.pallas_call(
        kern,
        grid=(gi, gj),
        in_specs=list(in_specs) + [specs[k][0] for k in ride],
        out_specs=[out_spec] + [specs[k][1] for k in ride],
        out_shape=[jax.ShapeDtypeStruct((bsz, seq, GROUP_WIDTH), BF16)]
        + [jax.ShapeDtypeStruct(riders[k][0].shape[1:], BF16) for k in ride],
        scratch_shapes=scratch,
        compiler_params=pltpu.CompilerParams(
            dimension_semantics=("parallel", "arbitrary"), vmem_limit_bytes=VMEM_LIMIT),
        name=name,
    )(*operands, *[riders[k][0] for k in ride])
    cast = [outs[1 + ride.index(k)] if k in ride else w[layer].astype(BF16)
            for k, (w, layer) in enumerate(riders)]
    return outs[0], cast


def _rwkv_part(p3, nb, tblk, mu, w0, w_up, a0, a_up, g_up, k_k, k_a, r_k, ln_w, ln_b):
    c = RWKV_CHUNK
    gw = GROUP_WIDTH
    tri = jnp.asarray(np.kron(np.eye(nb, dtype=np.float32), np.tril(np.ones((c, c), np.float32))), BF16)
    ones_bd = jnp.asarray(np.kron(np.eye(4, dtype=np.float32), np.ones((HEAD_DIM, HEAD_DIM), np.float32)), BF16)
    row = lambda t: t.reshape(1, -1)
    params = [row(mu), row(w0), w_up, row(a0), a_up, g_up, row(k_k), row(k_a), row(r_k), row(ln_w),
              row(ln_b), tri, ones_bd]
    in_specs = [pl.BlockSpec((nb, tblk, RWKV_IN), lambda i, j: (i, j, 0))] + [_full_spec(t) for t in params]
    scratch = [pltpu.VMEM((nb * gw // HEAD_DIM, HEAD_DIM, HEAD_DIM), F32),
               pltpu.VMEM((nb, 1, RWKV_IN), F32),
               pltpu.VMEM((nb * c, gw), F32)]
    return _rwkv_kernel, in_specs, [p3] + params, scratch


def _rwkv(p3, *params, riders=()):
    bsz, seq, _ = p3.shape
    nb = MIXER_SEQS if bsz % MIXER_SEQS == 0 else 1
    tblk = min(RWKV_BLOCK, seq)
    return _mixer_call(_rwkv_part(p3, nb, tblk, *params), bsz, seq, nb, tblk, "rwkv7", riders)


def _rope_table_kernel(pos_ref, invf_ref, *rest):
    n_riders = (len(rest) - 2) // 2
    cos_ref, sin_ref = rest[n_riders], rest[n_riders + 1]
    ang = pos_ref[...].astype(F32) * invf_ref[...]
    cos_ref[...] = jnp.cos(ang)
    sin_ref[...] = jnp.sin(ang)
    for src, dst in zip(rest[:n_riders], rest[n_riders + 2:]):
        dst[...] = src[...].astype(BF16)


def _rope_tables(positions, riders=()):
    bsz, seq = positions.shape
    d = HEAD_DIM
    w = 2 * d
    inv_freq = ROPE_THETA ** (-jnp.arange(0, ROPE_DIM, 2, dtype=F32) / ROPE_DIM)
    lane_d = np.arange(w) % d
    invf = jnp.where(lane_d < ROPE_DIM, inv_freq[lane_d % (ROPE_DIM // 2)], 0.0).reshape(1, w)
    tr = min(1024, bsz * seq)
    steps = bsz * seq // tr
    specs = [_rider_specs(wt.shape, layer, steps, 1) for wt, layer in riders]
    ride = [k for k, s in enumerate(specs) if s is not None]
    table = pl.BlockSpec((tr, w), lambda i, j: (i, 0))
    outs = pl.pallas_call(
        _rope_table_kernel,
        grid=(steps, 1),
        in_specs=[pl.BlockSpec((tr, 1), lambda i, j: (i, 0)), pl.BlockSpec((1, w), lambda i, j: (0, 0))]
        + [specs[k][0] for k in ride],
        out_specs=[table, table] + [specs[k][1] for k in ride],
        out_shape=[jax.ShapeDtypeStruct((bsz * seq, w), F32)] * 2
        + [jax.ShapeDtypeStruct(riders[k][0].shape[1:], BF16) for k in ride],
        compiler_params=pltpu.CompilerParams(
            dimension_semantics=("parallel", "arbitrary"), vmem_limit_bytes=VMEM_LIMIT),
        name="rope_tables",
    )(positions.reshape(bsz * seq, 1), invf, *[riders[k][0] for k in ride])
    cast = [outs[2 + ride.index(k)] if k in ride else wt[layer].astype(BF16)
            for k, (wt, layer) in enumerate(riders)]
    return outs[0], outs[1], cast


def _attn_kernel(cos_ref, sin_ref, q0_ref, q1_ref, kv_ref, qn_ref, kn_ref, sink_ref, bd_ref,
                 o_ref, kprev_ref, vprev_ref):
    nb, blk, _ = q0_ref.shape
    rows = nb * blk
    d = HEAD_DIM
    half = ROPE_DIM // 2
    group = ATTN_HEADS // ATTN_KV_HEADS
    kvw = ATTN_KV_HEADS * d
    first = pl.program_id(1) == 0
    ones_bd = bd_ref[...]
    cos = cos_ref[...].reshape(rows, 2 * d)
    sin = sin_ref[...].reshape(rows, 2 * d)

    def norm_rope(x, gain):
        w = x.shape[1]
        x = x * lax.rsqrt(_head_sum(x * x, ones_bd[:w, :w]) * (1.0 / d) + NORM_EPS) * gain
        ld = lax.broadcasted_iota(jnp.int32, (rows, w), 1) % d
        rot = jnp.where(ld < half, -pltpu.roll(x, w - half, axis=1),
                        jnp.where(ld < ROPE_DIM, pltpu.roll(x, half, axis=1), 0.0))
        tile = lambda t: t if w == t.shape[1] else jnp.concatenate([t] * (w // t.shape[1]), axis=1)
        return x * tile(cos) + rot * tile(sin)

    qs = [norm_rope(q_ref[...].astype(F32).reshape(rows, ATTN_COL_BLOCK), qn_ref[...]) * (d ** -0.5)
          for q_ref in (q0_ref, q1_ref)]
    kv = kv_ref[...].astype(F32).reshape(rows, ATTN_COL_BLOCK)
    k_cur = norm_rope(kv[:, :kvw], kn_ref[:, :kvw])
    v_cur = kv[:, kvw:]
    k_prev = kprev_ref[...]
    v_prev = vprev_ref[...]

    heads_per_ref = ATTN_COL_BLOCK // d
    q_head = lambda s, qh: qs[qh // heads_per_ref][s * blk:(s + 1) * blk,
                                                   (qh % heads_per_ref) * d:(qh % heads_per_ref + 1) * d]
    q_g = jnp.stack([jnp.concatenate([q_head(s, kh * group + i) for i in range(group)], axis=0)
                     for s in range(nb) for kh in range(ATTN_KV_HEADS)], axis=0)
    kv_heads = lambda t: jnp.stack([t[s * blk:(s + 1) * blk, kh * d:(kh + 1) * d]
                                    for s in range(nb) for kh in range(ATTN_KV_HEADS)], axis=0)

    ri = lax.broadcasted_iota(jnp.int32, (group * blk, blk), 0) % blk
    ci = lax.broadcasted_iota(jnp.int32, (group * blk, blk), 1)
    neg_inf = -jnp.inf
    no_prev = jnp.where(first, neg_inf, 0.0)
    s_prev = jnp.where(ci > ri, _bmm(q_g, kv_heads(k_prev), _BNT), neg_inf) + no_prev
    s_cur = jnp.where(ci <= ri, _bmm(q_g, kv_heads(k_cur), _BNT), neg_inf)
    sink = sink_ref[...]
    m = jnp.maximum(jnp.maximum(jnp.max(s_prev, axis=2, keepdims=True),
                                jnp.max(s_cur, axis=2, keepdims=True)), sink)
    p_prev = jnp.exp(s_prev - m)
    p_cur = jnp.exp(s_cur - m)
    ones = jnp.ones((nb * ATTN_KV_HEADS, blk, d), BF16)
    den = _bmm(p_prev, ones) + _bmm(p_cur, ones) + jnp.exp(sink - m)
    o = ((_bmm(p_prev, kv_heads(v_prev)) + _bmm(p_cur, kv_heads(v_cur))) / den).astype(o_ref.dtype)
    for s in range(nb):
        for qh in range(ATTN_HEADS):
            o_ref[s, :, qh * d:(qh + 1) * d] = o[s * ATTN_KV_HEADS + qh // group,
                                                 (qh % group) * blk:(qh % group + 1) * blk, :]

    kprev_ref[...] = k_cur
    vprev_ref[...] = v_cur


def _attn_part(p3, nb, rope_cos, rope_sin, q_norm, k_norm, sinks):
    bsz, seq, _ = p3.shape
    blk = ATTN_BLOCK
    cw = ATTN_COL_BLOCK
    d = HEAD_DIM
    qn = jnp.tile(q_norm.astype(F32), cw // d).reshape(1, cw)
    kn = jnp.tile(k_norm.astype(F32), cw // d).reshape(1, cw)
    ones_bd = jnp.asarray(np.kron(np.eye(cw // d, dtype=np.float32), np.ones((d, d), np.float32)), BF16)
    group = ATTN_HEADS // ATTN_KV_HEADS
    sink_col = jnp.repeat(sinks.astype(F32).reshape(ATTN_KV_HEADS, group), blk, axis=1)[:, :, None]
    sink_col = jnp.tile(sink_col, (nb, 1, 1))
    tw = rope_cos.shape[-1]
    cos3 = rope_cos.reshape(bsz, seq, tw)
    sin3 = rope_sin.reshape(bsz, seq, tw)
    col = lambda o: pl.BlockSpec((nb, blk, cw), lambda i, j: (i, j, ATTN_OFF_BLOCKS + o))
    trig = pl.BlockSpec((nb, blk, tw), lambda i, j: (i, j, 0))
    consts = [qn, kn, sink_col, ones_bd]
    in_specs = [trig, trig, col(0), col(1), col(2)] + [_full_spec(t) for t in consts]
    scratch = [pltpu.VMEM((nb * blk, ATTN_KV_HEADS * d), F32), pltpu.VMEM((nb * blk, ATTN_KV_HEADS * d), F32)]
    return _attn_kernel, in_specs, [cos3, sin3, p3, p3, p3] + consts, scratch


def _attn(p3, *params, riders=()):
    bsz, seq, _ = p3.shape
    nb = ATTN_SEQS if bsz % ATTN_SEQS == 0 else 1
    return _mixer_call(_attn_part(p3, nb, *params), bsz, seq, nb, ATTN_BLOCK, "swa_attn", riders)


def _s5_prep_kernel(lre_ref, lim_ref, ls_ref, bre_ref, bim_ref, are_ref, aim_ref, obre_ref, obim_ref):
    lre = lre_ref[...]
    lim = lim_ref[...]
    dt = jnp.exp(ls_ref[...])
    mag = jnp.exp(lre * dt)
    are = mag * jnp.cos(lim * dt)
    aim = mag * jnp.sin(lim * dt)
    are_ref[...] = are
    aim_ref[...] = aim
    inv = 1.0 / (lre * lre + lim * lim)
    cre = ((are - 1.0) * lre + aim * lim) * inv
    cim = (aim * lre - (are - 1.0) * lim) * inv
    bre = bre_ref[...]
    bim = bim_ref[...]
    obre_ref[...] = cre[:, None, :] * bre - cim[:, None, :] * bim
    obim_ref[...] = cre[:, None, :] * bim + cim[:, None, :] * bre


def _s5_kernel(u_ref, are_ref, aim_ref, b_ref, c_ref, d_ref, gw_ref, gb_ref, o_ref, x_ref, s_ref):
    bsz, tc, ch = u_ref.shape
    hw = S5_WIDTH // 2
    hc = ch // 2

    @pl.when(pl.program_id(0) == 0)
    def _():
        s_ref[...] = jnp.zeros_like(s_ref)

    u = jnp.swapaxes(u_ref[...].astype(F32), 0, 1).reshape(tc * bsz, ch)

    def scan(hf):
        xh = x_ref.at[hf]
        for s0 in range(0, hw, S5_STRIP):
            re = slice(s0, s0 + S5_STRIP)
            im = slice(hw + s0, hw + s0 + S5_STRIP)
            lam = slice(hw * hf + s0, hw * hf + s0 + S5_STRIP)
            ar = jnp.broadcast_to(are_ref[:, lam], (bsz, S5_STRIP))
            ai = jnp.broadcast_to(aim_ref[:, lam], (bsz, S5_STRIP))
            sre = slice(2 * hw * hf + s0, 2 * hw * hf + s0 + S5_STRIP)
            sim = slice(2 * hw * hf + hw + s0, 2 * hw * hf + hw + s0 + S5_STRIP)
            sr, si = s_ref[:, sre], s_ref[:, sim]
            for t in range(tc):
                rows = slice(t * bsz, (t + 1) * bsz)
                sr, si = ar * sr - ai * si + xh[rows, re], ar * si + ai * sr + xh[rows, im]
                xh[rows, re] = sr
                xh[rows, im] = si
            s_ref[:, sre] = sr
            s_ref[:, sim] = si

    x_ref[0] = _mm(u[:, :hc], b_ref[0])
    x_ref[1] = _mm(u[:, hc:], b_ref[1])
    scan(0)
    y0 = _mm(x_ref[0], c_ref[0])
    scan(1)
    y = jnp.concatenate([y0, _mm(x_ref[1], c_ref[1])], axis=1)
    y = y + d_ref[...] * u
    z = 0.5 * y * (1.0 + lax.erf(y * (2.0 ** -0.5)))
    out = z * jax.nn.sigmoid(_mm(z, gw_ref[...]) + gb_ref[...])
    o_ref[...] = jnp.swapaxes(out.reshape(tc, bsz, ch), 0, 1).astype(o_ref.dtype)


def _s5(p3, lam_re, lam_im, log_step, b_re, b_im, c_re, c_im, d_skip, glu_w, glu_b):
    bsz, seq, _ = p3.shape
    g, st, ch = S5_GROUPS, S5_STATE, S5_GROUP
    gwd = GROUP_WIDTH
    vm = pl.BlockSpec(memory_space=pltpu.VMEM)
    a_re, a_im, bb_re, bb_im = pl.pallas_call(
        _s5_prep_kernel,
        in_specs=[vm] * 5,
        out_specs=[vm] * 4,
        out_shape=[jax.ShapeDtypeStruct((g, st), F32)] * 2 + [jax.ShapeDtypeStruct((g, ch, st), F32)] * 2,
        name="s5_prep",
    )(lam_re, lam_im, log_step.reshape(g, 1), jnp.swapaxes(b_re, 1, 2), jnp.swapaxes(b_im, 1, 2))

    gh = g // 2
    eye = jnp.eye(gh, dtype=F32)
    bd_in = lambda t: (t[:, :, None, :] * eye[:, None, :, None]).reshape(gh * ch, gh * st)
    bd_out = lambda t: (jnp.swapaxes(t, 1, 2)[:, :, None, :] * eye[:, None, :, None]).reshape(gh * st, gh * ch)
    halves = lambda t: (t[:gh], t[gh:])
    b_mat = jnp.stack([jnp.concatenate([bd_in(r), bd_in(i)], axis=1)
                       for r, i in zip(halves(bb_re), halves(bb_im))]).astype(BF16)
    c_mat = jnp.stack([jnp.concatenate([bd_out(r), -bd_out(i)], axis=0)
                       for r, i in zip(halves(c_re.astype(F32)), halves(c_im.astype(F32)))]).astype(BF16)

    tc = min(64, seq)
    full = lambda t: pl.BlockSpec(t.shape, lambda i: (0,) * t.ndim)
    params = [a_re.reshape(1, S5_WIDTH), a_im.reshape(1, S5_WIDTH), b_mat, c_mat,
              d_skip.reshape(1, gwd), glu_w.astype(BF16), glu_b.reshape(1, gwd)]
    return pl.pallas_call(
        _s5_kernel,
        grid=(seq // tc,),
        in_specs=[pl.BlockSpec((bsz, tc, gwd), lambda i: (0, i, S5_OFF_BLOCKS))] + [full(t) for t in params],
        out_specs=pl.BlockSpec((bsz, tc, gwd), lambda i: (0, i, 0)),
        out_shape=jax.ShapeDtypeStruct((bsz, seq, gwd), BF16),
        scratch_shapes=[pltpu.VMEM((2, tc * bsz, S5_WIDTH), F32), pltpu.VMEM((bsz, 2 * S5_WIDTH), F32)],
        compiler_params=pltpu.CompilerParams(
            dimension_semantics=("arbitrary",), vmem_limit_bytes=VMEM_LIMIT),
        name="s5_scan",
    )(p3, *params)


def _hgrn_kernel(qf_ref, ig_ref, lbraw_ref, gn_ref, tri_ref, bd_ref, o_ref, s_ref, osc_ref, *, layer):
    nb, tb, _ = qf_ref.shape
    cz = HGRN_CHUNK
    gw = GROUP_WIDTH
    n = HEAD_DIM
    nh = gw // n

    qf = qf_ref[...].astype(F32).reshape(nb * tb, 2 * gw)
    ig = ig_ref[...].astype(F32).reshape(nb * tb, 2 * gw)
    q, f = qf[:, :gw], qf[:, gw:]
    v, g = ig[:, :gw], ig[:, gw:]

    lbr = lbraw_ref[...]
    e = jnp.exp(lbr - jnp.max(lbr, axis=0, keepdims=True))
    sm = e / jnp.sum(e, axis=0, keepdims=True)
    lb = jnp.zeros((1, gw), F32)
    for i in range(1, layer + 1):
        lb = lb + sm[i:i + 1, :]

    q = _silu(q) * (n ** -0.5)
    f_gate = lb + (1.0 - lb) * jax.nn.sigmoid(f)
    log_f = jnp.log(f_gate)
    k = 1.0 - f_gate

    parts = _split2(log_f)

    def per_seq(c_ref):
        cm = c_ref[...]
        seq_rows = lambda t, s: t[s * tb:(s + 1) * tb, :]
        return jnp.concatenate(
            [_dot(cm, seq_rows(parts[0], s), _NN) + _dot(cm, seq_rows(parts[1], s), _NN)
             for s in range(nb)], axis=0)

    bcum = per_seq(tri_ref)
    b3 = bcum.reshape(nb * tb // cz, cz, gw)
    chunk_row = lambda r: jnp.broadcast_to(b3[:, r:r + 1, :], b3.shape).reshape(nb * tb, gw)
    bmid = chunk_row(cz // 2 - 1)
    blast = chunk_row(cz - 1)
    qe = q * jnp.exp(bcum - bmid)
    ke = k * jnp.exp(bmid - bcum)
    kl = k * jnp.exp(blast - bcum)
    qb = q * jnp.exp(bcum)
    dec = jnp.exp(blast)

    ri = lax.broadcasted_iota(jnp.int32, (tb, tb), 0)
    ci = lax.broadcasted_iota(jnp.int32, (tb, tb), 1)
    mask = jnp.logical_and(ri // cz == ci // cz, ri >= ci)

    heads = lambda t: jnp.stack(
        [t[s * tb:(s + 1) * tb, h * n:(h + 1) * n] for s in range(nb) for h in range(nh)], axis=0)
    v_h, qb_h, kl_h, dec_h = heads(v), heads(qb), heads(kl), heads(dec)
    att = jnp.where(mask, _bmm(heads(qe), heads(ke), _BNT), 0.0)
    o_intra = _bmm(att, v_h)
    nchunk = tb // cz
    ti = lax.broadcasted_iota(jnp.int32, (tb, nchunk * n), 0)
    li = lax.broadcasted_iota(jnp.int32, (tb, nchunk * n), 1)
    kl_spread = jnp.where(ti // cz == li // n, jnp.concatenate([kl_h] * nchunk, axis=2), 0.0)
    kv_all = _bmm(v_h, kl_spread, _BTN)
    st = s_ref[...]
    o_inter = []
    for j in range(nchunk):
        rows = slice(j * cz, (j + 1) * cz)
        o_inter.append(_bmm(qb_h[:, rows, :], st, _BNT))
        st = st * dec_h[:, j * cz:j * cz + 1, :] + kv_all[:, :, j * n:(j + 1) * n]
    s_ref[...] = st
    o_heads = o_intra + jnp.concatenate(o_inter, axis=1)
    for s in range(nb):
        for h in range(nh):
            osc_ref[s * tb:(s + 1) * tb, h * n:(h + 1) * n] = o_heads[s * nh + h]

    o = osc_ref[...]
    ms = _head_sum(o * o, bd_ref[...]) * (1.0 / n)
    o_ref[...] = (o * lax.rsqrt(ms + NORM_EPS) * gn_ref[...] * _silu(g)).reshape(nb, tb, gw).astype(o_ref.dtype)


def _hgrn_part(p3, nb, lower_bounds, g_norm, layer):
    tb = HGRN_TILE
    cz = HGRN_CHUNK
    gw = GROUP_WIDTH
    idx = np.arange(tb)
    same = (idx[:, None] // cz) == (idx[None, :] // cz)
    tri = same & (idx[:, None] >= idx[None, :])
    consts = [jnp.asarray(tri.astype(np.float32), BF16)]
    ones_bd = jnp.asarray(np.kron(np.eye(4, dtype=np.float32), np.ones((HEAD_DIM, HEAD_DIM), np.float32)), BF16)
    gn = jnp.tile(g_norm.astype(F32), gw // HEAD_DIM).reshape(1, gw)
    params = [lower_bounds.astype(F32), gn] + consts + [ones_bd]
    col = lambda o: pl.BlockSpec((nb, tb, 2 * gw), lambda i, j: (i, j, HGRN_OFF_BLOCKS + o))
    in_specs = [col(0), col(1)] + [_full_spec(t) for t in params]
    scratch = [pltpu.VMEM((nb * gw // HEAD_DIM, HEAD_DIM, HEAD_DIM), F32), pltpu.VMEM((nb * tb, gw), F32)]
    return functools.partial(_hgrn_kernel, layer=layer), in_specs, [p3, p3] + params, scratch


def _hgrn(p3, *params, riders=()):
    bsz, seq, _ = p3.shape
    nb = HGRN_SEQS if bsz % HGRN_SEQS == 0 else 1
    return _mixer_call(_hgrn_part(p3, nb, *params), bsz, seq, nb, HGRN_TILE, "hgrn2", riders)


def kernel(x, positions, ffn1_norm, ffn1_w_gate, ffn1_w_up, ffn1_w_down, mix_norm, w_in, rwkv_mu, rwkv_w0, rwkv_w_up, rwkv_a0, rwkv_a_up, rwkv_g_up, rwkv_k_k, rwkv_k_a, rwkv_r_k, rwkv_ln_w, rwkv_ln_b, attn_q_norm, attn_k_norm, attn_sinks, s5_lambda_re, s5_lambda_im, s5_log_step, s5_b_re, s5_b_im, s5_c_re, s5_c_im, s5_d, s5_glu_w, s5_glu_b, hgrn_lower_bounds, hgrn_g_norm, w_out, ffn2_norm, ffn2_w_gate, ffn2_w_up, ffn2_w_down):
    bsz, seq, d = x.shape
    depth = w_in.shape[0]
    n = bsz * seq
    xf = x.reshape(n, d)
    rope_cos, rope_sin, (g1, u1, d1, w_in16) = _rope_tables(
        positions, riders=[(ffn1_w_gate, 0), (ffn1_w_up, 0), (ffn1_w_down, 0), (w_in, 0)])
    for l in range(depth):
        more = l + 1 < depth
        xf = _ffn(xf, ffn1_norm[l], g1, u1, d1)
        p3 = _proj(xf, mix_norm[l], w_in16).reshape(bsz, seq, D_IN)
        y_a, (w_out16, g2, u2) = _rwkv(
            p3, rwkv_mu[l], rwkv_w0[l], rwkv_w_up[l], rwkv_a0[l], rwkv_a_up[l], rwkv_g_up[l],
            rwkv_k_k[l], rwkv_k_a[l], rwkv_r_k[l].reshape(-1), rwkv_ln_w[l], rwkv_ln_b[l],
            riders=[(w_out, l), (ffn2_w_gate, l), (ffn2_w_up, l)])
        y_b, cast_b = _attn(p3, rope_cos, rope_sin, attn_q_norm[l], attn_k_norm[l], attn_sinks[l],
                            riders=[(ffn2_w_down, l)] + ([(ffn1_w_down, l + 1)] if more else []))
        y_c = _s5(p3, s5_lambda_re[l], s5_lambda_im[l], s5_log_step[l], s5_b_re[l], s5_b_im[l],
                  s5_c_re[l], s5_c_im[l], s5_d[l], s5_glu_w[l], s5_glu_b[l])
        y_d, cast_d = _hgrn(p3, hgrn_lower_bounds, hgrn_g_norm[l], l,
                            riders=[(w_in, l + 1), (ffn1_w_gate, l + 1), (ffn1_w_up, l + 1)] if more else [])
        ys = [t.reshape(n, GROUP_WIDTH) for t in (y_a, y_b, y_c, y_d)]
        xf = _outproj(xf, ys, w_out16)
        xf = _ffn(xf, ffn2_norm[l], g2, u2, cast_b[0])
        if more:
            d1 = cast_b[1]
            w_in16, g1, u1 = cast_d
    return xf.reshape(bsz, seq, d)
```
